```python
import jax, jax.numpy as jnp
from jax import lax
import numpy as np

D_MODEL = 2048
BATCH = 4
SEQ = 4096
DEPTH = 1

HEAD_DIM = 128
MIX_WIDTH = D_MODEL
N_HEADS = MIX_WIDTH // HEAD_DIM
N_HEADS_MOBA = N_HEADS // 2
N_HEADS_DIL = N_HEADS - N_HEADS_MOBA
W_MOBA = N_HEADS_MOBA * HEAD_DIM
W_DIL = N_HEADS_DIL * HEAD_DIM
QKV_SPLITS = (W_MOBA, 2 * W_MOBA, 3 * W_MOBA, 3 * W_MOBA + W_DIL, 3 * W_MOBA + 2 * W_DIL)
ROT_DIM = HEAD_DIM // 4
ROPE_THETA = 500000.0
MOBA_BLOCK = 256
MOBA_TOPK = 3
MOBA_QCHUNK = 32
DIL_PAIRS = ((128, 1), (512, 4), (2048, 16))
DIL_BLOCK = 128
D_FF = 4 * D_MODEL
CONV_WIDTH = 3
RMS_EPS = 1e-6
SCALE = HEAD_DIM ** -0.5
NEG = -1e30

kernel_name = 'hymba_moba_dilated_convffn_block'


def rms_norm(x, g):
    xf = x.astype(jnp.float32)
    y = xf * lax.rsqrt(jnp.mean(xf * xf, axis=-1, keepdims=True) + RMS_EPS)
    return (y * g.astype(jnp.float32)).astype(x.dtype)


def partial_rope(t, positions):
    half = ROT_DIM // 2
    inv_freq = ROPE_THETA ** (-jnp.arange(half, dtype=jnp.float32) / half)
    ang = positions.astype(jnp.float32)[..., None] * inv_freq
    cos = jnp.cos(ang)[:, :, None, :]
    sin = jnp.sin(ang)[:, :, None, :]
    tr = t[..., :ROT_DIM].astype(jnp.float32)
    t1, t2 = tr[..., :half], tr[..., half:]
    rot = jnp.concatenate([t1 * cos - t2 * sin, t2 * cos + t1 * sin], axis=-1).astype(t.dtype)
    return jnp.concatenate([rot, t[..., ROT_DIM:]], axis=-1)


def moba_attention(q, k, v):
    B, S, H, Dh = q.shape
    nb = -(-S // MOBA_BLOCK)
    Sp = nb * MOBA_BLOCK
    k_top = min(MOBA_TOPK, nb - 1)
    pad = ((0, 0), (0, Sp - S), (0, 0), (0, 0))
    kbh = jnp.pad(k, pad).reshape(B, nb, MOBA_BLOCK, H, Dh).transpose(0, 3, 1, 2, 4)
    vbh = jnp.pad(v, pad).reshape(B, nb, MOBA_BLOCK, H, Dh).transpose(0, 3, 1, 2, 4)
    q_pos = jnp.arange(S, dtype=jnp.int32)
    q_blk = q_pos // MOBA_BLOCK
    qh = q.transpose(0, 2, 1, 3)
    own = jnp.broadcast_to(q_blk[None, None, :, None], (B, H, S, 1))
    own_valid = jnp.ones((B, H, S, 1), dtype=bool)
    if k_top > 0:
        k_mean = jnp.mean(kbh.astype(jnp.float32), axis=3)
        gate = jnp.einsum('bhsd,bhnd->bhsn', qh.astype(jnp.float32), k_mean)
        fully_past = jnp.arange(nb)[None, :] < q_blk[:, None]
        gate = jnp.where(fully_past, gate, NEG)
        _, top_idx = lax.top_k(gate, k_top)
        top_idx = top_idx.astype(jnp.int32)
        sel = jnp.concatenate([top_idx, own], axis=-1)
        valid = jnp.concatenate([top_idx < q_blk[:, None], own_valid], axis=-1)
    else:
        sel, valid = own, own_valid
    nsel = sel.shape[-1]
    C = MOBA_QCHUNK
    NC = S // C

    def to_chunks(t):
        return jnp.moveaxis(t.reshape((B, H, NC, C) + t.shape[3:]), 2, 0)

    bi = jnp.arange(B)[:, None, None, None]
    hi = jnp.arange(H)[None, :, None, None]

    def chunk(args):
        qc, selc, validc, posc = args
        kg = kbh[bi, hi, selc]
        vg = vbh[bi, hi, selc]
        s = jnp.einsum('bhcd,bhcjkd->bhcjk', qc, kg, preferred_element_type=jnp.float32) * SCALE
        key_pos = selc[..., None] * MOBA_BLOCK + jnp.arange(MOBA_BLOCK, dtype=jnp.int32)
        mask = validc[..., None] & (key_pos <= posc[None, None, :, None, None])
        s = jnp.where(mask, s, NEG).reshape(B, H, C, nsel * MOBA_BLOCK)
        p = jax.nn.softmax(s, axis=-1).reshape(B, H, C, nsel, MOBA_BLOCK)
        return jnp.einsum('bhcjk,bhcjkd->bhcd', p.astype(vg.dtype), vg,
                          preferred_element_type=jnp.float32)

    out = lax.map(chunk, (to_chunks(qh), to_chunks(sel), to_chunks(valid), q_pos.reshape(NC, C)))
    return jnp.moveaxis(out, 0, 2).reshape(B, H, S, Dh).transpose(0, 2, 1, 3)


def dilated_branch(q, k, v, window, dilation):
    B, S, H, Dh = q.shape
    n_back = window // dilation
    unit = dilation * DIL_BLOCK
    Sp = -(-S // unit) * unit
    L = Sp // dilation
    nblk = L // DIL_BLOCK

    def split(t):
        t = jnp.pad(t, ((0, 0), (0, Sp - S), (0, 0), (0, 0))).reshape(B, L, dilation, H, Dh)
        return t.transpose(0, 2, 3, 1, 4).reshape(B, dilation, H, nblk, DIL_BLOCK, Dh)

    def with_prev(t):
        prev = jnp.pad(t, ((0, 0), (0, 0), (0, 0), (1, 0), (0, 0), (0, 0)))[:, :, :, :-1]
        return jnp.concatenate([prev, t], axis=4)

    qs = split(q)
    kk = with_prev(split(k))
    vv = with_prev(split(v))
    s = jnp.einsum('brhnqd,brhnkd->brhnqk', qs, kk, preferred_element_type=jnp.float32) * SCALE
    qi = jnp.arange(DIL_BLOCK) + DIL_BLOCK
    ki = jnp.arange(2 * DIL_BLOCK)
    dist = qi[:, None] - ki[None, :]
    band = (dist >= 0) & (dist <= n_back)
    blk_ok = (jnp.arange(nblk)[:, None, None] > 0) | (ki[None, None, :] >= DIL_BLOCK)
    mask = band[None] & blk_ok
    s = jnp.where(mask, s, NEG)
    m = jnp.max(s, axis=-1, keepdims=True)
    p = jnp.exp(s - m)
    den = jnp.sum(p, axis=-1, keepdims=True)
    o = jnp.einsum('brhnqk,brhnkd->brhnqd', p.astype(vv.dtype), vv,
                   preferred_element_type=jnp.float32) / den
    lse = (m + jnp.log(den))[..., 0]
    o = o.reshape(B, dilation, H, L, Dh).transpose(0, 3, 1, 2, 4).reshape(B, Sp, H, Dh)[:, :S]
    lse = lse.reshape(B, dilation, H, L).transpose(0, 3, 1, 2).reshape(B, Sp, H)[:, :S]
    return o, lse


def dilated_mixture(q, k, v):
    outs, lses = [], []
    for window, dilation in DIL_PAIRS:
        o, lse = dilated_branch(q, k, v, window, dilation)
        outs.append(o)
        lses.append(lse)
    w = jax.nn.softmax(jnp.stack(lses, axis=0), axis=0)
    return jnp.sum(w[..., None] * jnp.stack(outs, axis=0), axis=0)


def conv_glu_ffn(h, w_gate, w_up, conv_w, conv_b, w_down):
    S = h.shape[1]
    g = h @ w_gate
    u = h @ w_up
    gp = jnp.pad(g, ((0, 0), (CONV_WIDTH - 1, 0), (0, 0)))
    g = conv_b + gp[:, 0:S] * conv_w[0]
    for j in range(1, CONV_WIDTH):
        g = g + gp[:, j:j + S] * conv_w[j]
    return (jax.nn.gelu(g, approximate=True) * u) @ w_down


def setup_inputs(seed: int = 0) -> dict:
    key = jax.random.key(seed)
    ks = jax.random.split(key, 14)
    f32 = jnp.float32

    def nrm(k, shape, scale):
        return jax.random.normal(k, shape, f32) * scale

    def gain(k, n):
        return 1.0 + 0.02 * jax.random.normal(k, (DEPTH, n), f32)

    return {
        'x': jax.random.normal(ks[0], (BATCH, SEQ, D_MODEL), f32),
        'positions': jnp.broadcast_to(jnp.arange(SEQ, dtype=jnp.int32), (BATCH, SEQ)),
        'attn_pre_g': gain(ks[1], D_MODEL),
        'w_qkv': nrm(ks[2], (DEPTH, D_MODEL, 3 * MIX_WIDTH), D_MODEL ** -0.5),
        'moba_out_g': gain(ks[3], W_MOBA),
        'dil_out_g': gain(ks[4], W_DIL),
        'w_o': nrm(ks[5], (DEPTH, MIX_WIDTH, D_MODEL), MIX_WIDTH ** -0.5),
        'attn_post_g': gain(ks[6], D_MODEL),
        'ffn_pre_g': gain(ks[7], D_MODEL),
        'w_gate': nrm(ks[8], (DEPTH, D_MODEL, D_FF), D_MODEL ** -0.5),
        'w_up': nrm(ks[9], (DEPTH, D_MODEL, D_FF), D_MODEL ** -0.5),
        'conv_w': nrm(ks[10], (DEPTH, CONV_WIDTH, D_FF), CONV_WIDTH ** -0.5),
        'conv_b': nrm(ks[11], (DEPTH, D_FF), 0.01),
        'w_down': nrm(ks[12], (DEPTH, D_FF, D_MODEL), D_FF ** -0.5),
        'ffn_post_g': gain(ks[13], D_MODEL),
    }


def reference(x, positions, attn_pre_g, w_qkv, moba_out_g, dil_out_g, w_o, attn_post_g,
              ffn_pre_g, w_gate, w_up, conv_w, conv_b, w_down, ffn_post_g):
    B, S, _ = x.shape
    for l in range(DEPTH):
        h = rms_norm(x, attn_pre_g[l])
        qkv = h @ w_qkv[l]
        qa, ka, va, qb, kb, vb = jnp.split(qkv, QKV_SPLITS, axis=-1)
        qa = partial_rope(qa.reshape(B, S, N_HEADS_MOBA, HEAD_DIM), positions)
        ka = partial_rope(ka.reshape(B, S, N_HEADS_MOBA, HEAD_DIM), positions)
        va = va.reshape(B, S, N_HEADS_MOBA, HEAD_DIM)
        qb = partial_rope(qb.reshape(B, S, N_HEADS_DIL, HEAD_DIM), positions)
        kb = partial_rope(kb.reshape(B, S, N_HEADS_DIL, HEAD_DIM), positions)
        vb = vb.reshape(B, S, N_HEADS_DIL, HEAD_DIM)
        oa = moba_attention(qa, ka, va).astype(x.dtype).reshape(B, S, W_MOBA)
        ob = dilated_mixture(qb, kb, vb).astype(x.dtype).reshape(B, S, W_DIL)
        mix = jnp.concatenate([rms_norm(oa, moba_out_g[l]), rms_norm(ob, dil_out_g[l])], axis=-1)
        x = x + rms_norm(mix @ w_o[l], attn_post_g[l])
        h = rms_norm(x, ffn_pre_g[l])
        y = conv_glu_ffn(h, w_gate[l], w_up[l], conv_w[l], conv_b[l], w_down[l])
        x = x + rms_norm(y, ffn_post_g[l])
    return x
```

```python
import functools

import jax
import jax.numpy as jnp
from jax import lax
from jax.experimental import pallas as pl
from jax.experimental.pallas import tpu as pltpu

F32 = jnp.float32
BF16 = jnp.bfloat16

HEAD_DIM = 128
ROT_DIM = HEAD_DIM // 4
ROT_HALF = ROT_DIM // 2
ROPE_THETA = 500000.0
MOBA_BLOCK = 256
MOBA_TOPK = 3
DIL_PAIRS = ((128, 1), (512, 4), (2048, 16))
DIL_BLOCK = 128
CONV_WIDTH = 3
RMS_EPS = 1e-6
SCALE = HEAD_DIM ** -0.5
NEG = -1e30

V7X_VMEM_LIMIT_BYTES = 56 * 1024 * 1024
BF16_SUBLANE_TILE = 16

NT_DIMS = (((1,), (1,)), ((), ()))


def _rms_scale(x):
    return lax.rsqrt(jnp.mean(x * x, axis=-1, keepdims=True) + RMS_EPS)


def _qkv_kernel(x_ref, g_ref, w_ref, cos_ref, sa_ref, sb_ref, out_ref, vt_ref, hn_ref, *,
                heads_per_tile, moba_v_tile, plain_tiles):
    j = pl.program_id(2)

    @pl.when(j == 0)
    def _():
        x = x_ref[0]
        hn_ref[...] = (x * _rms_scale(x) * g_ref[...]).astype(BF16)

    acc = jnp.dot(hn_ref[...], w_ref[...], preferred_element_type=F32)
    is_plain = functools.reduce(jnp.logical_or, [j == t for t in plain_tiles])

    @pl.when(jnp.logical_not(is_plain))
    def _():
        cos = cos_ref[0]
        sa = sa_ref[0]
        sb = sb_ref[0]
        for h in range(heads_per_tile):
            t = acc[:, h * HEAD_DIM:(h + 1) * HEAD_DIM]
            r = (t * cos
                 + pltpu.roll(t, HEAD_DIM - ROT_HALF, 1) * sa
                 + pltpu.roll(t, ROT_HALF, 1) * sb)
            out_ref[0, h] = r.astype(BF16)

    @pl.when(is_plain)
    def _():
        for h in range(heads_per_tile):
            out_ref[0, h] = acc[:, h * HEAD_DIM:(h + 1) * HEAD_DIM].astype(BF16)

    @pl.when(j == moba_v_tile)
    def _():
        n_blk = acc.shape[0] // MOBA_BLOCK
        for h in range(heads_per_tile):
            for c in range(n_blk):
                blk = acc[c * MOBA_BLOCK:(c + 1) * MOBA_BLOCK, h * HEAD_DIM:(h + 1) * HEAD_DIM]
                vt_ref[0, h, c] = blk.T.astype(BF16)


def _qkv_call(x, g, w_bf16, cos_t, sa_t, sb_t, *, n_heads_a, tm=512):
    B, S, D = x.shape
    N = w_bf16.shape[1]
    tn = n_heads_a * HEAD_DIM
    heads_per_tile = tn // HEAD_DIM
    n_col_blocks = N // HEAD_DIM
    grid = (B, S // tm, N // tn)
    kern = functools.partial(_qkv_kernel, heads_per_tile=heads_per_tile, moba_v_tile=2,
                             plain_tiles=(2, 5))
    return pl.pallas_call(
        kern,
        grid=grid,
        in_specs=[
            pl.BlockSpec((1, tm, D), lambda b, i, j: (b, i, 0)),
            pl.BlockSpec((1, D), lambda b, i, j: (0, 0)),
            pl.BlockSpec((D, tn), lambda b, i, j: (0, j)),
            pl.BlockSpec((1, tm, HEAD_DIM), lambda b, i, j: (b, i, 0)),
            pl.BlockSpec((1, tm, HEAD_DIM), lambda b, i, j: (b, i, 0)),
            pl.BlockSpec((1, tm, HEAD_DIM), lambda b, i, j: (b, i, 0)),
        ],
        out_specs=[
            pl.BlockSpec((1, heads_per_tile, tm, HEAD_DIM), lambda b, i, j: (b, j, i, 0)),
            pl.BlockSpec((1, heads_per_tile, tm // MOBA_BLOCK, HEAD_DIM, MOBA_BLOCK),
                         lambda b, i, j: (b, 0, i, 0, 0)),
        ],
        out_shape=[
            jax.ShapeDtypeStruct((B, n_col_blocks, S, HEAD_DIM), BF16),
            jax.ShapeDtypeStruct((B, n_heads_a, S // MOBA_BLOCK, HEAD_DIM, MOBA_BLOCK), BF16),
        ],
        scratch_shapes=[pltpu.VMEM((tm, D), BF16)],
        compiler_params=pltpu.CompilerParams(
            dimension_semantics=("parallel", "parallel", "arbitrary"),
            vmem_limit_bytes=V7X_VMEM_LIMIT_BYTES),
        name="qkv_rope",
    )(x, g, w_bf16, cos_t, sa_t, sb_t)


def _moba_kernel(q_ref, k_ref, vt_ref, o_ref, kmean_ref, bias_ref, *, n_blocks):
    i = pl.program_id(2)
    blk_sz = MOBA_BLOCK

    @pl.when(i == 0)
    def _():
        for blk in range(n_blocks):
            kb = k_ref[0, 0, blk * blk_sz:(blk + 1) * blk_sz, :].astype(F32)
            kmean_ref[blk:blk + 1, :] = jnp.mean(kb, axis=0, keepdims=True)

    q = q_ref[0, 0]

    km = kmean_ref[...]
    km_hi = km.astype(BF16)
    km_lo = (km - km_hi.astype(F32)).astype(BF16)
    gate = (lax.dot_general(km_hi, q, NT_DIMS, preferred_element_type=F32)
            + lax.dot_general(km_lo, q, NT_DIMS, preferred_element_type=F32))

    blk_id = lax.broadcasted_iota(jnp.int32, gate.shape, 0).astype(F32)
    neg_inf = jnp.float32(-jnp.inf)
    g = jnp.where(blk_id < i.astype(F32), gate, neg_inf)
    sel = jnp.zeros(gate.shape, dtype=jnp.bool_)
    for _ in range(MOBA_TOPK):
        m = jnp.max(g, axis=0, keepdims=True)
        first = jnp.min(jnp.where(g == m, blk_id, float(n_blocks)), axis=0, keepdims=True)
        pick = jnp.logical_and(blk_id == first, m > neg_inf)
        sel = jnp.logical_or(sel, pick)
        g = jnp.where(pick, neg_inf, g)
    bias_ref[...] = jnp.where(sel, 0.0, NEG).astype(F32)

    kd = k_ref[0, 0, pl.ds(pl.multiple_of(i * blk_sz, blk_sz), blk_sz), :]
    s = lax.dot_general(kd, q, NT_DIMS, preferred_element_type=F32) * SCALE
    key_pos = lax.broadcasted_iota(jnp.int32, s.shape, 0)
    q_pos = lax.broadcasted_iota(jnp.int32, s.shape, 1)
    s = jnp.where(key_pos <= q_pos, s, NEG)
    m0 = jnp.max(s, axis=0, keepdims=True)
    p = jnp.exp(s - m0)
    l0 = jnp.sum(p, axis=0, keepdims=True)
    acc0 = jnp.dot(vt_ref[0, 0, i], p.astype(BF16), preferred_element_type=F32)

    def body(j, carry):
        m_prev, l_prev, acc_prev = carry
        kb = k_ref[0, 0, pl.ds(pl.multiple_of(j * blk_sz, blk_sz), blk_sz), :]
        sj = lax.dot_general(kb, q, NT_DIMS, preferred_element_type=F32) * SCALE
        sj = sj + bias_ref[pl.ds(j, 1), :]
        m_new = jnp.maximum(m_prev, jnp.max(sj, axis=0, keepdims=True))
        alpha = jnp.exp(m_prev - m_new)
        pj = jnp.exp(sj - m_new)
        l_new = alpha * l_prev + jnp.sum(pj, axis=0, keepdims=True)
        acc_new = alpha * acc_prev + jnp.dot(vt_ref[0, 0, j], pj.astype(BF16),
                                             preferred_element_type=F32)
        return m_new, l_new, acc_new

    _, l_fin, acc_fin = lax.fori_loop(0, i, body, (m0, l0, acc0))
    o_ref[0, 0] = (acc_fin / l_fin).T


def _moba_call(qkv_heads, vt, *, n_heads_a):
    B, _, S, _ = qkv_heads.shape
    n_blocks = S // MOBA_BLOCK
    tq = MOBA_BLOCK
    kern = functools.partial(_moba_kernel, n_blocks=n_blocks)
    return pl.pallas_call(
        kern,
        grid=(B, n_heads_a, S // tq),
        in_specs=[
            pl.BlockSpec((1, 1, tq, HEAD_DIM), lambda b, h, i: (b, h, i, 0)),
            pl.BlockSpec((1, 1, S, HEAD_DIM), lambda b, h, i: (b, n_heads_a + h, 0, 0)),
            pl.BlockSpec((1, 1, n_blocks, HEAD_DIM, MOBA_BLOCK), lambda b, h, i: (b, h, 0, 0, 0)),
        ],
        out_specs=pl.BlockSpec((1, 1, tq, HEAD_DIM), lambda b, h, i: (b, h, i, 0)),
        out_shape=jax.ShapeDtypeStruct((B, n_heads_a, S, HEAD_DIM), F32),
        scratch_shapes=[pltpu.VMEM((n_blocks, HEAD_DIM), F32),
                        pltpu.VMEM((n_blocks, tq), F32)],
        compiler_params=pltpu.CompilerParams(
            dimension_semantics=("parallel", "parallel", "arbitrary"),
            vmem_limit_bytes=V7X_VMEM_LIMIT_BYTES),
        name="moba_attn",
    )(qkv_heads, qkv_heads, vt)


def _dilated_kernel(*refs, seq_len):
    n_br = len(DIL_PAIRS)
    in_refs = refs[:3 * n_br]
    out_refs = refs[3 * n_br:]
    blk = DIL_BLOCK

    qi = lax.broadcasted_iota(jnp.int32, (blk, 2 * blk), 0) + blk
    ki = lax.broadcasted_iota(jnp.int32, (blk, 2 * blk), 1)
    dist = qi - ki
    band_bias = jnp.where(jnp.logical_and(dist >= 0, dist <= blk), 0.0, NEG).astype(F32)
    qi0 = lax.broadcasted_iota(jnp.int32, (blk, blk), 0)
    ki0 = lax.broadcasted_iota(jnp.int32, (blk, blk), 1)
    causal_bias = jnp.where(ki0 <= qi0, 0.0, NEG).astype(F32)

    def attend(qb, kb, vb, bias):
        s = lax.dot_general(qb, kb, NT_DIMS, preferred_element_type=F32) * SCALE + bias
        m = jnp.max(s, axis=-1, keepdims=True)
        p = jnp.exp(s - m)
        den = jnp.sum(p, axis=-1, keepdims=True)
        o = jnp.dot(p.astype(BF16), vb, preferred_element_type=F32) / den
        lse = m + jnp.log(den)
        lse_row = jnp.broadcast_to(lse, (blk, blk)).T[0:1, :]
        return o, lse_row

    for g, (window, dilation) in enumerate(DIL_PAIRS):
        assert window // dilation == DIL_BLOCK
        q_ref, k_ref, v_ref = in_refs[3 * g:3 * g + 3]
        o_ref, lse_ref = out_refs[2 * g:2 * g + 2]
        n_blk = seq_len // dilation // blk
        for r in range(dilation):
            cs = slice(r * HEAD_DIM, (r + 1) * HEAD_DIM)

            o, lse_row = attend(q_ref[0, 0, 0:blk, cs], k_ref[0, 0, 0:blk, cs],
                                v_ref[0, 0, 0:blk, cs], causal_bias)
            o_ref[0, 0, 0:blk, cs] = o
            lse_ref[0, 0, r * n_blk:r * n_blk + 1, :] = lse_row

            def body(n, carry, q_ref=q_ref, k_ref=k_ref, v_ref=v_ref, o_ref=o_ref,
                     lse_ref=lse_ref, cs=cs, r=r, n_blk=n_blk):
                q0 = pl.multiple_of(n * blk, blk)
                k0 = pl.multiple_of((n - 1) * blk, blk)
                o, lse_row = attend(q_ref[0, 0, pl.ds(q0, blk), cs],
                                    k_ref[0, 0, pl.ds(k0, 2 * blk), cs],
                                    v_ref[0, 0, pl.ds(k0, 2 * blk), cs], band_bias)
                o_ref[0, 0, pl.ds(q0, blk), cs] = o
                lse_ref[0, 0, pl.ds(r * n_blk + n, 1), :] = lse_row
                return carry

            lax.fori_loop(1, n_blk, body, 0)


def _dilated_call(qkv_heads, *, n_heads_b, q_off, k_off, v_off):
    B, C, S, _ = qkv_heads.shape
    in_arrays, in_specs, out_shapes, out_specs = [], [], [], []
    lse_rows = S // DIL_BLOCK
    for _, d in DIL_PAIRS:
        view = qkv_heads.reshape(B, C, S // d, d * HEAD_DIM)
        blk_shape = (1, 1, S // d, d * HEAD_DIM)
        for off in (q_off, k_off, v_off):
            in_arrays.append(view)
            in_specs.append(pl.BlockSpec(blk_shape, lambda b, h, off=off: (b, off + h, 0, 0)))
        out_shapes.append(jax.ShapeDtypeStruct((B, n_heads_b, S // d, d * HEAD_DIM), F32))
        out_specs.append(pl.BlockSpec(blk_shape, lambda b, h: (b, h, 0, 0)))
        out_shapes.append(jax.ShapeDtypeStruct((B, n_heads_b, lse_rows // d * d, DIL_BLOCK), F32))
        out_specs.append(pl.BlockSpec((1, 1, lse_rows, DIL_BLOCK), lambda b, h: (b, h, 0, 0)))
    outs = pl.pallas_call(
        functools.partial(_dilated_kernel, seq_len=S),
        grid=(B, n_heads_b),
        in_specs=in_specs,
        out_specs=out_specs,
        out_shape=out_shapes,
        compiler_params=pltpu.CompilerParams(
            dimension_semantics=("parallel", "parallel"),
            vmem_limit_bytes=V7X_VMEM_LIMIT_BYTES),
        name="dilated_attn",
    )(*in_arrays)
    o_nat, lse_nat = [], []
    for g, (_, d) in enumerate(DIL_PAIRS):
        o_nat.append(outs[2 * g].reshape(B, n_heads_b, S, HEAD_DIM))
        lse = outs[2 * g + 1].reshape(B, n_heads_b, d, S // d)
        lse_nat.append(lse.transpose(0, 3, 2, 1).reshape(B, S, n_heads_b))
    return o_nat, lse_nat


def _attnout_kernel(x_ref, oa_ref, o1_ref, o2_ref, o3_ref, l1_ref, l2_ref, l3_ref,
                    ga_ref, gb_ref, wo_ref, pg_ref, out_ref, mix_ref, *, n_heads_a, n_heads_b):
    wa = n_heads_a * HEAD_DIM

    ssq = None
    for h in range(n_heads_a):
        t = oa_ref[0, h]
        part = jnp.sum(t * t, axis=-1, keepdims=True)
        ssq = part if ssq is None else ssq + part
    inv = lax.rsqrt(ssq / wa + RMS_EPS)
    for h in range(n_heads_a):
        cs = slice(h * HEAD_DIM, (h + 1) * HEAD_DIM)
        mix_ref[:, cs] = (oa_ref[0, h] * inv * ga_ref[:, cs]).astype(BF16)

    l1, l2, l3 = l1_ref[0], l2_ref[0], l3_ref[0]
    lmax = jnp.maximum(jnp.maximum(l1, l2), l3)
    e1, e2, e3 = jnp.exp(l1 - lmax), jnp.exp(l2 - lmax), jnp.exp(l3 - lmax)
    esum = e1 + e2 + e3
    w1, w2, w3 = e1 / esum, e2 / esum, e3 / esum
    wb = n_heads_b * HEAD_DIM
    obs = []
    ssq = None
    for h in range(n_heads_b):
        ob = (w1[:, h:h + 1] * o1_ref[0, h] + w2[:, h:h + 1] * o2_ref[0, h]
              + w3[:, h:h + 1] * o3_ref[0, h])
        obs.append(ob)
        part = jnp.sum(ob * ob, axis=-1, keepdims=True)
        ssq = part if ssq is None else ssq + part
    inv = lax.rsqrt(ssq / wb + RMS_EPS)
    for h in range(n_heads_b):
        cs = slice(h * HEAD_DIM, (h + 1) * HEAD_DIM)
        mix_ref[:, wa + h * HEAD_DIM:wa + (h + 1) * HEAD_DIM] = (
            obs[h] * inv * gb_ref[:, cs]).astype(BF16)

    y = jnp.dot(mix_ref[...], wo_ref[...], preferred_element_type=F32)
    out_ref[0] = x_ref[0] + y * _rms_scale(y) * pg_ref[...]


def _attnout_call(x, oa, o_branches, lse_branches, ga, gb, wo_bf16, pg, *, tm=256):
    B, S, D = x.shape
    n_heads_a = oa.shape[1]
    n_heads_b = o_branches[0].shape[1]
    head_spec_a = pl.BlockSpec((1, n_heads_a, tm, HEAD_DIM), lambda b, i: (b, 0, i, 0))
    head_spec_b = pl.BlockSpec((1, n_heads_b, tm, HEAD_DIM), lambda b, i: (b, 0, i, 0))
    lse_spec = pl.BlockSpec((1, tm, n_heads_b), lambda b, i: (b, i, 0))
    kern = functools.partial(_attnout_kernel, n_heads_a=n_heads_a, n_heads_b=n_heads_b)
    return pl.pallas_call(
        kern,
        grid=(B, S // tm),
        in_specs=[
            pl.BlockSpec((1, tm, D), lambda b, i: (b, i, 0)),
            head_spec_a, head_spec_b, head_spec_b, head_spec_b,
            lse_spec, lse_spec, lse_spec,
            pl.BlockSpec((1, ga.shape[1]), lambda b, i: (0, 0)),
            pl.BlockSpec((1, gb.shape[1]), lambda b, i: (0, 0)),
            pl.BlockSpec(wo_bf16.shape, lambda b, i: (0, 0)),
            pl.BlockSpec((1, D), lambda b, i: (0, 0)),
        ],
        out_specs=pl.BlockSpec((1, tm, D), lambda b, i: (b, i, 0)),
        out_shape=jax.ShapeDtypeStruct((B, S, D), F32),
        scratch_shapes=[pltpu.VMEM((tm, wo_bf16.shape[0]), BF16)],
        compiler_params=pltpu.CompilerParams(
            dimension_semantics=("parallel", "parallel"),
            vmem_limit_bytes=V7X_VMEM_LIMIT_BYTES),
        name="attn_out",
    )(x, oa, *o_branches, *lse_branches, ga, gb, wo_bf16, pg)


def _ffn_kernel(x_ref, xh_ref, g_ref, wg_ref, wu_ref, cw_ref, cb_ref, wd_ref, pg_ref,
                out_ref, hn_ref, acc_ref, *, halo):
    i = pl.program_id(1)
    f = pl.program_id(2)
    tm = x_ref.shape[1]

    @pl.when(f == 0)
    def _():
        x = x_ref[0]
        hn_ref[halo:, :] = (x * _rms_scale(x) * g_ref[...]).astype(BF16)
        xh = xh_ref[0]
        hh = xh * _rms_scale(xh) * g_ref[...]
        hn_ref[0:halo, :] = jnp.where(i > 0, hh, 0.0).astype(BF16)
        acc_ref[...] = jnp.zeros_like(acc_ref)

    gate_ext = jnp.dot(hn_ref[...], wg_ref[...], preferred_element_type=F32)
    up = jnp.dot(hn_ref[halo:, :], wu_ref[...], preferred_element_type=F32)
    cw = cw_ref[...]
    gc = cb_ref[...] + gate_ext[halo - 2:halo - 2 + tm] * cw[0:1, :]
    gc = gc + gate_ext[halo - 1:halo - 1 + tm] * cw[1:2, :]
    gc = gc + gate_ext[halo:halo + tm] * cw[2:3, :]
    gelu = 0.5 * gc * (1.0 + jnp.tanh(0.7978845608028654 * (gc + 0.044715 * (gc * gc * gc))))
    act = (gelu * up).astype(BF16)
    acc_ref[...] += jnp.dot(act, wd_ref[...], preferred_element_type=F32)

    @pl.when(f == pl.num_programs(2) - 1)
    def _():
        y = acc_ref[...]
        out_ref[0] = x_ref[0] + y * _rms_scale(y) * pg_ref[...]


def _ffn_call(x, g, wg_bf16, wu_bf16, conv_w, conv_b, wd_bf16, pg, *, tm=512, tf=1024):
    B, S, D = x.shape
    d_ff = wg_bf16.shape[1]
    halo = BF16_SUBLANE_TILE
    assert CONV_WIDTH - 1 <= halo and tm % halo == 0
    halo_blocks_per_tile = tm // halo
    kern = functools.partial(_ffn_kernel, halo=halo)
    return pl.pallas_call(
        kern,
        grid=(B, S // tm, d_ff // tf),
        in_specs=[
            pl.BlockSpec((1, tm, D), lambda b, i, f: (b, i, 0)),
            pl.BlockSpec((1, halo, D),
                         lambda b, i, f: (b, jnp.maximum(i * halo_blocks_per_tile - 1, 0), 0)),
            pl.BlockSpec((1, D), lambda b, i, f: (0, 0)),
            pl.BlockSpec((D, tf), lambda b, i, f: (0, f)),
            pl.BlockSpec((D, tf), lambda b, i, f: (0, f)),
            pl.BlockSpec((CONV_WIDTH, tf), lambda b, i, f: (0, f)),
            pl.BlockSpec((1, tf), lambda b, i, f: (0, f)),
            pl.BlockSpec((tf, D), lambda b, i, f: (f, 0)),
            pl.BlockSpec((1, D), lambda b, i, f: (0, 0)),
        ],
        out_specs=pl.BlockSpec((1, tm, D), lambda b, i, f: (b, i, 0)),
        out_shape=jax.ShapeDtypeStruct((B, S, D), F32),
        scratch_shapes=[pltpu.VMEM((halo + tm, D), BF16), pltpu.VMEM((tm, D), F32)],
        compiler_params=pltpu.CompilerParams(
            dimension_semantics=("parallel", "parallel", "arbitrary"),
            vmem_limit_bytes=V7X_VMEM_LIMIT_BYTES),
        name="conv_glu_ffn",
    )(x, x, g, wg_bf16, wu_bf16, conv_w, conv_b, wd_bf16, pg)


def _rope_tables(positions):
    inv_freq = ROPE_THETA ** (-jnp.arange(ROT_HALF, dtype=F32) / ROT_HALF)
    ang = positions.astype(F32)[..., None] * inv_freq
    cos, sin = jnp.cos(ang), jnp.sin(ang)
    ones = jnp.ones(ang.shape[:-1] + (HEAD_DIM - ROT_DIM,), F32)
    zeros_tail = jnp.zeros(ang.shape[:-1] + (HEAD_DIM - ROT_HALF,), F32)
    zeros_head = jnp.zeros_like(sin)
    cos_t = jnp.concatenate([cos, cos, ones], axis=-1)
    sa_t = jnp.concatenate([-sin, zeros_tail], axis=-1)
    sb_t = jnp.concatenate([zeros_head, sin, zeros_tail[..., :HEAD_DIM - ROT_DIM]], axis=-1)
    return cos_t, sa_t, sb_t


def kernel(x, positions, attn_pre_g, w_qkv, moba_out_g, dil_out_g, w_o, attn_post_g, ffn_pre_g,
           w_gate, w_up, conv_w, conv_b, w_down, ffn_post_g):
    depth = w_qkv.shape[0]
    n_heads_a = moba_out_g.shape[1] // HEAD_DIM
    n_heads_b = dil_out_g.shape[1] // HEAD_DIM
    cos_t, sa_t, sb_t = _rope_tables(positions)
    for l in range(depth):
        qkv_heads, vt = _qkv_call(x, attn_pre_g[l][None], w_qkv[l].astype(BF16), cos_t, sa_t, sb_t,
                                  n_heads_a=n_heads_a)
        oa = _moba_call(qkv_heads, vt, n_heads_a=n_heads_a)
        o_br, lse_br = _dilated_call(qkv_heads, n_heads_b=n_heads_b, q_off=3 * n_heads_a,
                                     k_off=3 * n_heads_a + n_heads_b,
                                     v_off=3 * n_heads_a + 2 * n_heads_b)
        x = _attnout_call(x, oa, o_br, lse_br, moba_out_g[l][None], dil_out_g[l][None],
                          w_o[l].astype(BF16), attn_post_g[l][None])
        x = _ffn_call(x, ffn_pre_g[l][None], w_gate[l].astype(BF16), w_up[l].astype(BF16),
                      conv_w[l], conv_b[l][None], w_down[l].astype(BF16), ffn_post_g[l][None])
    return x
```

```python
import functools

import jax
import jax.numpy as jnp
from jax import lax
from jax.experimental import pallas as pl
from jax.experimental.pallas import tpu as pltpu

F32 = jnp.float32
BF16 = jnp.bfloat16

HEAD_DIM = 128
ROT_DIM = HEAD_DIM // 4
ROT_HALF = ROT_DIM // 2
ROPE_THETA = 500000.0
MOBA_BLOCK = 256
MOBA_TOPK = 3
DIL_PAIRS = ((128, 1), (512, 4), (2048, 16))
DIL_BLOCK = 128
CONV_WIDTH = 3
RMS_EPS = 1e-6
SCALE = HEAD_DIM ** -0.5
NEG = -1e30

V7X_VMEM_LIMIT_BYTES = 56 * 1024 * 1024
BF16_SUBLANE_TILE = 16
V7X_MXU_WIDTH = 256

NT_DIMS = (((1,), (1,)), ((), ()))


def _rms_scale(x):
    return lax.rsqrt(jnp.mean(x * x, axis=-1, keepdims=True) + RMS_EPS)


def _qkv_kernel(x_ref, g_ref, w_ref, cos_ref, sa_ref, sb_ref,
                nat_ref, vt_ref, d4_ref, d16_ref, hn_ref, slab_ref, *, heads_per_tile):
    j = pl.program_id(2)
    tm = hn_ref.shape[0]
    heads_per_dot = V7X_MXU_WIDTH // HEAD_DIM
    dot_width = heads_per_dot * HEAD_DIM

    @pl.when(j == 0)
    def _():
        x = x_ref[0]
        hn_ref[...] = (x * _rms_scale(x) * g_ref[...]).astype(BF16)

    def rope(t):
        return (t * cos_ref[0]
                + pltpu.roll(t, HEAD_DIM - ROT_HALF, 1) * sa_ref[0]
                + pltpu.roll(t, ROT_HALF, 1) * sb_ref[0])

    def run(with_rope, with_vt, with_dilated):
        for c in range(heads_per_tile // heads_per_dot):
            acc = jnp.dot(hn_ref[...], w_ref[:, c * dot_width:(c + 1) * dot_width],
                          preferred_element_type=F32)
            for hh in range(heads_per_dot):
                h = c * heads_per_dot + hh
                t = acc[:, hh * HEAD_DIM:(hh + 1) * HEAD_DIM]
                if with_rope:
                    t = rope(t)
                nat_ref[0, h] = t.astype(BF16)
                if with_vt:
                    for blk in range(tm // MOBA_BLOCK):
                        vt_ref[0, h, blk] = t[blk * MOBA_BLOCK:(blk + 1) * MOBA_BLOCK].T.astype(BF16)
                if with_dilated:
                    slab_ref[h] = t
                    for ref, d in ((d4_ref, 4), (d16_ref, 16)):
                        for res in range(d):
                            ref[0, h, res] = slab_ref[h, pl.ds(res, tm // d, stride=d), :].astype(BF16)

    pl.when(j < 2)(lambda: run(True, False, False))
    pl.when(j == 2)(lambda: run(False, True, False))
    pl.when(jnp.logical_or(j == 3, j == 4))(lambda: run(True, False, True))
    pl.when(j == 5)(lambda: run(False, False, True))


def _qkv_call(x, g, w_bf16, cos_t, sa_t, sb_t, *, n_heads_a, n_heads_b, tm=512):
    B, S, D = x.shape
    N = w_bf16.shape[1]
    assert n_heads_a == n_heads_b and N == 3 * (n_heads_a + n_heads_b) * HEAD_DIM
    hpt = n_heads_a
    tn = hpt * HEAD_DIM
    first_b_tile = 3
    kern = functools.partial(_qkv_kernel, heads_per_tile=hpt)

    def dil_index(b, i, j):
        return (b, jnp.maximum(j - first_b_tile, 0), 0, i, 0)

    return pl.pallas_call(
        kern,
        grid=(B, S // tm, N // tn),
        in_specs=[
            pl.BlockSpec((1, tm, D), lambda b, i, j: (b, i, 0)),
            pl.BlockSpec((1, D), lambda b, i, j: (0, 0)),
            pl.BlockSpec((D, tn), lambda b, i, j: (0, j)),
            pl.BlockSpec((1, tm, HEAD_DIM), lambda b, i, j: (b, i, 0)),
            pl.BlockSpec((1, tm, HEAD_DIM), lambda b, i, j: (b, i, 0)),
            pl.BlockSpec((1, tm, HEAD_DIM), lambda b, i, j: (b, i, 0)),
        ],
        out_specs=[
            pl.BlockSpec((1, hpt, tm, HEAD_DIM), lambda b, i, j: (b, j, i, 0)),
            pl.BlockSpec((1, hpt, tm // MOBA_BLOCK, HEAD_DIM, MOBA_BLOCK),
                         lambda b, i, j: (b, 0, i, 0, 0)),
            pl.BlockSpec((1, hpt, 4, tm // 4, HEAD_DIM), dil_index),
            pl.BlockSpec((1, hpt, 16, tm // 16, HEAD_DIM), dil_index),
        ],
        out_shape=[
            jax.ShapeDtypeStruct((B, N // HEAD_DIM, S, HEAD_DIM), BF16),
            jax.ShapeDtypeStruct((B, n_heads_a, S // MOBA_BLOCK, HEAD_DIM, MOBA_BLOCK), BF16),
            jax.ShapeDtypeStruct((B, 3 * n_heads_b, 4, S // 4, HEAD_DIM), BF16),
            jax.ShapeDtypeStruct((B, 3 * n_heads_b, 16, S // 16, HEAD_DIM), BF16),
        ],
        scratch_shapes=[pltpu.VMEM((tm, D), BF16), pltpu.VMEM((hpt, tm, HEAD_DIM), F32)],
        compiler_params=pltpu.CompilerParams(
            dimension_semantics=("parallel", "parallel", "arbitrary"),
            vmem_limit_bytes=V7X_VMEM_LIMIT_BYTES),
        name="qkv_rope",
    )(x, g, w_bf16, cos_t, sa_t, sb_t)


def _moba_kernel(q_ref, k_ref, vt_ref, o_ref, kmean_ref, bias_ref, *, n_blocks, heads, lookahead=8):
    i = pl.program_id(2)
    blk_sz = MOBA_BLOCK

    @pl.when(i == 0)
    def _():
        for h in range(heads):
            for blk in range(n_blocks):
                kb = k_ref[0, h, blk * blk_sz:(blk + 1) * blk_sz, :].astype(F32)
                kmean_ref[h, blk:blk + 1, :] = jnp.mean(kb, axis=0, keepdims=True)

    def select_blocks(h, q):
        km = kmean_ref[h]
        km_hi = km.astype(BF16)
        km_lo = (km - km_hi.astype(F32)).astype(BF16)
        gate = (lax.dot_general(km_hi, q, NT_DIMS, preferred_element_type=F32)
                + lax.dot_general(km_lo, q, NT_DIMS, preferred_element_type=F32))
        blk_id = lax.broadcasted_iota(jnp.int32, gate.shape, 0).astype(F32)
        neg_inf = jnp.float32(-jnp.inf)
        g = jnp.where(blk_id < i.astype(F32), gate, neg_inf)
        sel = jnp.zeros(gate.shape, dtype=jnp.bool_)
        for _ in range(MOBA_TOPK):
            m = jnp.max(g, axis=0, keepdims=True)
            first = jnp.min(jnp.where(g == m, blk_id, float(n_blocks)), axis=0, keepdims=True)
            pick = jnp.logical_and(blk_id == first, m > neg_inf)
            sel = jnp.logical_or(sel, pick)
            g = jnp.where(pick, neg_inf, g)
        bias_ref[h] = jnp.where(sel, 0.0, NEG).astype(F32)

    def scores(h, j):
        kb = k_ref[0, h, pl.ds(pl.multiple_of(j * blk_sz, blk_sz), blk_sz), :]
        return lax.dot_general(kb, q_ref[0, h], NT_DIMS, preferred_element_type=F32)

    def pipelined(stage_a, stage_b):
        ahead = [stage_a(h) for h in range(min(lookahead, heads))]
        outs = []
        for h in range(heads):
            if h + lookahead < heads:
                ahead.append(stage_a(h + lookahead))
            outs.append(stage_b(h, ahead[h]))
        return outs

    def own_a(h):
        select_blocks(h, q_ref[0, h])
        return scores(h, i)

    def own_b(h, s):
        key_pos = lax.broadcasted_iota(jnp.int32, s.shape, 0)
        q_pos = lax.broadcasted_iota(jnp.int32, s.shape, 1)
        s = jnp.where(key_pos <= q_pos, s * SCALE, NEG)
        m0 = jnp.max(s, axis=0, keepdims=True)
        p = jnp.exp(s - m0)
        l0 = jnp.sum(p, axis=0, keepdims=True)
        acc0 = jnp.dot(vt_ref[0, h, i], p.astype(BF16), preferred_element_type=F32)
        return m0, l0, acc0

    init = pipelined(own_a, own_b)

    def body(j, carry):
        def past_b(h, s):
            m_prev, l_prev, acc_prev = carry[h]
            sj = s * SCALE + bias_ref[h, pl.ds(j, 1), :]
            m_new = jnp.maximum(m_prev, jnp.max(sj, axis=0, keepdims=True))
            alpha = jnp.exp(m_prev - m_new)
            pj = jnp.exp(sj - m_new)
            l_new = alpha * l_prev + jnp.sum(pj, axis=0, keepdims=True)
            acc_new = alpha * acc_prev + jnp.dot(vt_ref[0, h, j], pj.astype(BF16),
                                                 preferred_element_type=F32)
            return m_new, l_new, acc_new

        return tuple(pipelined(lambda h: scores(h, j), past_b))

    fin = lax.fori_loop(0, i, body, tuple(init))
    for h in range(heads):
        _, l_fin, acc_fin = fin[h]
        o_ref[0, h] = (acc_fin / l_fin).T


def _moba_call(qkv_heads, vt, *, n_heads_a, heads_per_step=8):
    B, _, S, _ = qkv_heads.shape
    n_blocks = S // MOBA_BLOCK
    tq = MOBA_BLOCK
    hps = heads_per_step
    assert n_heads_a % hps == 0
    k_first = n_heads_a // hps
    kern = functools.partial(_moba_kernel, n_blocks=n_blocks, heads=hps)
    return pl.pallas_call(
        kern,
        grid=(B, n_heads_a // hps, S // tq),
        in_specs=[
            pl.BlockSpec((1, hps, tq, HEAD_DIM), lambda b, h, i: (b, h, i, 0)),
            pl.BlockSpec((1, hps, S, HEAD_DIM), lambda b, h, i: (b, k_first + h, 0, 0)),
            pl.BlockSpec((1, hps, n_blocks, HEAD_DIM, MOBA_BLOCK), lambda b, h, i: (b, h, 0, 0, 0)),
        ],
        out_specs=pl.BlockSpec((1, hps, tq, HEAD_DIM), lambda b, h, i: (b, h, i, 0)),
        out_shape=jax.ShapeDtypeStruct((B, n_heads_a, S, HEAD_DIM), F32),
        scratch_shapes=[pltpu.VMEM((hps, n_blocks, HEAD_DIM), F32),
                        pltpu.VMEM((hps, n_blocks, tq), F32)],
        compiler_params=pltpu.CompilerParams(
            dimension_semantics=("parallel", "parallel", "arbitrary"),
            vmem_limit_bytes=V7X_VMEM_LIMIT_BYTES),
        name="moba_attn",
    )(qkv_heads, qkv_heads, vt)


def _dilated_kernel(q1_ref, k1_ref, v1_ref, q4_ref, k4_ref, v4_ref, q16_ref, k16_ref, v16_ref,
                    ob_ref, o_scr, lse_scr, bias_scr, *, seq_len, unroll, combine_rows):
    blk = DIL_BLOCK
    branch_refs = ((q1_ref, k1_ref, v1_ref), (q4_ref, k4_ref, v4_ref), (q16_ref, k16_ref, v16_ref))

    qi = lax.broadcasted_iota(jnp.int32, (blk, 2 * blk), 0)
    ki = lax.broadcasted_iota(jnp.int32, (blk, 2 * blk), 1)
    dist = qi + blk - ki
    bias_scr[0] = jnp.where(jnp.logical_and(dist >= 0, dist <= blk), 0.0, NEG).astype(F32)
    bias_scr[1] = jnp.where(ki <= qi, 0.0, NEG).astype(F32)

    for g, (window, d) in enumerate(DIL_PAIRS):
        assert window // d == DIL_BLOCK
        q_ref, k_ref, v_ref = branch_refs[g]
        n_blk = seq_len // d // blk

        def rows_of(ref, r, start, size, d=d):
            if d == 1:
                return ref[0, 0, pl.ds(start, size), :]
            return ref[0, 0, r, pl.ds(start, size), :]

        def key_start(n):
            return pl.multiple_of(jnp.maximum(n - 1, 0) * blk, blk)

        def scores(r, n, q_ref=q_ref, k_ref=k_ref, rows_of=rows_of):
            qb = rows_of(q_ref, r, pl.multiple_of(n * blk, blk), blk)
            kb = rows_of(k_ref, r, key_start(n), 2 * blk)
            return lax.dot_general(qb, kb, NT_DIMS, preferred_element_type=F32)

        def finish(r, n, s, g=g, d=d, v_ref=v_ref, rows_of=rows_of):
            s = s * SCALE + bias_scr[(n == 0).astype(jnp.int32)]
            m = jnp.max(s, axis=-1, keepdims=True)
            p = jnp.exp(s - m)
            den = jnp.sum(p, axis=-1, keepdims=True)
            vb = rows_of(v_ref, r, key_start(n), 2 * blk)
            o = jnp.dot(p.astype(BF16), vb, preferred_element_type=F32) / den
            lse = jnp.broadcast_to(m + jnp.log(den), (blk, HEAD_DIM))
            if d == 1:
                rows = pl.ds(pl.multiple_of(n * blk, blk), blk)
            else:
                rows = pl.ds(pl.multiple_of(n * (blk * d), blk) + r, blk, stride=d)
            o_scr[g, rows, :] = o
            lse_scr[g, rows, :] = lse

        def run_blocks(r, ns, scores=scores, finish=finish):
            nxt = scores(r, ns[0])
            for idx, n in enumerate(ns):
                cur = nxt
                if idx + 1 < len(ns):
                    nxt = scores(r, ns[idx + 1])
                finish(r, n, cur)

        u = min(unroll, n_blk)
        for r in range(d):
            if n_blk == u:
                run_blocks(r, [jnp.int32(n) for n in range(n_blk)])
            else:
                def body(it, carry, r=r, run_blocks=run_blocks, u=u):
                    run_blocks(r, [it * u + uu for uu in range(u)])
                    return carry
                lax.fori_loop(0, n_blk // u, body, 0)

    def combine(c, carry):
        rows = pl.ds(pl.multiple_of(c * combine_rows, combine_rows), combine_rows)
        l1, l2, l3 = lse_scr[0, rows, :], lse_scr[1, rows, :], lse_scr[2, rows, :]
        lmax = jnp.maximum(jnp.maximum(l1, l2), l3)
        e1, e2, e3 = jnp.exp(l1 - lmax), jnp.exp(l2 - lmax), jnp.exp(l3 - lmax)
        num = e1 * o_scr[0, rows, :] + e2 * o_scr[1, rows, :] + e3 * o_scr[2, rows, :]
        ob_ref[0, 0, rows, :] = num / (e1 + e2 + e3)
        return carry

    lax.fori_loop(0, seq_len // combine_rows, combine, 0)


def _dilated_call(qkv_heads, d4, d16, *, n_heads_b, q_off, k_off, v_off, unroll=4):
    B, _, S, _ = qkv_heads.shape
    offs_nat = (q_off, k_off, v_off)
    offs_dil = (0, n_heads_b, 2 * n_heads_b)
    in_arrays = [qkv_heads] * 3 + [d4] * 3 + [d16] * 3
    in_specs = (
        [pl.BlockSpec((1, 1, S, HEAD_DIM), lambda b, h, off=off: (b, off + h, 0, 0))
         for off in offs_nat]
        + [pl.BlockSpec((1, 1, 4, S // 4, HEAD_DIM), lambda b, h, off=off: (b, off + h, 0, 0, 0))
           for off in offs_dil]
        + [pl.BlockSpec((1, 1, 16, S // 16, HEAD_DIM), lambda b, h, off=off: (b, off + h, 0, 0, 0))
           for off in offs_dil])
    n_br = len(DIL_PAIRS)
    return pl.pallas_call(
        functools.partial(_dilated_kernel, seq_len=S, unroll=unroll, combine_rows=256),
        grid=(B, n_heads_b),
        in_specs=in_specs,
        out_specs=pl.BlockSpec((1, 1, S, HEAD_DIM), lambda b, h: (b, h, 0, 0)),
        out_shape=jax.ShapeDtypeStruct((B, n_heads_b, S, HEAD_DIM), F32),
        scratch_shapes=[pltpu.VMEM((n_br, S, HEAD_DIM), F32),
                        pltpu.VMEM((n_br, S, HEAD_DIM), F32),
                        pltpu.VMEM((2, DIL_BLOCK, 2 * DIL_BLOCK), F32)],
        compiler_params=pltpu.CompilerParams(
            dimension_semantics=("parallel", "parallel"),
            vmem_limit_bytes=V7X_VMEM_LIMIT_BYTES),
        name="dilated_attn",
    )(*in_arrays)


def _attnout_kernel(x_ref, oa_ref, ob_ref, ga_ref, gb_ref, wo_ref, pg_ref, out_ref, mix_ref):
    col = 0
    for o_ref, g_ref in ((oa_ref, ga_ref), (ob_ref, gb_ref)):
        n_heads = o_ref.shape[1]
        ssq = None
        for h in range(n_heads):
            t = o_ref[0, h]
            part = jnp.sum(t * t, axis=-1, keepdims=True)
            ssq = part if ssq is None else ssq + part
        inv = lax.rsqrt(ssq / (n_heads * HEAD_DIM) + RMS_EPS)
        for h in range(n_heads):
            gs = slice(h * HEAD_DIM, (h + 1) * HEAD_DIM)
            mix_ref[:, col:col + HEAD_DIM] = (o_ref[0, h] * inv * g_ref[:, gs]).astype(BF16)
            col += HEAD_DIM
    y = jnp.dot(mix_ref[...], wo_ref[...], preferred_element_type=F32)
    out_ref[0] = x_ref[0] + y * _rms_scale(y) * pg_ref[...]


def _attnout_call(x, oa, ob, ga, gb, wo_bf16, pg, *, tm=256):
    B, S, D = x.shape
    head_spec_a = pl.BlockSpec((1, oa.shape[1], tm, HEAD_DIM), lambda b, i: (b, 0, i, 0))
    head_spec_b = pl.BlockSpec((1, ob.shape[1], tm, HEAD_DIM), lambda b, i: (b, 0, i, 0))
    return pl.pallas_call(
        _attnout_kernel,
        grid=(B, S // tm),
        in_specs=[
            pl.BlockSpec((1, tm, D), lambda b, i: (b, i, 0)),
            head_spec_a, head_spec_b,
            pl.BlockSpec((1, ga.shape[1]), lambda b, i: (0, 0)),
            pl.BlockSpec((1, gb.shape[1]), lambda b, i: (0, 0)),
            pl.BlockSpec(wo_bf16.shape, lambda b, i: (0, 0)),
            pl.BlockSpec((1, D), lambda b, i: (0, 0)),
        ],
        out_specs=pl.BlockSpec((1, tm, D), lambda b, i: (b, i, 0)),
        out_shape=jax.ShapeDtypeStruct((B, S, D), F32),
        scratch_shapes=[pltpu.VMEM((tm, wo_bf16.shape[0]), BF16)],
        compiler_params=pltpu.CompilerParams(
            dimension_semantics=("parallel", "parallel"),
            vmem_limit_bytes=V7X_VMEM_LIMIT_BYTES),
        name="attn_out",
    )(x, oa, ob, ga, gb, wo_bf16, pg)


def _ffn_kernel(x_ref, xh_ref, g_ref, wg_ref, wu_ref, cw_ref, cb_ref, wd_ref, pg_ref,
                out_ref, hn_ref, acc_ref, *, halo):
    i = pl.program_id(1)
    f = pl.program_id(2)
    tm = x_ref.shape[1]

    @pl.when(f == 0)
    def _():
        x = x_ref[0]
        hn_ref[halo:, :] = (x * _rms_scale(x) * g_ref[...]).astype(BF16)
        xh = xh_ref[0]
        hh = xh * _rms_scale(xh) * g_ref[...]
        hn_ref[0:halo, :] = jnp.where(i > 0, hh, 0.0).astype(BF16)
        acc_ref[...] = jnp.zeros_like(acc_ref)

    gate_ext = jnp.dot(hn_ref[...], wg_ref[...], preferred_element_type=F32)
    up = jnp.dot(hn_ref[halo:, :], wu_ref[...], preferred_element_type=F32)
    cw = cw_ref[...]
    gc = cb_ref[...] + gate_ext[halo - 2:halo - 2 + tm] * cw[0:1, :]
    gc = gc + gate_ext[halo - 1:halo - 1 + tm] * cw[1:2, :]
    gc = gc + gate_ext[halo:halo + tm] * cw[2:3, :]
    gelu = 0.5 * gc * (1.0 + jnp.tanh(0.7978845608028654 * (gc + 0.044715 * (gc * gc * gc))))
    act = (gelu * up).astype(BF16)
    acc_ref[...] += jnp.dot(act, wd_ref[...], preferred_element_type=F32)

    @pl.when(f == pl.num_programs(2) - 1)
    def _():
        y = acc_ref[...]
        out_ref[0] = x_ref[0] + y * _rms_scale(y) * pg_ref[...]


def _ffn_call(x, g, wg_bf16, wu_bf16, conv_w, conv_b, wd_bf16, pg, *, tm=512, tf=1024):
    B, S, D = x.shape
    d_ff = wg_bf16.shape[1]
    halo = BF16_SUBLANE_TILE
    assert CONV_WIDTH - 1 <= halo and tm % halo == 0
    halo_blocks_per_tile = tm // halo
    kern = functools.partial(_ffn_kernel, halo=halo)
    return pl.pallas_call(
        kern,
        grid=(B, S // tm, d_ff // tf),
        in_specs=[
            pl.BlockSpec((1, tm, D), lambda b, i, f: (b, i, 0)),
            pl.BlockSpec((1, halo, D),
                         lambda b, i, f: (b, jnp.maximum(i * halo_blocks_per_tile - 1, 0), 0)),
            pl.BlockSpec((1, D), lambda b, i, f: (0, 0)),
            pl.BlockSpec((D, tf), lambda b, i, f: (0, f)),
            pl.BlockSpec((D, tf), lambda b, i, f: (0, f)),
            pl.BlockSpec((CONV_WIDTH, tf), lambda b, i, f: (0, f)),
            pl.BlockSpec((1, tf), lambda b, i, f: (0, f)),
            pl.BlockSpec((tf, D), lambda b, i, f: (f, 0)),
            pl.BlockSpec((1, D), lambda b, i, f: (0, 0)),
        ],
        out_specs=pl.BlockSpec((1, tm, D), lambda b, i, f: (b, i, 0)),
        out_shape=jax.ShapeDtypeStruct((B, S, D), F32),
        scratch_shapes=[pltpu.VMEM((halo + tm, D), BF16), pltpu.VMEM((tm, D), F32)],
        compiler_params=pltpu.CompilerParams(
            dimension_semantics=("parallel", "parallel", "arbitrary"),
            vmem_limit_bytes=V7X_VMEM_LIMIT_BYTES),
        name="conv_glu_ffn",
    )(x, x, g, wg_bf16, wu_bf16, conv_w, conv_b, wd_bf16, pg)


def _rope_tables(positions):
    inv_freq = ROPE_THETA ** (-jnp.arange(ROT_HALF, dtype=F32) / ROT_HALF)
    ang = positions.astype(F32)[..., None] * inv_freq
    cos, sin = jnp.cos(ang), jnp.sin(ang)
    ones = jnp.ones(ang.shape[:-1] + (HEAD_DIM - ROT_DIM,), F32)
    zeros_tail = jnp.zeros(ang.shape[:-1] + (HEAD_DIM - ROT_HALF,), F32)
    zeros_head = jnp.zeros_like(sin)
    cos_t = jnp.concatenate([cos, cos, ones], axis=-1)
    sa_t = jnp.concatenate([-sin, zeros_tail], axis=-1)
    sb_t = jnp.concatenate([zeros_head, sin, zeros_tail[..., :HEAD_DIM - ROT_DIM]], axis=-1)
    return cos_t, sa_t, sb_t


def kernel(x, positions, attn_pre_g, w_qkv, moba_out_g, dil_out_g, w_o, attn_post_g, ffn_pre_g,
           w_gate, w_up, conv_w, conv_b, w_down, ffn_post_g):
    depth = w_qkv.shape[0]
    n_heads_a = moba_out_g.shape[1] // HEAD_DIM
    n_heads_b = dil_out_g.shape[1] // HEAD_DIM
    cos_t, sa_t, sb_t = _rope_tables(positions)
    for l in range(depth):
        qkv_heads, vt, d4, d16 = _qkv_call(x, attn_pre_g[l][None], w_qkv[l].astype(BF16),
                                           cos_t, sa_t, sb_t,
                                           n_heads_a=n_heads_a, n_heads_b=n_heads_b)
        oa = _moba_call(qkv_heads, vt, n_heads_a=n_heads_a)
        ob = _dilated_call(qkv_heads, d4, d16, n_heads_b=n_heads_b, q_off=3 * n_heads_a,
                           k_off=3 * n_heads_a + n_heads_b, v_off=3 * n_heads_a + 2 * n_heads_b)
        x = _attnout_call(x, oa, ob, moba_out_g[l][None], dil_out_g[l][None],
                          w_o[l].astype(BF16), attn_post_g[l][None])
        x = _ffn_call(x, ffn_pre_g[l][None], w_gate[l].astype(BF16), w_up[l].astype(BF16),
                      conv_w[l], conv_b[l][None], w_down[l].astype(BF16), ffn_post_g[l][None])
    return x
```

```python
import functools

import jax
import jax.numpy as jnp
from jax import lax
from jax.experimental import pallas as pl
from jax.experimental.pallas import tpu as pltpu

F32 = jnp.float32
BF16 = jnp.bfloat16

HEAD_DIM = 128
ROT_DIM = HEAD_DIM // 4
ROT_HALF = ROT_DIM // 2
ROPE_THETA = 500000.0
MOBA_BLOCK = 256
MOBA_TOPK = 3
DIL_PAIRS = ((128, 1), (512, 4), (2048, 16))
DIL_BLOCK = 128
CONV_WIDTH = 3
RMS_EPS = 1e-6
SCALE = HEAD_DIM ** -0.5
SCALE_LOG2E = SCALE * 1.4426950408889634
NEG = -1e30

V7X_VMEM_LIMIT_BYTES = 56 * 1024 * 1024
BF16_SUBLANE_TILE = 16
V7X_MXU_WIDTH = 256
MOBA_VT_ROWS = HEAD_DIM + BF16_SUBLANE_TILE

NT_DIMS = (((1,), (1,)), ((), ()))


def _rms_scale(x):
    return lax.rsqrt(jnp.mean(x * x, axis=-1, keepdims=True) + RMS_EPS)


def _qkv_kernel(x_ref, g_ref, w_ref, cos_ref, sa_ref, sb_ref,
                nat_ref, vt_ref, d4_ref, d16_ref, hn_ref, slab_ref, *, heads_per_tile):
    j = pl.program_id(2)
    tm = hn_ref.shape[0]
    heads_per_dot = V7X_MXU_WIDTH // HEAD_DIM
    dot_width = heads_per_dot * HEAD_DIM

    @pl.when(j == 0)
    def _():
        x = x_ref[0]
        hn_ref[...] = (x * _rms_scale(x) * g_ref[...]).astype(BF16)

    def rope(t):
        return (t * cos_ref[0]
                + pltpu.roll(t, HEAD_DIM - ROT_HALF, 1) * sa_ref[0]
                + pltpu.roll(t, ROT_HALF, 1) * sb_ref[0])

    def run(with_rope, with_vt, with_dilated):
        for c in range(heads_per_tile // heads_per_dot):
            acc = jnp.dot(hn_ref[...], w_ref[:, c * dot_width:(c + 1) * dot_width],
                          preferred_element_type=F32)
            for hh in range(heads_per_dot):
                h = c * heads_per_dot + hh
                t = acc[:, hh * HEAD_DIM:(hh + 1) * HEAD_DIM]
                if with_rope:
                    t = rope(t)
                nat_ref[0, h] = t.astype(BF16)
                if with_vt:
                    for blk in range(tm // MOBA_BLOCK):
                        vt_ref[0, h, blk, 0:HEAD_DIM, :] = (
                            t[blk * MOBA_BLOCK:(blk + 1) * MOBA_BLOCK].T.astype(BF16))
                        vt_ref[0, h, blk, HEAD_DIM:, :] = jnp.ones(
                            (MOBA_VT_ROWS - HEAD_DIM, MOBA_BLOCK), BF16)
                if with_dilated:
                    slab_ref[h] = t
                    for ref, d in ((d4_ref, 4), (d16_ref, 16)):
                        for res in range(d):
                            ref[0, h, res] = slab_ref[h, pl.ds(res, tm // d, stride=d), :].astype(BF16)

    pl.when(j < 2)(lambda: run(True, False, False))
    pl.when(j == 2)(lambda: run(False, True, False))
    pl.when(jnp.logical_or(j == 3, j == 4))(lambda: run(True, False, True))
    pl.when(j == 5)(lambda: run(False, False, True))


def _qkv_call(x, g, w_bf16, cos_t, sa_t, sb_t, *, n_heads_a, n_heads_b, tm=1024):
    B, S, D = x.shape
    N = w_bf16.shape[1]
    assert n_heads_a == n_heads_b and N == 3 * (n_heads_a + n_heads_b) * HEAD_DIM
    hpt = n_heads_a
    tn = hpt * HEAD_DIM
    first_b_tile = 3
    kern = functools.partial(_qkv_kernel, heads_per_tile=hpt)

    def dil_index(b, i, j):
        return (b, jnp.maximum(j - first_b_tile, 0), 0, i, 0)

    return pl.pallas_call(
        kern,
        grid=(B, S // tm, N // tn),
        in_specs=[
            pl.BlockSpec((1, tm, D), lambda b, i, j: (b, i, 0)),
            pl.BlockSpec((1, D), lambda b, i, j: (0, 0)),
            pl.BlockSpec((D, tn), lambda b, i, j: (0, j)),
            pl.BlockSpec((1, tm, HEAD_DIM), lambda b, i, j: (b, i, 0)),
            pl.BlockSpec((1, tm, HEAD_DIM), lambda b, i, j: (b, i, 0)),
            pl.BlockSpec((1, tm, HEAD_DIM), lambda b, i, j: (b, i, 0)),
        ],
        out_specs=[
            pl.BlockSpec((1, hpt, tm, HEAD_DIM), lambda b, i, j: (b, j, i, 0)),
            pl.BlockSpec((1, hpt, tm // MOBA_BLOCK, MOBA_VT_ROWS, MOBA_BLOCK),
                         lambda b, i, j: (b, 0, i, 0, 0)),
            pl.BlockSpec((1, hpt, 4, tm // 4, HEAD_DIM), dil_index),
            pl.BlockSpec((1, hpt, 16, tm // 16, HEAD_DIM), dil_index),
        ],
        out_shape=[
            jax.ShapeDtypeStruct((B, N // HEAD_DIM, S, HEAD_DIM), BF16),
            jax.ShapeDtypeStruct((B, n_heads_a, S // MOBA_BLOCK, MOBA_VT_ROWS, MOBA_BLOCK), BF16),
            jax.ShapeDtypeStruct((B, 3 * n_heads_b, 4, S // 4, HEAD_DIM), BF16),
            jax.ShapeDtypeStruct((B, 3 * n_heads_b, 16, S // 16, HEAD_DIM), BF16),
        ],
        scratch_shapes=[pltpu.VMEM((tm, D), BF16), pltpu.VMEM((hpt, tm, HEAD_DIM), F32)],
        compiler_params=pltpu.CompilerParams(
            dimension_semantics=("parallel", "parallel", "arbitrary"),
            vmem_limit_bytes=V7X_VMEM_LIMIT_BYTES),
        name="qkv_rope",
    )(x, g, w_bf16, cos_t, sa_t, sb_t)


def _moba_kernel(q_ref, k_ref, vt_ref, o_ref, kmean_ref, bias_ref, acc_ref, *, n_blocks, heads,
                 lookahead=8, group=2):
    assert n_blocks % group == 0
    i = pl.program_id(2)
    blk_sz = MOBA_BLOCK

    @pl.when(i == 0)
    def _():
        for h in range(heads):
            for blk in range(n_blocks):
                kb = k_ref[0, h, blk * blk_sz:(blk + 1) * blk_sz, :].astype(F32)
                kmean_ref[h, blk:blk + 1, :] = jnp.mean(kb, axis=0, keepdims=True)

    def select_blocks(h, q):
        km = kmean_ref[h]
        km_hi = km.astype(BF16)
        km_lo = (km - km_hi.astype(F32)).astype(BF16)
        gate = (lax.dot_general(km_hi, q, NT_DIMS, preferred_element_type=F32)
                + lax.dot_general(km_lo, q, NT_DIMS, preferred_element_type=F32))
        blk_id = lax.broadcasted_iota(jnp.int32, gate.shape, 0).astype(F32)
        neg_inf = jnp.float32(-jnp.inf)
        g = jnp.where(blk_id < i.astype(F32), gate, neg_inf)
        sel = jnp.zeros(gate.shape, dtype=jnp.bool_)
        for _ in range(MOBA_TOPK):
            m = jnp.max(g, axis=0, keepdims=True)
            first = jnp.min(jnp.where(g == m, blk_id, float(n_blocks)), axis=0, keepdims=True)
            pick = jnp.logical_and(blk_id == first, m > neg_inf)
            sel = jnp.logical_or(sel, pick)
            g = jnp.where(pick, neg_inf, g)
        bias_ref[h] = jnp.where(sel, 0.0, NEG).astype(F32)

    def scores(h, first_blk, n_blk):
        rows = n_blk * blk_sz
        kb = k_ref[0, h, pl.ds(pl.multiple_of(first_blk * blk_sz, blk_sz), rows), :]
        return lax.dot_general(kb, q_ref[0, h], NT_DIMS, preferred_element_type=F32)

    def pipelined(stage_a, stage_b):
        ahead = [stage_a(h) for h in range(min(lookahead, heads))]
        outs = []
        for h in range(heads):
            if h + lookahead < heads:
                ahead.append(stage_a(h + lookahead))
            outs.append(stage_b(h, ahead[h]))
        return outs

    def own_a(h):
        select_blocks(h, q_ref[0, h])
        return scores(h, i, 1)

    def own_b(h, s):
        key_pos = lax.broadcasted_iota(jnp.int32, s.shape, 0)
        q_pos = lax.broadcasted_iota(jnp.int32, s.shape, 1)
        t = jnp.where(key_pos <= q_pos, s * SCALE_LOG2E, NEG)
        m0 = jnp.max(t, axis=0, keepdims=True)
        p = jnp.exp2(t - m0)
        acc_ref[h] = jnp.dot(vt_ref[0, h, i], p.astype(BF16), preferred_element_type=F32)
        return m0

    init = pipelined(own_a, own_b)

    def body(c, carry):
        first = c * group

        def past_b(h, s):
            m_prev = carry[h]
            chunks = [s[g * blk_sz:(g + 1) * blk_sz] for g in range(group)]
            brows = [bias_ref[h, pl.ds(first + g, 1), :] for g in range(group)]
            m_new = m_prev
            for sg, brow in zip(chunks, brows):
                m_new = jnp.maximum(m_new, jnp.max(sg, axis=0, keepdims=True) * SCALE_LOG2E + brow)
            acc_new = jnp.exp2(m_prev - m_new) * acc_ref[h]
            for g, (sg, brow) in enumerate(zip(chunks, brows)):
                pg = jnp.exp2(sg * SCALE_LOG2E - (m_new - brow))
                acc_new = acc_new + jnp.dot(vt_ref[0, h, first + g], pg.astype(BF16),
                                            preferred_element_type=F32)
            acc_ref[h] = acc_new
            return m_new

        return tuple(pipelined(lambda h: scores(h, first, group), past_b))

    lax.fori_loop(0, (i + group - 1) // group, body, tuple(init))
    for h in range(heads):
        acc_fin = acc_ref[h]
        o_ref[0, h] = (acc_fin[:HEAD_DIM] / acc_fin[HEAD_DIM:HEAD_DIM + 1]).T


def _moba_call(qkv_heads, vt, *, n_heads_a, heads_per_step=8):
    B, _, S, _ = qkv_heads.shape
    n_blocks = S // MOBA_BLOCK
    tq = MOBA_BLOCK
    hps = heads_per_step
    assert n_heads_a % hps == 0
    k_first = n_heads_a // hps
    kern = functools.partial(_moba_kernel, n_blocks=n_blocks, heads=hps)
    return pl.pallas_call(
        kern,
        grid=(B, n_heads_a // hps, S // tq),
        in_specs=[
            pl.BlockSpec((1, hps, tq, HEAD_DIM), lambda b, h, i: (b, h, i, 0)),
            pl.BlockSpec((1, hps, S, HEAD_DIM), lambda b, h, i: (b, k_first + h, 0, 0)),
            pl.BlockSpec((1, hps, n_blocks, MOBA_VT_ROWS, MOBA_BLOCK),
                         lambda b, h, i: (b, h, 0, 0, 0)),
        ],
        out_specs=pl.BlockSpec((1, hps, tq, HEAD_DIM), lambda b, h, i: (b, h, i, 0)),
        out_shape=jax.ShapeDtypeStruct((B, n_heads_a, S, HEAD_DIM), F32),
        scratch_shapes=[pltpu.VMEM((hps, n_blocks, HEAD_DIM), F32),
                        pltpu.VMEM((hps, n_blocks, tq), F32),
                        pltpu.VMEM((hps, MOBA_VT_ROWS, tq), F32)],
        compiler_params=pltpu.CompilerParams(
            dimension_semantics=("parallel", "parallel", "arbitrary"),
            vmem_limit_bytes=V7X_VMEM_LIMIT_BYTES),
        name="moba_attn",
    )(qkv_heads, qkv_heads, vt)


def _dilated_kernel(q1_ref, k1_ref, v1_ref, q4_ref, k4_ref, v4_ref, q16_ref, k16_ref, v16_ref,
                    ob_ref, o_scr, lse_scr, bias_scr, *, seq_len, unroll, combine_rows):
    blk = DIL_BLOCK
    branch_refs = ((q1_ref, k1_ref, v1_ref), (q4_ref, k4_ref, v4_ref), (q16_ref, k16_ref, v16_ref))

    qi = lax.broadcasted_iota(jnp.int32, (blk, 2 * blk), 0)
    ki = lax.broadcasted_iota(jnp.int32, (blk, 2 * blk), 1)
    dist = qi + blk - ki
    bias_scr[0] = jnp.where(jnp.logical_and(dist >= 0, dist <= blk), 0.0, NEG).astype(F32)
    bias_scr[1] = jnp.where(ki <= qi, 0.0, NEG).astype(F32)

    for g, (window, d) in enumerate(DIL_PAIRS):
        assert window // d == DIL_BLOCK
        q_ref, k_ref, v_ref = branch_refs[g]
        n_blk = seq_len // d // blk

        def rows_of(ref, r, start, size, d=d):
            if d == 1:
                return ref[0, 0, pl.ds(start, size), :]
            return ref[0, 0, r, pl.ds(start, size), :]

        def key_start(n):
            return pl.multiple_of(jnp.maximum(n - 1, 0) * blk, blk)

        def scores(r, n, q_ref=q_ref, k_ref=k_ref, rows_of=rows_of):
            qb = rows_of(q_ref, r, pl.multiple_of(n * blk, blk), blk)
            kb = rows_of(k_ref, r, key_start(n), 2 * blk)
            return lax.dot_general(qb, kb, NT_DIMS, preferred_element_type=F32)

        def finish(r, n, s, g=g, d=d, v_ref=v_ref, rows_of=rows_of):
            s = s * SCALE + bias_scr[jnp.where(n == 0, 1, 0)]
            m = jnp.max(s, axis=-1, keepdims=True)
            p = jnp.exp(s - m)
            den = jnp.sum(p, axis=-1, keepdims=True)
            vb = rows_of(v_ref, r, key_start(n), 2 * blk)
            o = jnp.dot(p.astype(BF16), vb, preferred_element_type=F32) / den
            lse = jnp.broadcast_to(m + jnp.log(den), (blk, HEAD_DIM))
            if d == 1:
                rows = pl.ds(pl.multiple_of(n * blk, blk), blk)
            else:
                rows = pl.ds(pl.multiple_of(n * (blk * d), blk) + r, blk, stride=d)
            o_scr[g, rows, :] = o
            lse_scr[g, rows, :] = lse

        def run_blocks(r, ns, scores=scores, finish=finish):
            nxt = scores(r, ns[0])
            for idx, n in enumerate(ns):
                cur = nxt
                if idx + 1 < len(ns):
                    nxt = scores(r, ns[idx + 1])
                finish(r, n, cur)

        u = min(unroll, n_blk)
        for r in range(d):
            if n_blk == u:
                run_blocks(r, [jnp.int32(n) for n in range(n_blk)])
            else:
                def body(it, carry, r=r, run_blocks=run_blocks, u=u):
                    run_blocks(r, [it * u + uu for uu in range(u)])
                    return carry
                lax.fori_loop(0, n_blk // u, body, 0)

    def combine(c, carry):
        rows = pl.ds(pl.multiple_of(c * combine_rows, combine_rows), combine_rows)
        l1, l2, l3 = lse_scr[0, rows, :], lse_scr[1, rows, :], lse_scr[2, rows, :]
        lmax = jnp.maximum(jnp.maximum(l1, l2), l3)
        e1, e2, e3 = jnp.exp(l1 - lmax), jnp.exp(l2 - lmax), jnp.exp(l3 - lmax)
        num = e1 * o_scr[0, rows, :] + e2 * o_scr[1, rows, :] + e3 * o_scr[2, rows, :]
        ob_ref[0, 0, rows, :] = num / (e1 + e2 + e3)
        return carry

    lax.fori_loop(0, seq_len // combine_rows, combine, 0)


def _dilated_call(qkv_heads, d4, d16, *, n_heads_b, q_off, k_off, v_off, unroll=4):
    B, _, S, _ = qkv_heads.shape
    offs_nat = (q_off, k_off, v_off)
    offs_dil = (0, n_heads_b, 2 * n_heads_b)
    in_arrays = [qkv_heads] * 3 + [d4] * 3 + [d16] * 3
    in_specs = (
        [pl.BlockSpec((1, 1, S, HEAD_DIM), lambda b, h, off=off: (b, off + h, 0, 0))
         for off in offs_nat]
        + [pl.BlockSpec((1, 1, 4, S // 4, HEAD_DIM), lambda b, h, off=off: (b, off + h, 0, 0, 0))
           for off in offs_dil]
        + [pl.BlockSpec((1, 1, 16, S // 16, HEAD_DIM), lambda b, h, off=off: (b, off + h, 0, 0, 0))
           for off in offs_dil])
    n_br = len(DIL_PAIRS)
    return pl.pallas_call(
        functools.partial(_dilated_kernel, seq_len=S, unroll=unroll, combine_rows=256),
        grid=(B, n_heads_b),
        in_specs=in_specs,
        out_specs=pl.BlockSpec((1, 1, S, HEAD_DIM), lambda b, h: (b, h, 0, 0)),
        out_shape=jax.ShapeDtypeStruct((B, n_heads_b, S, HEAD_DIM), F32),
        scratch_shapes=[pltpu.VMEM((n_br, S, HEAD_DIM), F32),
                        pltpu.VMEM((n_br, S, HEAD_DIM), F32),
                        pltpu.VMEM((2, DIL_BLOCK, 2 * DIL_BLOCK), F32)],
        compiler_params=pltpu.CompilerParams(
            dimension_semantics=("parallel", "parallel"),
            vmem_limit_bytes=V7X_VMEM_LIMIT_BYTES),
        name="dilated_attn",
    )(*in_arrays)


def _attnout_kernel(x_ref, oa_ref, ob_ref, ga_ref, gb_ref, wo_ref, pg_ref, out_ref, mix_ref):
    col = 0
    for o_ref, g_ref in ((oa_ref, ga_ref), (ob_ref, gb_ref)):
        n_heads = o_ref.shape[1]
        ssq = None
        for h in range(n_heads):
            t = o_ref[0, h]
            part = jnp.sum(t * t, axis=-1, keepdims=True)
            ssq = part if ssq is None else ssq + part
        inv = lax.rsqrt(ssq / (n_heads * HEAD_DIM) + RMS_EPS)
        for h in range(n_heads):
            gs = slice(h * HEAD_DIM, (h + 1) * HEAD_DIM)
            mix_ref[:, col:col + HEAD_DIM] = (o_ref[0, h] * inv * g_ref[:, gs]).astype(BF16)
            col += HEAD_DIM
    y = jnp.dot(mix_ref[...], wo_ref[...], preferred_element_type=F32)
    out_ref[0] = x_ref[0] + y * _rms_scale(y) * pg_ref[...]


def _attnout_call(x, oa, ob, ga, gb, wo_bf16, pg, *, tm=256):
    B, S, D = x.shape
    head_spec_a = pl.BlockSpec((1, oa.shape[1], tm, HEAD_DIM), lambda b, i: (b, 0, i, 0))
    head_spec_b = pl.BlockSpec((1, ob.shape[1], tm, HEAD_DIM), lambda b, i: (b, 0, i, 0))
    return pl.pallas_call(
        _attnout_kernel,
        grid=(B, S // tm),
        in_specs=[
            pl.BlockSpec((1, tm, D), lambda b, i: (b, i, 0)),
            head_spec_a, head_spec_b,
            pl.BlockSpec((1, ga.shape[1]), lambda b, i: (0, 0)),
            pl.BlockSpec((1, gb.shape[1]), lambda b, i: (0, 0)),
            pl.BlockSpec(wo_bf16.shape, lambda b, i: (0, 0)),
            pl.BlockSpec((1, D), lambda b, i: (0, 0)),
        ],
        out_specs=pl.BlockSpec((1, tm, D), lambda b, i: (b, i, 0)),
        out_shape=jax.ShapeDtypeStruct((B, S, D), F32),
        scratch_shapes=[pltpu.VMEM((tm, wo_bf16.shape[0]), BF16)],
        compiler_params=pltpu.CompilerParams(
            dimension_semantics=("parallel", "parallel"),
            vmem_limit_bytes=V7X_VMEM_LIMIT_BYTES),
        name="attn_out",
    )(x, oa, ob, ga, gb, wo_bf16, pg)


def _ffn_kernel(x_ref, xh_ref, g_ref, wg_ref, wu_ref, cw_ref, cb_ref, wd_ref, pg_ref,
                out_ref, hn_ref, acc_ref, *, halo):
    i = pl.program_id(1)
    f = pl.program_id(2)
    tm = x_ref.shape[1]

    @pl.when(f == 0)
    def _():
        x = x_ref[0]
        hn_ref[halo:, :] = (x * _rms_scale(x) * g_ref[...]).astype(BF16)
        xh = xh_ref[0]
        hh = xh * _rms_scale(xh) * g_ref[...]
        hn_ref[0:halo, :] = jnp.where(i > 0, hh, 0.0).astype(BF16)
        acc_ref[...] = jnp.zeros_like(acc_ref)

    gate_ext = jnp.dot(hn_ref[...], wg_ref[...], preferred_element_type=F32)
    up = jnp.dot(hn_ref[halo:, :], wu_ref[...], preferred_element_type=F32)
    cw = cw_ref[...]
    gc = cb_ref[...] + gate_ext[halo - 2:halo - 2 + tm] * cw[0:1, :]
    gc = gc + gate_ext[halo - 1:halo - 1 + tm] * cw[1:2, :]
    gc = gc + gate_ext[halo:halo + tm] * cw[2:3, :]
    gelu = 0.5 * gc * (1.0 + jnp.tanh(0.7978845608028654 * (gc + 0.044715 * (gc * gc * gc))))
    act = (gelu * up).astype(BF16)
    acc_ref[...] += jnp.dot(act, wd_ref[...], preferred_element_type=F32)

    @pl.when(f == pl.num_programs(2) - 1)
    def _():
        y = acc_ref[...]
        out_ref[0] = x_ref[0] + y * _rms_scale(y) * pg_ref[...]


def _ffn_call(x, g, wg_bf16, wu_bf16, conv_w, conv_b, wd_bf16, pg, *, tm=512, tf=1024):
    B, S, D = x.shape
    d_ff = wg_bf16.shape[1]
    halo = BF16_SUBLANE_TILE
    assert CONV_WIDTH - 1 <= halo and tm % halo == 0
    halo_blocks_per_tile = tm // halo
    kern = functools.partial(_ffn_kernel, halo=halo)
    return pl.pallas_call(
        kern,
        grid=(B, S // tm, d_ff // tf),
        in_specs=[
            pl.BlockSpec((1, tm, D), lambda b, i, f: (b, i, 0)),
            pl.BlockSpec((1, halo, D),
                         lambda b, i, f: (b, jnp.maximum(i * halo_blocks_per_tile - 1, 0), 0)),
            pl.BlockSpec((1, D), lambda b, i, f: (0, 0)),
            pl.BlockSpec((D, tf), lambda b, i, f: (0, f)),
            pl.BlockSpec((D, tf), lambda b, i, f: (0, f)),
            pl.BlockSpec((CONV_WIDTH, tf), lambda b, i, f: (0, f)),
            pl.BlockSpec((1, tf), lambda b, i, f: (0, f)),
            pl.BlockSpec((tf, D), lambda b, i, f: (f, 0)),
            pl.BlockSpec((1, D), lambda b, i, f: (0, 0)),
        ],
        out_specs=pl.BlockSpec((1, tm, D), lambda b, i, f: (b, i, 0)),
        out_shape=jax.ShapeDtypeStruct((B, S, D), F32),
        scratch_shapes=[pltpu.VMEM((halo + tm, D), BF16), pltpu.VMEM((tm, D), F32)],
        compiler_params=pltpu.CompilerParams(
            dimension_semantics=("parallel", "parallel", "arbitrary"),
            vmem_limit_bytes=V7X_VMEM_LIMIT_BYTES),
        name="conv_glu_ffn",
    )(x, x, g, wg_bf16, wu_bf16, conv_w, conv_b, wd_bf16, pg)


def _rope_tables(positions):
    inv_freq = ROPE_THETA ** (-jnp.arange(ROT_HALF, dtype=F32) / ROT_HALF)
    ang = positions.astype(F32)[..., None] * inv_freq
    cos, sin = jnp.cos(ang), jnp.sin(ang)
    ones = jnp.ones(ang.shape[:-1] + (HEAD_DIM - ROT_DIM,), F32)
    zeros_tail = jnp.zeros(ang.shape[:-1] + (HEAD_DIM - ROT_HALF,), F32)
    zeros_head = jnp.zeros_like(sin)
    cos_t = jnp.concatenate([cos, cos, ones], axis=-1)
    sa_t = jnp.concatenate([-sin, zeros_tail], axis=-1)
    sb_t = jnp.concatenate([zeros_head, sin, zeros_tail[..., :HEAD_DIM - ROT_DIM]], axis=-1)
    return cos_t, sa_t, sb_t


def kernel(x, positions, attn_pre_g, w_qkv, moba_out_g, dil_out_g, w_o, attn_post_g, ffn_pre_g,
           w_gate, w_up, conv_w, conv_b, w_down, ffn_post_g):
    depth = w_qkv.shape[0]
    n_heads_a = moba_out_g.shape[1] // HEAD_DIM
    n_heads_b = dil_out_g.shape[1] // HEAD_DIM
    cos_t, sa_t, sb_t = _rope_tables(positions)
    for l in range(depth):
        qkv_heads, vt, d4, d16 = _qkv_call(x, attn_pre_g[l][None], w_qkv[l].astype(BF16),
                                           cos_t, sa_t, sb_t,
                                           n_heads_a=n_heads_a, n_heads_b=n_heads_b)
        oa = _moba_call(qkv_heads, vt, n_heads_a=n_heads_a)
        ob = _dilated_call(qkv_heads, d4, d16, n_heads_b=n_heads_b, q_off=3 * n_heads_a,
                           k_off=3 * n_heads_a + n_heads_b, v_off=3 * n_heads_a + 2 * n_heads_b)
        x = _attnout_call(x, oa, ob, moba_out_g[l][None], dil_out_g[l][None],
                          w_o[l].astype(BF16), attn_post_g[l][None])
        x = _ffn_call(x, ffn_pre_g[l][None], w_gate[l].astype(BF16), w_up[l].astype(BF16),
                      conv_w[l], conv_b[l][None], w_down[l].astype(BF16), ffn_post_g[l][None])
    return x
```

```python
import functools

import jax
import jax.numpy as jnp
from jax import lax
from jax.experimental import pallas as pl
from jax.experimental.pallas import tpu as pltpu

F32 = jnp.float32
BF16 = jnp.bfloat16

HEAD_DIM = 128
ROT_DIM = HEAD_DIM // 4
ROT_HALF = ROT_DIM // 2
ROPE_THETA = 500000.0
MOBA_BLOCK = 256
MOBA_TOPK = 3
DIL_PAIRS = ((128, 1), (512, 4), (2048, 16))
DIL_BLOCK = 128
CONV_WIDTH = 3
RMS_EPS = 1e-6
SCALE = HEAD_DIM ** -0.5
SCALE_LOG2E = SCALE * 1.4426950408889634
NEG = -1e30

V7X_VMEM_LIMIT_BYTES = 56 * 1024 * 1024
BF16_SUBLANE_TILE = 16
V7X_MXU_WIDTH = 256
MOBA_VT_ROWS = HEAD_DIM + BF16_SUBLANE_TILE

NT_DIMS = (((1,), (1,)), ((), ()))


def _rms_scale(x):
    return lax.rsqrt(jnp.mean(x * x, axis=-1, keepdims=True) + RMS_EPS)


def _qkv_kernel(x_ref, g_ref, w_ref, cos_ref, sa_ref, sb_ref,
                nat_ref, vt_ref, d4_ref, d16_ref, hn_ref, slab_ref, *, heads_per_tile):
    j = pl.program_id(2)
    tm = hn_ref.shape[0]
    heads_per_dot = V7X_MXU_WIDTH // HEAD_DIM
    dot_width = heads_per_dot * HEAD_DIM

    @pl.when(j == 0)
    def _():
        x = x_ref[0]
        hn_ref[...] = (x * _rms_scale(x) * g_ref[...]).astype(BF16)

    def rope(t):
        return (t * cos_ref[0]
                + pltpu.roll(t, HEAD_DIM - ROT_HALF, 1) * sa_ref[0]
                + pltpu.roll(t, ROT_HALF, 1) * sb_ref[0])

    def run(with_rope, with_vt, with_dilated):
        for c in range(heads_per_tile // heads_per_dot):
            acc = jnp.dot(hn_ref[...], w_ref[:, c * dot_width:(c + 1) * dot_width],
                          preferred_element_type=F32)
            for hh in range(heads_per_dot):
                h = c * heads_per_dot + hh
                t = acc[:, hh * HEAD_DIM:(hh + 1) * HEAD_DIM]
                if with_rope:
                    t = rope(t)
                nat_ref[0, h] = t.astype(BF16)
                if with_vt:
                    for blk in range(tm // MOBA_BLOCK):
                        vt_ref[0, h, blk, 0:HEAD_DIM, :] = (
                            t[blk * MOBA_BLOCK:(blk + 1) * MOBA_BLOCK].T.astype(BF16))
                        vt_ref[0, h, blk, HEAD_DIM:, :] = jnp.ones(
                            (MOBA_VT_ROWS - HEAD_DIM, MOBA_BLOCK), BF16)
                if with_dilated:
                    slab_ref[h] = t
                    for ref, d in ((d4_ref, 4), (d16_ref, 16)):
                        for res in range(d):
                            ref[0, h, res] = slab_ref[h, pl.ds(res, tm // d, stride=d), :].astype(BF16)

    pl.when(j < 2)(lambda: run(True, False, False))
    pl.when(j == 2)(lambda: run(False, True, False))
    pl.when(jnp.logical_or(j == 3, j == 4))(lambda: run(True, False, True))
    pl.when(j == 5)(lambda: run(False, False, True))


def _qkv_call(x, g, w_bf16, cos_t, sa_t, sb_t, *, n_heads_a, n_heads_b, tm=1024):
    B, S, D = x.shape
    N = w_bf16.shape[1]
    assert n_heads_a == n_heads_b and N == 3 * (n_heads_a + n_heads_b) * HEAD_DIM
    hpt = n_heads_a
    tn = hpt * HEAD_DIM
    first_b_tile = 3
    kern = functools.partial(_qkv_kernel, heads_per_tile=hpt)

    def dil_index(b, i, j):
        return (b, jnp.maximum(j - first_b_tile, 0), 0, i, 0)

    return pl.pallas_call(
        kern,
        grid=(B, S // tm, N // tn),
        in_specs=[
            pl.BlockSpec((1, tm, D), lambda b, i, j: (b, i, 0)),
            pl.BlockSpec((1, D), lambda b, i, j: (0, 0)),
            pl.BlockSpec((D, tn), lambda b, i, j: (0, j)),
            pl.BlockSpec((1, tm, HEAD_DIM), lambda b, i, j: (b, i, 0)),
            pl.BlockSpec((1, tm, HEAD_DIM), lambda b, i, j: (b, i, 0)),
            pl.BlockSpec((1, tm, HEAD_DIM), lambda b, i, j: (b, i, 0)),
        ],
        out_specs=[
            pl.BlockSpec((1, hpt, tm, HEAD_DIM), lambda b, i, j: (b, j, i, 0)),
            pl.BlockSpec((1, hpt, tm // MOBA_BLOCK, MOBA_VT_ROWS, MOBA_BLOCK),
                         lambda b, i, j: (b, 0, i, 0, 0)),
            pl.BlockSpec((1, hpt, 4, tm // 4, HEAD_DIM), dil_index),
            pl.BlockSpec((1, hpt, 16, tm // 16, HEAD_DIM), dil_index),
        ],
        out_shape=[
            jax.ShapeDtypeStruct((B, N // HEAD_DIM, S, HEAD_DIM), BF16),
            jax.ShapeDtypeStruct((B, n_heads_a, S // MOBA_BLOCK, MOBA_VT_ROWS, MOBA_BLOCK), BF16),
            jax.ShapeDtypeStruct((B, 3 * n_heads_b, 4, S // 4, HEAD_DIM), BF16),
            jax.ShapeDtypeStruct((B, 3 * n_heads_b, 16, S // 16, HEAD_DIM), BF16),
        ],
        scratch_shapes=[pltpu.VMEM((tm, D), BF16), pltpu.VMEM((hpt, tm, HEAD_DIM), F32)],
        compiler_params=pltpu.CompilerParams(
            dimension_semantics=("parallel", "parallel", "arbitrary"),
            vmem_limit_bytes=V7X_VMEM_LIMIT_BYTES),
        name="qkv_rope",
    )(x, g, w_bf16, cos_t, sa_t, sb_t)


def _moba_kernel(q_ref, k_ref, vt_ref, o_ref, kmean_ref, bias_ref, acc_ref, *, n_blocks, heads,
                 lookahead=8, group=2):
    assert n_blocks % group == 0
    i = pl.program_id(2)
    blk_sz = MOBA_BLOCK

    @pl.when(i == 0)
    def _():
        for h in range(heads):
            for blk in range(n_blocks):
                kb = k_ref[0, h, blk * blk_sz:(blk + 1) * blk_sz, :].astype(F32)
                kmean_ref[h, blk:blk + 1, :] = jnp.mean(kb, axis=0, keepdims=True)

    def select_blocks(h, q):
        km = kmean_ref[h]
        km_hi = km.astype(BF16)
        km_lo = (km - km_hi.astype(F32)).astype(BF16)
        gate = (lax.dot_general(km_hi, q, NT_DIMS, preferred_element_type=F32)
                + lax.dot_general(km_lo, q, NT_DIMS, preferred_element_type=F32))
        blk_id = lax.broadcasted_iota(jnp.int32, gate.shape, 0).astype(F32)
        neg_inf = jnp.float32(-jnp.inf)
        g = jnp.where(blk_id < i.astype(F32), gate, neg_inf)
        sel = jnp.zeros(gate.shape, dtype=jnp.bool_)
        for _ in range(MOBA_TOPK):
            m = jnp.max(g, axis=0, keepdims=True)
            first = jnp.min(jnp.where(g == m, blk_id, float(n_blocks)), axis=0, keepdims=True)
            pick = jnp.logical_and(blk_id == first, m > neg_inf)
            sel = jnp.logical_or(sel, pick)
            g = jnp.where(pick, neg_inf, g)
        bias_ref[h] = jnp.where(sel, 0.0, NEG).astype(F32)

    def scores(h, first_blk, n_blk):
        rows = n_blk * blk_sz
        kb = k_ref[0, h, pl.ds(pl.multiple_of(first_blk * blk_sz, blk_sz), rows), :]
        return lax.dot_general(kb, q_ref[0, h], NT_DIMS, preferred_element_type=F32)

    def pipelined(stage_a, stage_b):
        ahead = [stage_a(h) for h in range(min(lookahead, heads))]
        outs = []
        for h in range(heads):
            if h + lookahead < heads:
                ahead.append(stage_a(h + lookahead))
            outs.append(stage_b(h, ahead[h]))
        return outs

    def own_a(h):
        select_blocks(h, q_ref[0, h])
        return scores(h, i, 1)

    def own_b(h, s):
        key_pos = lax.broadcasted_iota(jnp.int32, s.shape, 0)
        q_pos = lax.broadcasted_iota(jnp.int32, s.shape, 1)
        t = jnp.where(key_pos <= q_pos, s * SCALE_LOG2E, NEG)
        m0 = jnp.max(t, axis=0, keepdims=True)
        p = jnp.exp2(t - m0)
        acc_ref[h] = jnp.dot(vt_ref[0, h, i], p.astype(BF16), preferred_element_type=F32)
        return m0

    init = pipelined(own_a, own_b)

    def body(c, carry):
        first = c * group

        def past_b(h, s):
            m_prev = carry[h]
            chunks = [s[g * blk_sz:(g + 1) * blk_sz] for g in range(group)]
            brows = [bias_ref[h, pl.ds(first + g, 1), :] for g in range(group)]
            m_new = m_prev
            for sg, brow in zip(chunks, brows):
                m_new = jnp.maximum(m_new, jnp.max(sg, axis=0, keepdims=True) * SCALE_LOG2E + brow)
            acc_new = jnp.exp2(m_prev - m_new) * acc_ref[h]
            for g, (sg, brow) in enumerate(zip(chunks, brows)):
                pg = jnp.exp2(sg * SCALE_LOG2E - (m_new - brow))
                acc_new = acc_new + jnp.dot(vt_ref[0, h, first + g], pg.astype(BF16),
                                            preferred_element_type=F32)
            acc_ref[h] = acc_new
            return m_new

        return tuple(pipelined(lambda h: scores(h, first, group), past_b))

    lax.fori_loop(0, (i + group - 1) // group, body, tuple(init))
    for h in range(heads):
        acc_fin = acc_ref[h]
        o_ref[0, h] = (acc_fin[:HEAD_DIM] / acc_fin[HEAD_DIM:HEAD_DIM + 1]).T


def _moba_call(qkv_heads, vt, *, n_heads_a, heads_per_step=8):
    B, _, S, _ = qkv_heads.shape
    n_blocks = S // MOBA_BLOCK
    tq = MOBA_BLOCK
    hps = heads_per_step
    assert n_heads_a % hps == 0
    k_first = n_heads_a // hps
    kern = functools.partial(_moba_kernel, n_blocks=n_blocks, heads=hps)
    return pl.pallas_call(
        kern,
        grid=(B, n_heads_a // hps, S // tq),
        in_specs=[
            pl.BlockSpec((1, hps, tq, HEAD_DIM), lambda b, h, i: (b, h, i, 0)),
            pl.BlockSpec((1, hps, S, HEAD_DIM), lambda b, h, i: (b, k_first + h, 0, 0)),
            pl.BlockSpec((1, hps, n_blocks, MOBA_VT_ROWS, MOBA_BLOCK),
                         lambda b, h, i: (b, h, 0, 0, 0)),
        ],
        out_specs=pl.BlockSpec((1, hps, tq, HEAD_DIM), lambda b, h, i: (b, h, i, 0)),
        out_shape=jax.ShapeDtypeStruct((B, n_heads_a, S, HEAD_DIM), F32),
        scratch_shapes=[pltpu.VMEM((hps, n_blocks, HEAD_DIM), F32),
                        pltpu.VMEM((hps, n_blocks, tq), F32),
                        pltpu.VMEM((hps, MOBA_VT_ROWS, tq), F32)],
        compiler_params=pltpu.CompilerParams(
            dimension_semantics=("parallel", "parallel", "arbitrary"),
            vmem_limit_bytes=V7X_VMEM_LIMIT_BYTES),
        name="moba_attn",
    )(qkv_heads, qkv_heads, vt)


def _dilated_kernel(q1_ref, k1_ref, v1_ref, q4_ref, k4_ref, v4_ref, q16_ref, k16_ref, v16_ref,
                    ob_ref, o_scr, lse_scr, bias_scr, *, seq_len, blocks_per_iter, combine_rows):
    blk = DIL_BLOCK
    branch_refs = ((q1_ref, k1_ref, v1_ref), (q4_ref, k4_ref, v4_ref), (q16_ref, k16_ref, v16_ref))

    qi = lax.broadcasted_iota(jnp.int32, (blk, 2 * blk), 0)
    ki = lax.broadcasted_iota(jnp.int32, (blk, 2 * blk), 1)
    dist = qi + blk - ki
    bias_scr[0] = jnp.where(jnp.logical_and(dist >= 0, dist <= blk), 0.0, NEG).astype(F32)
    bias_scr[1] = jnp.where(ki <= qi, 0.0, NEG).astype(F32)

    for g, (window, d) in enumerate(DIL_PAIRS):
        assert window // d == DIL_BLOCK
        q_ref, k_ref, v_ref = branch_refs[g]
        n_blk = seq_len // d // blk

        def rows_of(ref, r, start, size, d=d):
            if d == 1:
                return ref[0, 0, pl.ds(start, size), :]
            return ref[0, 0, r, pl.ds(start, size), :]

        def key_start(n):
            return pl.multiple_of(jnp.maximum(n - 1, 0) * blk, blk)

        def scores(r, n, q_ref=q_ref, k_ref=k_ref, rows_of=rows_of):
            qb = rows_of(q_ref, r, pl.multiple_of(n * blk, blk), blk)
            kb = rows_of(k_ref, r, key_start(n), 2 * blk)
            return lax.dot_general(qb, kb, NT_DIMS, preferred_element_type=F32)

        def finish(r, n, s, g=g, d=d, v_ref=v_ref, rows_of=rows_of):
            t = s * SCALE_LOG2E + bias_scr[jnp.where(n == 0, 1, 0)]
            m = jnp.max(t, axis=-1, keepdims=True)
            p = jnp.exp2(t - m)
            vb = rows_of(v_ref, r, key_start(n), 2 * blk)
            v_ones = jnp.concatenate([vb, jnp.ones_like(vb)], axis=1)
            pv = jnp.dot(p.astype(BF16), v_ones, preferred_element_type=F32)
            den = pv[:, HEAD_DIM:]
            if d == 1:
                rows = pl.ds(pl.multiple_of(n * blk, blk), blk)
            else:
                rows = pl.ds(pl.multiple_of(n * (blk * d), blk) + r, blk, stride=d)
            o_scr[g, rows, :] = pv[:, :HEAD_DIM] / den
            lse_scr[g, rows, :] = m + jnp.log2(den)

        def run_blocks(tasks, scores=scores, finish=finish):
            nxt = scores(*tasks[0])
            for idx, (r, n) in enumerate(tasks):
                cur = nxt
                if idx + 1 < len(tasks):
                    nxt = scores(*tasks[idx + 1])
                finish(r, n, cur)

        n_per_iter = max(1, min(blocks_per_iter // d, n_blk))

        def body(it, carry, run_blocks=run_blocks, d=d, n_per_iter=n_per_iter):
            run_blocks([(r, it * n_per_iter + u) for u in range(n_per_iter) for r in range(d)])
            return carry

        lax.fori_loop(0, n_blk // n_per_iter, body, 0)

    def combine(c, carry):
        rows = pl.ds(pl.multiple_of(c * combine_rows, combine_rows), combine_rows)
        l1, l2, l3 = lse_scr[0, rows, :], lse_scr[1, rows, :], lse_scr[2, rows, :]
        lmax = jnp.maximum(jnp.maximum(l1, l2), l3)
        e1, e2, e3 = jnp.exp2(l1 - lmax), jnp.exp2(l2 - lmax), jnp.exp2(l3 - lmax)
        num = e1 * o_scr[0, rows, :] + e2 * o_scr[1, rows, :] + e3 * o_scr[2, rows, :]
        ob_ref[0, 0, rows, :] = num / (e1 + e2 + e3)
        return carry

    lax.fori_loop(0, seq_len // combine_rows, combine, 0)


def _dilated_call(qkv_heads, d4, d16, *, n_heads_b, q_off, k_off, v_off, blocks_per_iter=16):
    B, _, S, _ = qkv_heads.shape
    offs_nat = (q_off, k_off, v_off)
    offs_dil = (0, n_heads_b, 2 * n_heads_b)
    in_arrays = [qkv_heads] * 3 + [d4] * 3 + [d16] * 3
    in_specs = (
        [pl.BlockSpec((1, 1, S, HEAD_DIM), lambda b, h, off=off: (b, off + h, 0, 0))
         for off in offs_nat]
        + [pl.BlockSpec((1, 1, 4, S // 4, HEAD_DIM), lambda b, h, off=off: (b, off + h, 0, 0, 0))
           for off in offs_dil]
        + [pl.BlockSpec((1, 1, 16, S // 16, HEAD_DIM), lambda b, h, off=off: (b, off + h, 0, 0, 0))
           for off in offs_dil])
    n_br = len(DIL_PAIRS)
    return pl.pallas_call(
        functools.partial(_dilated_kernel, seq_len=S, blocks_per_iter=blocks_per_iter,
                          combine_rows=256),
        grid=(B, n_heads_b),
        in_specs=in_specs,
        out_specs=pl.BlockSpec((1, 1, S, HEAD_DIM), lambda b, h: (b, h, 0, 0)),
        out_shape=jax.ShapeDtypeStruct((B, n_heads_b, S, HEAD_DIM), F32),
        scratch_shapes=[pltpu.VMEM((n_br, S, HEAD_DIM), F32),
                        pltpu.VMEM((n_br, S, HEAD_DIM), F32),
                        pltpu.VMEM((2, DIL_BLOCK, 2 * DIL_BLOCK), F32)],
        compiler_params=pltpu.CompilerParams(
            dimension_semantics=("parallel", "parallel"),
            vmem_limit_bytes=V7X_VMEM_LIMIT_BYTES),
        name="dilated_attn",
    )(*in_arrays)


def _attnout_kernel(x_ref, oa_ref, ob_ref, ga_ref, gb_ref, wo_ref, pg_ref, out_ref, mix_ref):
    col = 0
    for o_ref, g_ref in ((oa_ref, ga_ref), (ob_ref, gb_ref)):
        n_heads = o_ref.shape[1]
        ssq = None
        for h in range(n_heads):
            t = o_ref[0, h]
            part = jnp.sum(t * t, axis=-1, keepdims=True)
            ssq = part if ssq is None else ssq + part
        inv = lax.rsqrt(ssq / (n_heads * HEAD_DIM) + RMS_EPS)
        for h in range(n_heads):
            gs = slice(h * HEAD_DIM, (h + 1) * HEAD_DIM)
            mix_ref[:, col:col + HEAD_DIM] = (o_ref[0, h] * inv * g_ref[:, gs]).astype(BF16)
            col += HEAD_DIM
    y = jnp.dot(mix_ref[...], wo_ref[...], preferred_element_type=F32)
    out_ref[0] = x_ref[0] + y * _rms_scale(y) * pg_ref[...]


def _attnout_call(x, oa, ob, ga, gb, wo_bf16, pg, *, tm=256):
    B, S, D = x.shape
    head_spec_a = pl.BlockSpec((1, oa.shape[1], tm, HEAD_DIM), lambda b, i: (b, 0, i, 0))
    head_spec_b = pl.BlockSpec((1, ob.shape[1], tm, HEAD_DIM), lambda b, i: (b, 0, i, 0))
    return pl.pallas_call(
        _attnout_kernel,
        grid=(B, S // tm),
        in_specs=[
            pl.BlockSpec((1, tm, D), lambda b, i: (b, i, 0)),
            head_spec_a, head_spec_b,
            pl.BlockSpec((1, ga.shape[1]), lambda b, i: (0, 0)),
            pl.BlockSpec((1, gb.shape[1]), lambda b, i: (0, 0)),
            pl.BlockSpec(wo_bf16.shape, lambda b, i: (0, 0)),
            pl.BlockSpec((1, D), lambda b, i: (0, 0)),
        ],
        out_specs=pl.BlockSpec((1, tm, D), lambda b, i: (b, i, 0)),
        out_shape=jax.ShapeDtypeStruct((B, S, D), F32),
        scratch_shapes=[pltpu.VMEM((tm, wo_bf16.shape[0]), BF16)],
        compiler_params=pltpu.CompilerParams(
            dimension_semantics=("parallel", "parallel"),
            vmem_limit_bytes=V7X_VMEM_LIMIT_BYTES),
        name="attn_out",
    )(x, oa, ob, ga, gb, wo_bf16, pg)


def _ffn_kernel(x_ref, xh_ref, g_ref, wg_ref, wu_ref, cw_ref, cb_ref, wd_ref, pg_ref,
                out_ref, hn_ref, acc_ref, *, halo):
    i = pl.program_id(1)
    f = pl.program_id(2)
    tm = x_ref.shape[1]

    @pl.when(f == 0)
    def _():
        x = x_ref[0]
        hn_ref[halo:, :] = (x * _rms_scale(x) * g_ref[...]).astype(BF16)
        xh = xh_ref[0]
        hh = xh * _rms_scale(xh) * g_ref[...]
        hn_ref[0:halo, :] = jnp.where(i > 0, hh, 0.0).astype(BF16)
        acc_ref[...] = jnp.zeros_like(acc_ref)

    gate_ext = jnp.dot(hn_ref[...], wg_ref[...], preferred_element_type=F32)
    up = jnp.dot(hn_ref[halo:, :], wu_ref[...], preferred_element_type=F32)
    cw = cw_ref[...]
    gc = cb_ref[...] + gate_ext[halo - 2:halo - 2 + tm] * cw[0:1, :]
    gc = gc + gate_ext[halo - 1:halo - 1 + tm] * cw[1:2, :]
    gc = gc + gate_ext[halo:halo + tm] * cw[2:3, :]
    gelu = 0.5 * gc * (1.0 + jnp.tanh(0.7978845608028654 * (gc + 0.044715 * (gc * gc * gc))))
    act = (gelu * up).astype(BF16)
    acc_ref[...] += jnp.dot(act, wd_ref[...], preferred_element_type=F32)

    @pl.when(f == pl.num_programs(2) - 1)
    def _():
        y = acc_ref[...]
        out_ref[0] = x_ref[0] + y * _rms_scale(y) * pg_ref[...]


def _ffn_call(x, g, wg_bf16, wu_bf16, conv_w, conv_b, wd_bf16, pg, *, tm=512, tf=1024):
    B, S, D = x.shape
    d_ff = wg_bf16.shape[1]
    halo = BF16_SUBLANE_TILE
    assert CONV_WIDTH - 1 <= halo and tm % halo == 0
    halo_blocks_per_tile = tm // halo
    kern = functools.partial(_ffn_kernel, halo=halo)
    return pl.pallas_call(
        kern,
        grid=(B, S // tm, d_ff // tf),
        in_specs=[
            pl.BlockSpec((1, tm, D), lambda b, i, f: (b, i, 0)),
            pl.BlockSpec((1, halo, D),
                         lambda b, i, f: (b, jnp.maximum(i * halo_blocks_per_tile - 1, 0), 0)),
            pl.BlockSpec((1, D), lambda b, i, f: (0, 0)),
            pl.BlockSpec((D, tf), lambda b, i, f: (0, f)),
            pl.BlockSpec((D, tf), lambda b, i, f: (0, f)),
            pl.BlockSpec((CONV_WIDTH, tf), lambda b, i, f: (0, f)),
            pl.BlockSpec((1, tf), lambda b, i, f: (0, f)),
            pl.BlockSpec((tf, D), lambda b, i, f: (f, 0)),
            pl.BlockSpec((1, D), lambda b, i, f: (0, 0)),
        ],
        out_specs=pl.BlockSpec((1, tm, D), lambda b, i, f: (b, i, 0)),
        out_shape=jax.ShapeDtypeStruct((B, S, D), F32),
        scratch_shapes=[pltpu.VMEM((halo + tm, D), BF16), pltpu.VMEM((tm, D), F32)],
        compiler_params=pltpu.CompilerParams(
            dimension_semantics=("parallel", "parallel", "arbitrary"),
            vmem_limit_bytes=V7X_VMEM_LIMIT_BYTES),
        name="conv_glu_ffn",
    )(x, x, g, wg_bf16, wu_bf16, conv_w, conv_b, wd_bf16, pg)


def _rope_tables(positions):
    inv_freq = ROPE_THETA ** (-jnp.arange(ROT_HALF, dtype=F32) / ROT_HALF)
    ang = positions.astype(F32)[..., None] * inv_freq
    cos, sin = jnp.cos(ang), jnp.sin(ang)
    ones = jnp.ones(ang.shape[:-1] + (HEAD_DIM - ROT_DIM,), F32)
    zeros_tail = jnp.zeros(ang.shape[:-1] + (HEAD_DIM - ROT_HALF,), F32)
    zeros_head = jnp.zeros_like(sin)
    cos_t = jnp.concatenate([cos, cos, ones], axis=-1)
    sa_t = jnp.concatenate([-sin, zeros_tail], axis=-1)
    sb_t = jnp.concatenate([zeros_head, sin, zeros_tail[..., :HEAD_DIM - ROT_DIM]], axis=-1)
    return cos_t, sa_t, sb_t


def kernel(x, positions, attn_pre_g, w_qkv, moba_out_g, dil_out_g, w_o, attn_post_g, ffn_pre_g,
           w_gate, w_up, conv_w, conv_b, w_down, ffn_post_g):
    depth = w_qkv.shape[0]
    n_heads_a = moba_out_g.shape[1] // HEAD_DIM
    n_heads_b = dil_out_g.shape[1] // HEAD_DIM
    cos_t, sa_t, sb_t = _rope_tables(positions)
    for l in range(depth):
        qkv_heads, vt, d4, d16 = _qkv_call(x, attn_pre_g[l][None], w_qkv[l].astype(BF16),
                                           cos_t, sa_t, sb_t,
                                           n_heads_a=n_heads_a, n_heads_b=n_heads_b)
        oa = _moba_call(qkv_heads, vt, n_heads_a=n_heads_a)
        ob = _dilated_call(qkv_heads, d4, d16, n_heads_b=n_heads_b, q_off=3 * n_heads_a,
                           k_off=3 * n_heads_a + n_heads_b, v_off=3 * n_heads_a + 2 * n_heads_b)
        x = _attnout_call(x, oa, ob, moba_out_g[l][None], dil_out_g[l][None],
                          w_o[l].astype(BF16), attn_post_g[l][None])
        x = _ffn_call(x, ffn_pre_g[l][None], w_gate[l].astype(BF16), w_up[l].astype(BF16),
                      conv_w[l], conv_b[l][None], w_down[l].astype(BF16), ffn_post_g[l][None])
    return x
```

```python
import functools

import jax
import jax.numpy as jnp
from jax import lax
from jax.experimental import pallas as pl
from jax.experimental.pallas import tpu as pltpu

F32 = jnp.float32
BF16 = jnp.bfloat16

HEAD_DIM = 128
ROT_DIM = HEAD_DIM // 4
ROT_HALF = ROT_DIM // 2
ROPE_THETA = 500000.0
MOBA_BLOCK = 256
MOBA_TOPK = 3
DIL_PAIRS = ((128, 1), (512, 4), (2048, 16))
DIL_BLOCK = 128
CONV_WIDTH = 3
RMS_EPS = 1e-6
SCALE = HEAD_DIM ** -0.5
SCALE_LOG2E = SCALE * 1.4426950408889634
NEG = -1e30

V7X_VMEM_LIMIT_BYTES = 56 * 1024 * 1024
BF16_SUBLANE_TILE = 16
V7X_MXU_WIDTH = 256
MOBA_VT_ROWS = HEAD_DIM + BF16_SUBLANE_TILE

NT_DIMS = (((1,), (1,)), ((), ()))


def _rms_scale(x):
    return lax.rsqrt(jnp.mean(x * x, axis=-1, keepdims=True) + RMS_EPS)


def _qkv_kernel(x_ref, g_ref, w_ref, cos_ref, sa_ref, sb_ref,
                nat_ref, vt_ref, d4_ref, d16_ref, hn_ref, slab_ref, *, heads_per_tile):
    j = pl.program_id(2)
    tm = hn_ref.shape[0]
    heads_per_dot = V7X_MXU_WIDTH // HEAD_DIM
    dot_width = heads_per_dot * HEAD_DIM

    @pl.when(j == 0)
    def _():
        x = x_ref[0]
        hn_ref[...] = (x * _rms_scale(x) * g_ref[...]).astype(BF16)

    def rope(t):
        return (t * cos_ref[0]
                + pltpu.roll(t, HEAD_DIM - ROT_HALF, 1) * sa_ref[0]
                + pltpu.roll(t, ROT_HALF, 1) * sb_ref[0])

    def run(with_rope, with_vt, with_dilated):
        for c in range(heads_per_tile // heads_per_dot):
            acc = jnp.dot(hn_ref[...], w_ref[:, c * dot_width:(c + 1) * dot_width],
                          preferred_element_type=F32)
            for hh in range(heads_per_dot):
                h = c * heads_per_dot + hh
                t = acc[:, hh * HEAD_DIM:(hh + 1) * HEAD_DIM]
                if with_rope:
                    t = rope(t)
                nat_ref[0, h] = t.astype(BF16)
                if with_vt:
                    for blk in range(tm // MOBA_BLOCK):
                        vt_ref[0, h, blk, 0:HEAD_DIM, :] = (
                            t[blk * MOBA_BLOCK:(blk + 1) * MOBA_BLOCK].T.astype(BF16))
                        vt_ref[0, h, blk, HEAD_DIM:, :] = jnp.ones(
                            (MOBA_VT_ROWS - HEAD_DIM, MOBA_BLOCK), BF16)
                if with_dilated:
                    slab_ref[h] = t
                    for ref, d in ((d4_ref, 4), (d16_ref, 16)):
                        for res in range(d):
                            ref[0, h, res] = slab_ref[h, pl.ds(res, tm // d, stride=d), :].astype(BF16)

    pl.when(j < 2)(lambda: run(True, False, False))
    pl.when(j == 2)(lambda: run(False, True, False))
    pl.when(jnp.logical_or(j == 3, j == 4))(lambda: run(True, False, True))
    pl.when(j == 5)(lambda: run(False, False, True))


def _qkv_call(x, g, w_bf16, cos_t, sa_t, sb_t, *, n_heads_a, n_heads_b, tm=1024):
    B, S, D = x.shape
    N = w_bf16.shape[1]
    assert n_heads_a == n_heads_b and N == 3 * (n_heads_a + n_heads_b) * HEAD_DIM
    hpt = n_heads_a
    tn = hpt * HEAD_DIM
    first_b_tile = 3
    kern = functools.partial(_qkv_kernel, heads_per_tile=hpt)

    def dil_index(b, i, j):
        return (b, jnp.maximum(j - first_b_tile, 0), 0, i, 0)

    return pl.pallas_call(
        kern,
        grid=(B, S // tm, N // tn),
        in_specs=[
            pl.BlockSpec((1, tm, D), lambda b, i, j: (b, i, 0)),
            pl.BlockSpec((1, D), lambda b, i, j: (0, 0)),
            pl.BlockSpec((D, tn), lambda b, i, j: (0, j)),
            pl.BlockSpec((1, tm, HEAD_DIM), lambda b, i, j: (b, i, 0)),
            pl.BlockSpec((1, tm, HEAD_DIM), lambda b, i, j: (b, i, 0)),
            pl.BlockSpec((1, tm, HEAD_DIM), lambda b, i, j: (b, i, 0)),
        ],
        out_specs=[
            pl.BlockSpec((1, hpt, tm, HEAD_DIM), lambda b, i, j: (b, j, i, 0)),
            pl.BlockSpec((1, hpt, tm // MOBA_BLOCK, MOBA_VT_ROWS, MOBA_BLOCK),
                         lambda b, i, j: (b, 0, i, 0, 0)),
            pl.BlockSpec((1, hpt, 4, tm // 4, HEAD_DIM), dil_index),
            pl.BlockSpec((1, hpt, 16, tm // 16, HEAD_DIM), dil_index),
        ],
        out_shape=[
            jax.ShapeDtypeStruct((B, N // HEAD_DIM, S, HEAD_DIM), BF16),
            jax.ShapeDtypeStruct((B, n_heads_a, S // MOBA_BLOCK, MOBA_VT_ROWS, MOBA_BLOCK), BF16),
            jax.ShapeDtypeStruct((B, 3 * n_heads_b, 4, S // 4, HEAD_DIM), BF16),
            jax.ShapeDtypeStruct((B, 3 * n_heads_b, 16, S // 16, HEAD_DIM), BF16),
        ],
        scratch_shapes=[pltpu.VMEM((tm, D), BF16), pltpu.VMEM((hpt, tm, HEAD_DIM), F32)],
        compiler_params=pltpu.CompilerParams(
            dimension_semantics=("parallel", "parallel", "arbitrary"),
            vmem_limit_bytes=V7X_VMEM_LIMIT_BYTES),
        name="qkv_rope",
    )(x, g, w_bf16, cos_t, sa_t, sb_t)


def _moba_kernel(q_ref, k_ref, vt_ref, o_ref, kmean_ref, bias_ref, acc_ref, *, n_blocks, heads,
                 lookahead=8, group=2):
    assert n_blocks % group == 0
    i = pl.program_id(2)
    blk_sz = MOBA_BLOCK

    @pl.when(i == 0)
    def _():
        for h in range(heads):
            for blk in range(n_blocks):
                kb = k_ref[0, h, blk * blk_sz:(blk + 1) * blk_sz, :].astype(F32)
                kmean_ref[h, blk:blk + 1, :] = jnp.mean(kb, axis=0, keepdims=True)

    def select_blocks(h, q):
        km = kmean_ref[h]
        km_hi = km.astype(BF16)
        km_lo = (km - km_hi.astype(F32)).astype(BF16)
        gate = (lax.dot_general(km_hi, q, NT_DIMS, preferred_element_type=F32)
                + lax.dot_general(km_lo, q, NT_DIMS, preferred_element_type=F32))
        blk_id = lax.broadcasted_iota(jnp.int32, gate.shape, 0).astype(F32)
        neg_inf = jnp.float32(-jnp.inf)
        g = jnp.where(blk_id < i.astype(F32), gate, neg_inf)
        sel = jnp.zeros(gate.shape, dtype=jnp.bool_)
        for _ in range(MOBA_TOPK):
            m = jnp.max(g, axis=0, keepdims=True)
            first = jnp.min(jnp.where(g == m, blk_id, float(n_blocks)), axis=0, keepdims=True)
            pick = jnp.logical_and(blk_id == first, m > neg_inf)
            sel = jnp.logical_or(sel, pick)
            g = jnp.where(pick, neg_inf, g)
        bias_ref[h] = jnp.where(sel, 0.0, NEG).astype(F32)

    def scores(h, first_blk, n_blk):
        rows = n_blk * blk_sz
        kb = k_ref[0, h, pl.ds(pl.multiple_of(first_blk * blk_sz, blk_sz), rows), :]
        return lax.dot_general(kb, q_ref[0, h], NT_DIMS, preferred_element_type=F32)

    def pipelined(stage_a, stage_b):
        ahead = [stage_a(h) for h in range(min(lookahead, heads))]
        outs = []
        for h in range(heads):
            if h + lookahead < heads:
                ahead.append(stage_a(h + lookahead))
            outs.append(stage_b(h, ahead[h]))
        return outs

    def own_a(h):
        select_blocks(h, q_ref[0, h])
        return scores(h, i, 1)

    def own_b(h, s):
        key_pos = lax.broadcasted_iota(jnp.int32, s.shape, 0)
        q_pos = lax.broadcasted_iota(jnp.int32, s.shape, 1)
        t = jnp.where(key_pos <= q_pos, s * SCALE_LOG2E, NEG)
        m0 = jnp.max(t, axis=0, keepdims=True)
        p = jnp.exp2(t - m0)
        acc_ref[h] = jnp.dot(vt_ref[0, h, i], p.astype(BF16), preferred_element_type=F32)
        return m0

    init = pipelined(own_a, own_b)

    def body(c, carry):
        first = c * group

        def past_b(h, s):
            m_prev = carry[h]
            chunks = [s[g * blk_sz:(g + 1) * blk_sz] for g in range(group)]
            brows = [bias_ref[h, pl.ds(first + g, 1), :] for g in range(group)]
            m_new = m_prev
            for sg, brow in zip(chunks, brows):
                m_new = jnp.maximum(m_new, jnp.max(sg, axis=0, keepdims=True) * SCALE_LOG2E + brow)
            acc_new = jnp.exp2(m_prev - m_new) * acc_ref[h]
            for g, (sg, brow) in enumerate(zip(chunks, brows)):
                pg = jnp.exp2(sg * SCALE_LOG2E - (m_new - brow))
                acc_new = acc_new + jnp.dot(vt_ref[0, h, first + g], pg.astype(BF16),
                                            preferred_element_type=F32)
            acc_ref[h] = acc_new
            return m_new

        return tuple(pipelined(lambda h: scores(h, first, group), past_b))

    lax.fori_loop(0, (i + group - 1) // group, body, tuple(init))
    for h in range(heads):
        acc_fin = acc_ref[h]
        o_ref[0, h] = (acc_fin[:HEAD_DIM] / acc_fin[HEAD_DIM:HEAD_DIM + 1]).T


def _moba_call(qkv_heads, vt, *, n_heads_a, heads_per_step=8):
    B, _, S, _ = qkv_heads.shape
    n_blocks = S // MOBA_BLOCK
    tq = MOBA_BLOCK
    hps = heads_per_step
    assert n_heads_a % hps == 0
    k_first = n_heads_a // hps
    kern = functools.partial(_moba_kernel, n_blocks=n_blocks, heads=hps)
    return pl.pallas_call(
        kern,
        grid=(B, n_heads_a // hps, S // tq),
        in_specs=[
            pl.BlockSpec((1, hps, tq, HEAD_DIM), lambda b, h, i: (b, h, i, 0)),
            pl.BlockSpec((1, hps, S, HEAD_DIM), lambda b, h, i: (b, k_first + h, 0, 0)),
            pl.BlockSpec((1, hps, n_blocks, MOBA_VT_ROWS, MOBA_BLOCK),
                         lambda b, h, i: (b, h, 0, 0, 0)),
        ],
        out_specs=pl.BlockSpec((1, hps, tq, HEAD_DIM), lambda b, h, i: (b, h, i, 0)),
        out_shape=jax.ShapeDtypeStruct((B, n_heads_a, S, HEAD_DIM), F32),
        scratch_shapes=[pltpu.VMEM((hps, n_blocks, HEAD_DIM), F32),
                        pltpu.VMEM((hps, n_blocks, tq), F32),
                        pltpu.VMEM((hps, MOBA_VT_ROWS, tq), F32)],
        compiler_params=pltpu.CompilerParams(
            dimension_semantics=("parallel", "parallel", "arbitrary"),
            vmem_limit_bytes=V7X_VMEM_LIMIT_BYTES),
        name="moba_attn",
    )(qkv_heads, qkv_heads, vt)


def _dilated_kernel(q1_ref, k1_ref, v1_ref, q4_ref, k4_ref, v4_ref, q16_ref, k16_ref, v16_ref,
                    ob_ref, o_scr, lse_scr, bias_scr, *, seq_len, blocks_per_iter, combine_rows):
    blk = DIL_BLOCK
    branch_refs = ((q1_ref, k1_ref, v1_ref), (q4_ref, k4_ref, v4_ref), (q16_ref, k16_ref, v16_ref))

    qi = lax.broadcasted_iota(jnp.int32, (blk, 2 * blk), 0)
    ki = lax.broadcasted_iota(jnp.int32, (blk, 2 * blk), 1)
    dist = qi + blk - ki
    bias_scr[0] = jnp.where(jnp.logical_and(dist >= 0, dist <= blk), 0.0, NEG).astype(F32)
    bias_scr[1] = jnp.where(ki <= qi, 0.0, NEG).astype(F32)

    for g, (window, d) in enumerate(DIL_PAIRS):
        assert window // d == DIL_BLOCK
        q_ref, k_ref, v_ref = branch_refs[g]
        n_blk = seq_len // d // blk

        def rows_of(ref, r, start, size, d=d):
            if d == 1:
                return ref[0, 0, pl.ds(start, size), :]
            return ref[0, 0, r, pl.ds(start, size), :]

        def key_start(n):
            return pl.multiple_of(jnp.maximum(n - 1, 0) * blk, blk)

        def scores(r, n, q_ref=q_ref, k_ref=k_ref, rows_of=rows_of):
            qb = rows_of(q_ref, r, pl.multiple_of(n * blk, blk), blk)
            kb = rows_of(k_ref, r, key_start(n), 2 * blk)
            return lax.dot_general(qb, kb, NT_DIMS, preferred_element_type=F32)

        def finish(r, n, s, g=g, d=d, v_ref=v_ref, rows_of=rows_of):
            t = s * SCALE_LOG2E + bias_scr[jnp.where(n == 0, 1, 0)]
            m = jnp.max(t, axis=-1, keepdims=True)
            p = jnp.exp2(t - m)
            vb = rows_of(v_ref, r, key_start(n), 2 * blk)
            v_ones = jnp.concatenate([vb, jnp.ones_like(vb)], axis=1)
            pv = jnp.dot(p.astype(BF16), v_ones, preferred_element_type=F32)
            den = pv[:, HEAD_DIM:]
            if d == 1:
                rows = pl.ds(pl.multiple_of(n * blk, blk), blk)
            else:
                rows = pl.ds(pl.multiple_of(n * (blk * d), blk) + r, blk, stride=d)
            o_scr[g, rows, :] = pv[:, :HEAD_DIM] / den
            lse_scr[g, rows, :] = m + jnp.log2(den)

        def run_blocks(tasks, scores=scores, finish=finish):
            nxt = scores(*tasks[0])
            for idx, (r, n) in enumerate(tasks):
                cur = nxt
                if idx + 1 < len(tasks):
                    nxt = scores(*tasks[idx + 1])
                finish(r, n, cur)

        n_per_iter = max(1, min(blocks_per_iter // d, n_blk))

        def body(it, carry, run_blocks=run_blocks, d=d, n_per_iter=n_per_iter):
            run_blocks([(r, it * n_per_iter + u) for u in range(n_per_iter) for r in range(d)])
            return carry

        lax.fori_loop(0, n_blk // n_per_iter, body, 0)

    def combine(c, carry):
        rows = pl.ds(pl.multiple_of(c * combine_rows, combine_rows), combine_rows)
        l1, l2, l3 = lse_scr[0, rows, :], lse_scr[1, rows, :], lse_scr[2, rows, :]
        lmax = jnp.maximum(jnp.maximum(l1, l2), l3)
        e1, e2, e3 = jnp.exp2(l1 - lmax), jnp.exp2(l2 - lmax), jnp.exp2(l3 - lmax)
        num = e1 * o_scr[0, rows, :] + e2 * o_scr[1, rows, :] + e3 * o_scr[2, rows, :]
        ob_ref[0, 0, rows, :] = num / (e1 + e2 + e3)
        return carry

    lax.fori_loop(0, seq_len // combine_rows, combine, 0)


def _dilated_call(qkv_heads, d4, d16, *, n_heads_b, q_off, k_off, v_off, blocks_per_iter=16):
    B, _, S, _ = qkv_heads.shape
    offs_nat = (q_off, k_off, v_off)
    offs_dil = (0, n_heads_b, 2 * n_heads_b)
    in_arrays = [qkv_heads] * 3 + [d4] * 3 + [d16] * 3
    in_specs = (
        [pl.BlockSpec((1, 1, S, HEAD_DIM), lambda b, h, off=off: (b, off + h, 0, 0))
         for off in offs_nat]
        + [pl.BlockSpec((1, 1, 4, S // 4, HEAD_DIM), lambda b, h, off=off: (b, off + h, 0, 0, 0))
           for off in offs_dil]
        + [pl.BlockSpec((1, 1, 16, S // 16, HEAD_DIM), lambda b, h, off=off: (b, off + h, 0, 0, 0))
           for off in offs_dil])
    n_br = len(DIL_PAIRS)
    return pl.pallas_call(
        functools.partial(_dilated_kernel, seq_len=S, blocks_per_iter=blocks_per_iter,
                          combine_rows=256),
        grid=(B, n_heads_b),
        in_specs=in_specs,
        out_specs=pl.BlockSpec((1, 1, S, HEAD_DIM), lambda b, h: (b, h, 0, 0)),
        out_shape=jax.ShapeDtypeStruct((B, n_heads_b, S, HEAD_DIM), F32),
        scratch_shapes=[pltpu.VMEM((n_br, S, HEAD_DIM), F32),
                        pltpu.VMEM((n_br, S, HEAD_DIM), F32),
                        pltpu.VMEM((2, DIL_BLOCK, 2 * DIL_BLOCK), F32)],
        compiler_params=pltpu.CompilerParams(
            dimension_semantics=("parallel", "parallel"),
            vmem_limit_bytes=V7X_VMEM_LIMIT_BYTES),
        name="dilated_attn",
    )(*in_arrays)


def _attnout_kernel(x_ref, oa_ref, ob_ref, ga_ref, gb_ref, wo_ref, pg_ref, fg_ref,
                    out_ref, hn_ref, mix_ref, *, row_chunks):
    tm = x_ref.shape[1]
    rows_per_chunk = tm // row_chunks
    for c in range(row_chunks):
        rows = slice(c * rows_per_chunk, (c + 1) * rows_per_chunk)
        col = 0
        for o_ref, g_ref in ((oa_ref, ga_ref), (ob_ref, gb_ref)):
            n_heads = o_ref.shape[1]
            ssq = None
            for h in range(n_heads):
                t = o_ref[0, h, rows, :]
                part = jnp.sum(t * t, axis=-1, keepdims=True)
                ssq = part if ssq is None else ssq + part
            inv = lax.rsqrt(ssq / (n_heads * HEAD_DIM) + RMS_EPS)
            for h in range(n_heads):
                gs = slice(h * HEAD_DIM, (h + 1) * HEAD_DIM)
                mix_ref[rows, col:col + HEAD_DIM] = (
                    o_ref[0, h, rows, :] * inv * g_ref[:, gs]).astype(BF16)
                col += HEAD_DIM
        y = jnp.dot(mix_ref[rows, :], wo_ref[...], preferred_element_type=F32)
        x_new = x_ref[0, rows, :] + y * _rms_scale(y) * pg_ref[...]
        out_ref[0, rows, :] = x_new
        hn_ref[0, rows, :] = (x_new * _rms_scale(x_new) * fg_ref[...]).astype(BF16)


def _attnout_call(x, oa, ob, ga, gb, wo_bf16, pg, ffn_g, *, tm=512, row_chunks=2):
    B, S, D = x.shape
    head_spec_a = pl.BlockSpec((1, oa.shape[1], tm, HEAD_DIM), lambda b, i: (b, 0, i, 0))
    head_spec_b = pl.BlockSpec((1, ob.shape[1], tm, HEAD_DIM), lambda b, i: (b, 0, i, 0))
    row_spec = pl.BlockSpec((1, tm, D), lambda b, i: (b, i, 0))
    return pl.pallas_call(
        functools.partial(_attnout_kernel, row_chunks=row_chunks),
        grid=(B, S // tm),
        in_specs=[
            row_spec,
            head_spec_a, head_spec_b,
            pl.BlockSpec((1, ga.shape[1]), lambda b, i: (0, 0)),
            pl.BlockSpec((1, gb.shape[1]), lambda b, i: (0, 0)),
            pl.BlockSpec(wo_bf16.shape, lambda b, i: (0, 0)),
            pl.BlockSpec((1, D), lambda b, i: (0, 0)),
            pl.BlockSpec((1, D), lambda b, i: (0, 0)),
        ],
        out_specs=[row_spec, row_spec],
        out_shape=[jax.ShapeDtypeStruct((B, S, D), F32), jax.ShapeDtypeStruct((B, S, D), BF16)],
        scratch_shapes=[pltpu.VMEM((tm, wo_bf16.shape[0]), BF16)],
        compiler_params=pltpu.CompilerParams(
            dimension_semantics=("parallel", "parallel"),
            vmem_limit_bytes=V7X_VMEM_LIMIT_BYTES),
        name="attn_out",
    )(x, oa, ob, ga, gb, wo_bf16, pg, ffn_g)


def _ffn_kernel(x_ref, hn_ref, wg_ref, wu_ref, cw_ref, cb_ref, wd_ref, pg_ref,
                out_ref, acc_ref, tail_ref, act_ref):
    i = pl.program_id(1)
    f = pl.program_id(2)
    tm = x_ref.shape[1]
    halo = tail_ref.shape[1]

    @pl.when(i == 0)
    def _():
        tail_ref[f] = jnp.zeros(tail_ref.shape[1:], F32)

    @pl.when(f == 0)
    def _():
        acc_ref[...] = jnp.zeros_like(acc_ref)

    gate = jnp.dot(hn_ref[0], wg_ref[...], preferred_element_type=F32)
    up = jnp.dot(hn_ref[0], wu_ref[...], preferred_element_type=F32)
    cw = cw_ref[...]
    cb = cb_ref[...]

    def gated(g_m2, g_m1, g_0, u):
        gc = cb + g_m2 * cw[0:1, :] + g_m1 * cw[1:2, :] + g_0 * cw[2:3, :]
        gelu = 0.5 * gc * (1.0 + jnp.tanh(0.7978845608028654 * (gc + 0.044715 * (gc * gc * gc))))
        return (gelu * u).astype(BF16)

    act_ref[...] = gated(pltpu.roll(gate, 2, 0), pltpu.roll(gate, 1, 0), gate, up)
    head = BF16_SUBLANE_TILE
    ext = jnp.concatenate([tail_ref[f], gate[0:head]], axis=0)
    act_ref[0:head, :] = gated(ext[halo - 2:halo - 2 + head], ext[halo - 1:halo - 1 + head],
                               gate[0:head], up[0:head])
    tail_ref[f] = gate[tm - halo:, :]
    acc_ref[...] += jnp.dot(act_ref[...], wd_ref[...], preferred_element_type=F32)

    @pl.when(f == pl.num_programs(2) - 1)
    def _():
        y = acc_ref[...]
        out_ref[0] = x_ref[0] + y * _rms_scale(y) * pg_ref[...]


def _ffn_call(x, hn, wg_bf16, wu_bf16, conv_w, conv_b, wd_bf16, pg, *, tm=512, tf=1024):
    B, S, D = x.shape
    d_ff = wg_bf16.shape[1]
    halo = 8
    assert CONV_WIDTH - 1 <= halo
    return pl.pallas_call(
        _ffn_kernel,
        grid=(B, S // tm, d_ff // tf),
        in_specs=[
            pl.BlockSpec((1, tm, D), lambda b, i, f: (b, i, 0)),
            pl.BlockSpec((1, tm, D), lambda b, i, f: (b, i, 0)),
            pl.BlockSpec((D, tf), lambda b, i, f: (0, f)),
            pl.BlockSpec((D, tf), lambda b, i, f: (0, f)),
            pl.BlockSpec((CONV_WIDTH, tf), lambda b, i, f: (0, f)),
            pl.BlockSpec((1, tf), lambda b, i, f: (0, f)),
            pl.BlockSpec((tf, D), lambda b, i, f: (f, 0)),
            pl.BlockSpec((1, D), lambda b, i, f: (0, 0)),
        ],
        out_specs=pl.BlockSpec((1, tm, D), lambda b, i, f: (b, i, 0)),
        out_shape=jax.ShapeDtypeStruct((B, S, D), F32),
        scratch_shapes=[pltpu.VMEM((tm, D), F32), pltpu.VMEM((d_ff // tf, halo, tf), F32),
                        pltpu.VMEM((tm, tf), BF16)],
        compiler_params=pltpu.CompilerParams(
            dimension_semantics=("parallel", "arbitrary", "arbitrary"),
            vmem_limit_bytes=V7X_VMEM_LIMIT_BYTES + 2 * 1024 * 1024),
        name="conv_glu_ffn",
    )(x, hn, wg_bf16, wu_bf16, conv_w, conv_b, wd_bf16, pg)


def _rope_tables(positions):
    inv_freq = ROPE_THETA ** (-jnp.arange(ROT_HALF, dtype=F32) / ROT_HALF)
    ang = positions.astype(F32)[..., None] * inv_freq
    cos, sin = jnp.cos(ang), jnp.sin(ang)
    ones = jnp.ones(ang.shape[:-1] + (HEAD_DIM - ROT_DIM,), F32)
    zeros_tail = jnp.zeros(ang.shape[:-1] + (HEAD_DIM - ROT_HALF,), F32)
    zeros_head = jnp.zeros_like(sin)
    cos_t = jnp.concatenate([cos, cos, ones], axis=-1)
    sa_t = jnp.concatenate([-sin, zeros_tail], axis=-1)
    sb_t = jnp.concatenate([zeros_head, sin, zeros_tail[..., :HEAD_DIM - ROT_DIM]], axis=-1)
    return cos_t, sa_t, sb_t


def kernel(x, positions, attn_pre_g, w_qkv, moba_out_g, dil_out_g, w_o, attn_post_g, ffn_pre_g,
           w_gate, w_up, conv_w, conv_b, w_down, ffn_post_g):
    depth = w_qkv.shape[0]
    n_heads_a = moba_out_g.shape[1] // HEAD_DIM
    n_heads_b = dil_out_g.shape[1] // HEAD_DIM
    cos_t, sa_t, sb_t = _rope_tables(positions)
    for l in range(depth):
        qkv_heads, vt, d4, d16 = _qkv_call(x, attn_pre_g[l][None], w_qkv[l].astype(BF16),
                                           cos_t, sa_t, sb_t,
                                           n_heads_a=n_heads_a, n_heads_b=n_heads_b)
        oa = _moba_call(qkv_heads, vt, n_heads_a=n_heads_a)
        ob = _dilated_call(qkv_heads, d4, d16, n_heads_b=n_heads_b, q_off=3 * n_heads_a,
                           k_off=3 * n_heads_a + n_heads_b, v_off=3 * n_heads_a + 2 * n_heads_b)
        x, hn = _attnout_call(x, oa, ob, moba_out_g[l][None], dil_out_g[l][None],
                              w_o[l].astype(BF16), attn_post_g[l][None], ffn_pre_g[l][None])
        x = _ffn_call(x, hn, w_gate[l].astype(BF16), w_up[l].astype(BF16),
                      conv_w[l], conv_b[l][None], w_down[l].astype(BF16), ffn_post_g[l][None])
    return x
```

```python
import functools

import jax
import jax.numpy as jnp
from jax import lax
from jax.experimental import pallas as pl
from jax.experimental.pallas import tpu as pltpu

F32 = jnp.float32
BF16 = jnp.bfloat16

HEAD_DIM = 128
ROT_DIM = HEAD_DIM // 4
ROT_HALF = ROT_DIM // 2
ROPE_THETA = 500000.0
MOBA_BLOCK = 256
MOBA_TOPK = 3
DIL_PAIRS = ((128, 1), (512, 4), (2048, 16))
DIL_BLOCK = 128
CONV_WIDTH = 3
RMS_EPS = 1e-6
SCALE = HEAD_DIM ** -0.5
SCALE_LOG2E = SCALE * 1.4426950408889634
NEG = -1e30

V7X_VMEM_LIMIT_BYTES = 56 * 1024 * 1024
BF16_SUBLANE_TILE = 16
V7X_MXU_WIDTH = 256
MOBA_VT_ROWS = HEAD_DIM + BF16_SUBLANE_TILE

NT_DIMS = (((1,), (1,)), ((), ()))


def _rms_scale(x):
    return lax.rsqrt(jnp.mean(x * x, axis=-1, keepdims=True) + RMS_EPS)


def _qkv_kernel(x_ref, g_ref, w_ref, cos_ref, sa_ref, sb_ref,
                nat_ref, vt_ref, d4_ref, d16_ref, hn_ref, slab_ref, *, heads_per_tile):
    j = pl.program_id(2)
    tm = hn_ref.shape[0]
    heads_per_dot = V7X_MXU_WIDTH // HEAD_DIM
    dot_width = heads_per_dot * HEAD_DIM

    @pl.when(j == 0)
    def _():
        x = x_ref[0]
        hn_ref[...] = (x * _rms_scale(x) * g_ref[...]).astype(BF16)

    def run(with_rope, with_vt, with_dilated):
        if with_rope:
            q_scale = jnp.where(jnp.logical_or(j == 0, j == 3), SCALE_LOG2E, 1.0).astype(F32)
            cos, sa, sb = cos_ref[0] * q_scale, sa_ref[0] * q_scale, sb_ref[0] * q_scale

            def rope(t):
                return (t * cos + pltpu.roll(t, HEAD_DIM - ROT_HALF, 1) * sa
                        + pltpu.roll(t, ROT_HALF, 1) * sb)

        for c in range(heads_per_tile // heads_per_dot):
            acc = jnp.dot(hn_ref[...], w_ref[:, c * dot_width:(c + 1) * dot_width],
                          preferred_element_type=F32)
            for hh in range(heads_per_dot):
                h = c * heads_per_dot + hh
                t = acc[:, hh * HEAD_DIM:(hh + 1) * HEAD_DIM]
                if with_rope:
                    t = rope(t)
                nat_ref[0, h] = t.astype(BF16)
                if with_vt:
                    for blk in range(tm // MOBA_BLOCK):
                        vt_ref[0, h, blk, 0:HEAD_DIM, :] = (
                            t[blk * MOBA_BLOCK:(blk + 1) * MOBA_BLOCK].T.astype(BF16))
                        vt_ref[0, h, blk, HEAD_DIM:, :] = jnp.ones(
                            (MOBA_VT_ROWS - HEAD_DIM, MOBA_BLOCK), BF16)
                if with_dilated:
                    slab_ref[h] = t
                    for ref, d in ((d4_ref, 4), (d16_ref, 16)):
                        for res in range(d):
                            ref[0, h, res] = slab_ref[h, pl.ds(res, tm // d, stride=d), :].astype(BF16)

    pl.when(j < 2)(lambda: run(True, False, False))
    pl.when(j == 2)(lambda: run(False, True, False))
    pl.when(jnp.logical_or(j == 3, j == 4))(lambda: run(True, False, True))
    pl.when(j == 5)(lambda: run(False, False, True))


def _qkv_call(x, g, w_bf16, cos_t, sa_t, sb_t, *, n_heads_a, n_heads_b, tm=1024):
    B, S, D = x.shape
    N = w_bf16.shape[1]
    assert n_heads_a == n_heads_b and N == 3 * (n_heads_a + n_heads_b) * HEAD_DIM
    hpt = n_heads_a
    tn = hpt * HEAD_DIM
    first_b_tile = 3
    kern = functools.partial(_qkv_kernel, heads_per_tile=hpt)

    def dil_index(b, i, j):
        return (b, jnp.maximum(j - first_b_tile, 0), 0, i, 0)

    return pl.pallas_call(
        kern,
        grid=(B, S // tm, N // tn),
        in_specs=[
            pl.BlockSpec((1, tm, D), lambda b, i, j: (b, i, 0)),
            pl.BlockSpec((1, D), lambda b, i, j: (0, 0)),
            pl.BlockSpec((D, tn), lambda b, i, j: (0, j)),
            pl.BlockSpec((1, tm, HEAD_DIM), lambda b, i, j: (b, i, 0)),
            pl.BlockSpec((1, tm, HEAD_DIM), lambda b, i, j: (b, i, 0)),
            pl.BlockSpec((1, tm, HEAD_DIM), lambda b, i, j: (b, i, 0)),
        ],
        out_specs=[
            pl.BlockSpec((1, hpt, tm, HEAD_DIM), lambda b, i, j: (b, j, i, 0)),
            pl.BlockSpec((1, hpt, tm // MOBA_BLOCK, MOBA_VT_ROWS, MOBA_BLOCK),
                         lambda b, i, j: (b, 0, i, 0, 0)),
            pl.BlockSpec((1, hpt, 4, tm // 4, HEAD_DIM), dil_index),
            pl.BlockSpec((1, hpt, 16, tm // 16, HEAD_DIM), dil_index),
        ],
        out_shape=[
            jax.ShapeDtypeStruct((B, N // HEAD_DIM, S, HEAD_DIM), BF16),
            jax.ShapeDtypeStruct((B, n_heads_a, S // MOBA_BLOCK, MOBA_VT_ROWS, MOBA_BLOCK), BF16),
            jax.ShapeDtypeStruct((B, 3 * n_heads_b, 4, S // 4, HEAD_DIM), BF16),
            jax.ShapeDtypeStruct((B, 3 * n_heads_b, 16, S // 16, HEAD_DIM), BF16),
        ],
        scratch_shapes=[pltpu.VMEM((tm, D), BF16), pltpu.VMEM((hpt, tm, HEAD_DIM), F32)],
        compiler_params=pltpu.CompilerParams(
            dimension_semantics=("parallel", "parallel", "arbitrary"),
            vmem_limit_bytes=V7X_VMEM_LIMIT_BYTES),
        name="qkv_rope",
    )(x, g, w_bf16, cos_t, sa_t, sb_t)


def _moba_kernel(q_ref, k_ref, vt_ref, o_ref, kmean_ref, bias_ref, acc_ref, *, n_blocks, heads,
                 lookahead=8, group=2):
    assert n_blocks % group == 0
    i = pl.program_id(2)
    blk_sz = MOBA_BLOCK

    @pl.when(i == 0)
    def _():
        for h in range(heads):
            for blk in range(n_blocks):
                kb = k_ref[0, h, blk * blk_sz:(blk + 1) * blk_sz, :].astype(F32)
                kmean_ref[h, blk:blk + 1, :] = jnp.mean(kb, axis=0, keepdims=True)

    def select_blocks(h, q):
        km = kmean_ref[h]
        km_hi = km.astype(BF16)
        km_lo = (km - km_hi.astype(F32)).astype(BF16)
        gate = (lax.dot_general(km_hi, q, NT_DIMS, preferred_element_type=F32)
                + lax.dot_general(km_lo, q, NT_DIMS, preferred_element_type=F32))
        blk_id = lax.broadcasted_iota(jnp.int32, gate.shape, 0).astype(F32)
        neg_inf = jnp.float32(-jnp.inf)
        g = jnp.where(blk_id < i.astype(F32), gate, neg_inf)
        sel = jnp.zeros(gate.shape, dtype=jnp.bool_)
        for _ in range(MOBA_TOPK):
            m = jnp.max(g, axis=0, keepdims=True)
            first = jnp.min(jnp.where(g == m, blk_id, float(n_blocks)), axis=0, keepdims=True)
            pick = jnp.logical_and(blk_id == first, m > neg_inf)
            sel = jnp.logical_or(sel, pick)
            g = jnp.where(pick, neg_inf, g)
        bias_ref[h] = jnp.where(sel, 0.0, NEG).astype(F32)

    def scores(h, first_blk, n_blk):
        rows = n_blk * blk_sz
        kb = k_ref[0, h, pl.ds(pl.multiple_of(first_blk * blk_sz, blk_sz), rows), :]
        return lax.dot_general(kb, q_ref[0, h], NT_DIMS, preferred_element_type=F32)

    def pipelined(stage_a, stage_b):
        ahead = [stage_a(h) for h in range(min(lookahead, heads))]
        outs = []
        for h in range(heads):
            if h + lookahead < heads:
                ahead.append(stage_a(h + lookahead))
            outs.append(stage_b(h, ahead[h]))
        return outs

    def own_a(h):
        select_blocks(h, q_ref[0, h])
        return scores(h, i, 1)

    def own_b(h, s):
        key_pos = lax.broadcasted_iota(jnp.int32, s.shape, 0)
        q_pos = lax.broadcasted_iota(jnp.int32, s.shape, 1)
        t = jnp.where(key_pos <= q_pos, s, NEG)
        m0 = jnp.max(t, axis=0, keepdims=True)
        p = jnp.exp2(t - m0)
        acc_ref[h] = jnp.dot(vt_ref[0, h, i], p.astype(BF16), preferred_element_type=F32)
        return m0

    init = pipelined(own_a, own_b)

    def body(c, carry):
        first = c * group

        def past_b(h, s):
            m_prev = carry[h]
            chunks = [s[g * blk_sz:(g + 1) * blk_sz] for g in range(group)]
            brows = [bias_ref[h, pl.ds(first + g, 1), :] for g in range(group)]
            m_new = m_prev
            for sg, brow in zip(chunks, brows):
                m_new = jnp.maximum(m_new, jnp.max(sg, axis=0, keepdims=True) + brow)
            acc_new = jnp.exp2(m_prev - m_new) * acc_ref[h]
            for g, (sg, brow) in enumerate(zip(chunks, brows)):
                pg = jnp.exp2(sg - (m_new - brow))
                acc_new = acc_new + jnp.dot(vt_ref[0, h, first + g], pg.astype(BF16),
                                            preferred_element_type=F32)
            acc_ref[h] = acc_new
            return m_new

        return tuple(pipelined(lambda h: scores(h, first, group), past_b))

    lax.fori_loop(0, (i + group - 1) // group, body, tuple(init))
    for h in range(heads):
        acc_fin = acc_ref[h]
        o_ref[0, h] = (acc_fin[:HEAD_DIM] / acc_fin[HEAD_DIM:HEAD_DIM + 1]).T


def _moba_call(qkv_heads, vt, *, n_heads_a, heads_per_step=8):
    B, _, S, _ = qkv_heads.shape
    n_blocks = S // MOBA_BLOCK
    tq = MOBA_BLOCK
    hps = heads_per_step
    assert n_heads_a % hps == 0
    k_first = n_heads_a // hps
    kern = functools.partial(_moba_kernel, n_blocks=n_blocks, heads=hps)
    return pl.pallas_call(
        kern,
        grid=(B, n_heads_a // hps, S // tq),
        in_specs=[
            pl.BlockSpec((1, hps, tq, HEAD_DIM), lambda b, h, i: (b, h, i, 0)),
            pl.BlockSpec((1, hps, S, HEAD_DIM), lambda b, h, i: (b, k_first + h, 0, 0)),
            pl.BlockSpec((1, hps, n_blocks, MOBA_VT_ROWS, MOBA_BLOCK),
                         lambda b, h, i: (b, h, 0, 0, 0)),
        ],
        out_specs=pl.BlockSpec((1, hps, tq, HEAD_DIM), lambda b, h, i: (b, h, i, 0)),
        out_shape=jax.ShapeDtypeStruct((B, n_heads_a, S, HEAD_DIM), F32),
        scratch_shapes=[pltpu.VMEM((hps, n_blocks, HEAD_DIM), F32),
                        pltpu.VMEM((hps, n_blocks, tq), F32),
                        pltpu.VMEM((hps, MOBA_VT_ROWS, tq), F32)],
        compiler_params=pltpu.CompilerParams(
            dimension_semantics=("parallel", "parallel", "arbitrary"),
            vmem_limit_bytes=V7X_VMEM_LIMIT_BYTES),
        name="moba_attn",
    )(qkv_heads, qkv_heads, vt)


def _dilated_kernel(q1_ref, k1_ref, v1_ref, q4_ref, k4_ref, v4_ref, q16_ref, k16_ref, v16_ref,
                    ob_ref, o_scr, lse_scr, bias_scr, *, seq_len, blocks_per_iter, combine_rows):
    blk = DIL_BLOCK
    branch_refs = ((q1_ref, k1_ref, v1_ref), (q4_ref, k4_ref, v4_ref), (q16_ref, k16_ref, v16_ref))

    qi = lax.broadcasted_iota(jnp.int32, (blk, 2 * blk), 0)
    ki = lax.broadcasted_iota(jnp.int32, (blk, 2 * blk), 1)
    dist = qi + blk - ki
    bias_scr[0] = jnp.where(jnp.logical_and(dist >= 0, dist <= blk), 0.0, NEG).astype(F32)
    bias_scr[1] = jnp.where(ki <= qi, 0.0, NEG).astype(F32)

    for g, (window, d) in enumerate(DIL_PAIRS):
        assert window // d == DIL_BLOCK
        q_ref, k_ref, v_ref = branch_refs[g]
        n_blk = seq_len // d // blk

        def rows_of(ref, r, start, size, d=d):
            if d == 1:
                return ref[0, 0, pl.ds(start, size), :]
            return ref[0, 0, r, pl.ds(start, size), :]

        def key_start(n):
            return pl.multiple_of(jnp.maximum(n - 1, 0) * blk, blk)

        def scores(r, n, q_ref=q_ref, k_ref=k_ref, rows_of=rows_of):
            qb = rows_of(q_ref, r, pl.multiple_of(n * blk, blk), blk)
            kb = rows_of(k_ref, r, key_start(n), 2 * blk)
            return lax.dot_general(qb, kb, NT_DIMS, preferred_element_type=F32)

        def finish(r, n, s, g=g, d=d, v_ref=v_ref, rows_of=rows_of):
            t = s + bias_scr[jnp.where(n == 0, 1, 0)]
            m = jnp.max(t, axis=-1, keepdims=True)
            p = jnp.exp2(t - m)
            vb = rows_of(v_ref, r, key_start(n), 2 * blk)
            v_ones = jnp.concatenate([vb, jnp.ones_like(vb)], axis=1)
            pv = jnp.dot(p.astype(BF16), v_ones, preferred_element_type=F32)
            den = pv[:, HEAD_DIM:]
            if d == 1:
                rows = pl.ds(pl.multiple_of(n * blk, blk), blk)
            else:
                rows = pl.ds(pl.multiple_of(n * (blk * d), blk) + r, blk, stride=d)
            o_scr[g, rows, :] = pv[:, :HEAD_DIM] / den
            lse_scr[g, rows, :] = m + jnp.log2(den)

        def run_blocks(tasks, scores=scores, finish=finish):
            nxt = scores(*tasks[0])
            for idx, (r, n) in enumerate(tasks):
                cur = nxt
                if idx + 1 < len(tasks):
                    nxt = scores(*tasks[idx + 1])
                finish(r, n, cur)

        n_per_iter = max(1, min(blocks_per_iter // d, n_blk))

        def body(it, carry, run_blocks=run_blocks, d=d, n_per_iter=n_per_iter):
            run_blocks([(r, it * n_per_iter + u) for u in range(n_per_iter) for r in range(d)])
            return carry

        lax.fori_loop(0, n_blk // n_per_iter, body, 0)

    def combine(c, carry):
        rows = pl.ds(pl.multiple_of(c * combine_rows, combine_rows), combine_rows)
        l1, l2, l3 = lse_scr[0, rows, :], lse_scr[1, rows, :], lse_scr[2, rows, :]
        lmax = jnp.maximum(jnp.maximum(l1, l2), l3)
        e1, e2, e3 = jnp.exp2(l1 - lmax), jnp.exp2(l2 - lmax), jnp.exp2(l3 - lmax)
        num = e1 * o_scr[0, rows, :] + e2 * o_scr[1, rows, :] + e3 * o_scr[2, rows, :]
        ob_ref[0, 0, rows, :] = num / (e1 + e2 + e3)
        return carry

    lax.fori_loop(0, seq_len // combine_rows, combine, 0)


def _dilated_call(qkv_heads, d4, d16, *, n_heads_b, q_off, k_off, v_off, blocks_per_iter=16):
    B, _, S, _ = qkv_heads.shape
    offs_nat = (q_off, k_off, v_off)
    offs_dil = (0, n_heads_b, 2 * n_heads_b)
    in_arrays = [qkv_heads] * 3 + [d4] * 3 + [d16] * 3
    in_specs = (
        [pl.BlockSpec((1, 1, S, HEAD_DIM), lambda b, h, off=off: (b, off + h, 0, 0))
         for off in offs_nat]
        + [pl.BlockSpec((1, 1, 4, S // 4, HEAD_DIM), lambda b, h, off=off: (b, off + h, 0, 0, 0))
           for off in offs_dil]
        + [pl.BlockSpec((1, 1, 16, S // 16, HEAD_DIM), lambda b, h, off=off: (b, off + h, 0, 0, 0))
           for off in offs_dil])
    n_br = len(DIL_PAIRS)
    return pl.pallas_call(
        functools.partial(_dilated_kernel, seq_len=S, blocks_per_iter=blocks_per_iter,
                          combine_rows=256),
        grid=(B, n_heads_b),
        in_specs=in_specs,
        out_specs=pl.BlockSpec((1, 1, S, HEAD_DIM), lambda b, h: (b, h, 0, 0)),
        out_shape=jax.ShapeDtypeStruct((B, n_heads_b, S, HEAD_DIM), F32),
        scratch_shapes=[pltpu.VMEM((n_br, S, HEAD_DIM), F32),
                        pltpu.VMEM((n_br, S, HEAD_DIM), F32),
                        pltpu.VMEM((2, DIL_BLOCK, 2 * DIL_BLOCK), F32)],
        compiler_params=pltpu.CompilerParams(
            dimension_semantics=("parallel", "parallel"),
            vmem_limit_bytes=V7X_VMEM_LIMIT_BYTES),
        name="dilated_attn",
    )(*in_arrays)


def _attnout_kernel(x_ref, oa_ref, ob_ref, ga_ref, gb_ref, wo_ref, pg_ref, fg_ref,
                    out_ref, hn_ref, mix_ref, *, row_chunks):
    tm = x_ref.shape[1]
    rows_per_chunk = tm // row_chunks
    for c in range(row_chunks):
        rows = slice(c * rows_per_chunk, (c + 1) * rows_per_chunk)
        col = 0
        for o_ref, g_ref in ((oa_ref, ga_ref), (ob_ref, gb_ref)):
            n_heads = o_ref.shape[1]
            ssq = None
            for h in range(n_heads):
                t = o_ref[0, h, rows, :]
                part = jnp.sum(t * t, axis=-1, keepdims=True)
                ssq = part if ssq is None else ssq + part
            inv = lax.rsqrt(ssq / (n_heads * HEAD_DIM) + RMS_EPS)
            for h in range(n_heads):
                gs = slice(h * HEAD_DIM, (h + 1) * HEAD_DIM)
                mix_ref[rows, col:col + HEAD_DIM] = (
                    o_ref[0, h, rows, :] * inv * g_ref[:, gs]).astype(BF16)
                col += HEAD_DIM
        y = jnp.dot(mix_ref[rows, :], wo_ref[...], preferred_element_type=F32)
        x_new = x_ref[0, rows, :] + y * _rms_scale(y) * pg_ref[...]
        out_ref[0, rows, :] = x_new
        hn_ref[0, rows, :] = (x_new * _rms_scale(x_new) * fg_ref[...]).astype(BF16)


def _attnout_call(x, oa, ob, ga, gb, wo_bf16, pg, ffn_g, *, tm=512, row_chunks=2):
    B, S, D = x.shape
    head_spec_a = pl.BlockSpec((1, oa.shape[1], tm, HEAD_DIM), lambda b, i: (b, 0, i, 0))
    head_spec_b = pl.BlockSpec((1, ob.shape[1], tm, HEAD_DIM), lambda b, i: (b, 0, i, 0))
    row_spec = pl.BlockSpec((1, tm, D), lambda b, i: (b, i, 0))
    return pl.pallas_call(
        functools.partial(_attnout_kernel, row_chunks=row_chunks),
        grid=(B, S // tm),
        in_specs=[
            row_spec,
            head_spec_a, head_spec_b,
            pl.BlockSpec((1, ga.shape[1]), lambda b, i: (0, 0)),
            pl.BlockSpec((1, gb.shape[1]), lambda b, i: (0, 0)),
            pl.BlockSpec(wo_bf16.shape, lambda b, i: (0, 0)),
            pl.BlockSpec((1, D), lambda b, i: (0, 0)),
            pl.BlockSpec((1, D), lambda b, i: (0, 0)),
        ],
        out_specs=[row_spec, row_spec],
        out_shape=[jax.ShapeDtypeStruct((B, S, D), F32), jax.ShapeDtypeStruct((B, S, D), BF16)],
        scratch_shapes=[pltpu.VMEM((tm, wo_bf16.shape[0]), BF16)],
        compiler_params=pltpu.CompilerParams(
            dimension_semantics=("parallel", "parallel"),
            vmem_limit_bytes=V7X_VMEM_LIMIT_BYTES),
        name="attn_out",
    )(x, oa, ob, ga, gb, wo_bf16, pg, ffn_g)


def _ffn_kernel(x_ref, hn_ref, wg_ref, wu_ref, cw_ref, cb_ref, wd_ref, pg_ref,
                out_ref, acc_ref, tail_ref, act_ref):
    i = pl.program_id(1)
    f = pl.program_id(2)
    tm = x_ref.shape[1]
    halo = tail_ref.shape[1]

    @pl.when(i == 0)
    def _():
        tail_ref[f] = jnp.zeros(tail_ref.shape[1:], F32)

    @pl.when(f == 0)
    def _():
        acc_ref[...] = jnp.zeros_like(acc_ref)

    gate = jnp.dot(hn_ref[0], wg_ref[...], preferred_element_type=F32)
    up = jnp.dot(hn_ref[0], wu_ref[...], preferred_element_type=F32)
    cw = cw_ref[...]
    cb = cb_ref[...]

    def gated(g_m2, g_m1, g_0, u):
        gc = cb + g_m2 * cw[0:1, :] + g_m1 * cw[1:2, :] + g_0 * cw[2:3, :]
        gelu = 0.5 * gc * (1.0 + jnp.tanh(0.7978845608028654 * (gc + 0.044715 * (gc * gc * gc))))
        return (gelu * u).astype(BF16)

    act_ref[...] = gated(pltpu.roll(gate, 2, 0), pltpu.roll(gate, 1, 0), gate, up)
    head = BF16_SUBLANE_TILE
    ext = jnp.concatenate([tail_ref[f], gate[0:head]], axis=0)
    act_ref[0:head, :] = gated(ext[halo - 2:halo - 2 + head], ext[halo - 1:halo - 1 + head],
                               gate[0:head], up[0:head])
    tail_ref[f] = gate[tm - halo:, :]
    acc_ref[...] += jnp.dot(act_ref[...], wd_ref[...], preferred_element_type=F32)

    @pl.when(f == pl.num_programs(2) - 1)
    def _():
        y = acc_ref[...]
        out_ref[0] = x_ref[0] + y * _rms_scale(y) * pg_ref[...]


def _ffn_call(x, hn, wg_bf16, wu_bf16, conv_w, conv_b, wd_bf16, pg, *, tm=512, tf=1024):
    B, S, D = x.shape
    d_ff = wg_bf16.shape[1]
    halo = 8
    assert CONV_WIDTH - 1 <= halo
    return pl.pallas_call(
        _ffn_kernel,
        grid=(B, S // tm, d_ff // tf),
        in_specs=[
            pl.BlockSpec((1, tm, D), lambda b, i, f: (b, i, 0)),
            pl.BlockSpec((1, tm, D), lambda b, i, f: (b, i, 0)),
            pl.BlockSpec((D, tf), lambda b, i, f: (0, f)),
            pl.BlockSpec((D, tf), lambda b, i, f: (0, f)),
            pl.BlockSpec((CONV_WIDTH, tf), lambda b, i, f: (0, f)),
            pl.BlockSpec((1, tf), lambda b, i, f: (0, f)),
            pl.BlockSpec((tf, D), lambda b, i, f: (f, 0)),
            pl.BlockSpec((1, D), lambda b, i, f: (0, 0)),
        ],
        out_specs=pl.BlockSpec((1, tm, D), lambda b, i, f: (b, i, 0)),
        out_shape=jax.ShapeDtypeStruct((B, S, D), F32),
        scratch_shapes=[pltpu.VMEM((tm, D), F32), pltpu.VMEM((d_ff // tf, halo, tf), F32),
                        pltpu.VMEM((tm, tf), BF16)],
        compiler_params=pltpu.CompilerParams(
            dimension_semantics=("parallel", "arbitrary", "arbitrary"),
            vmem_limit_bytes=V7X_VMEM_LIMIT_BYTES + 2 * 1024 * 1024),
        name="conv_glu_ffn",
    )(x, hn, wg_bf16, wu_bf16, conv_w, conv_b, wd_bf16, pg)


def _rope_tables(positions):
    inv_freq = ROPE_THETA ** (-jnp.arange(ROT_HALF, dtype=F32) / ROT_HALF)
    ang = positions.astype(F32)[..., None] * inv_freq
    cos, sin = jnp.cos(ang), jnp.sin(ang)
    ones = jnp.ones(ang.shape[:-1] + (HEAD_DIM - ROT_DIM,), F32)
    zeros_tail = jnp.zeros(ang.shape[:-1] + (HEAD_DIM - ROT_HALF,), F32)
    zeros_head = jnp.zeros_like(sin)
    cos_t = jnp.concatenate([cos, cos, ones], axis=-1)
    sa_t = jnp.concatenate([-sin, zeros_tail], axis=-1)
    sb_t = jnp.concatenate([zeros_head, sin, zeros_tail[..., :HEAD_DIM - ROT_DIM]], axis=-1)
    return cos_t, sa_t, sb_t


def kernel(x, positions, attn_pre_g, w_qkv, moba_out_g, dil_out_g, w_o, attn_post_g, ffn_pre_g,
           w_gate, w_up, conv_w, conv_b, w_down, ffn_post_g):
    depth = w_qkv.shape[0]
    n_heads_a = moba_out_g.shape[1] // HEAD_DIM
    n_heads_b = dil_out_g.shape[1] // HEAD_DIM
    cos_t, sa_t, sb_t = _rope_tables(positions)
    for l in range(depth):
        qkv_heads, vt, d4, d16 = _qkv_call(x, attn_pre_g[l][None], w_qkv[l].astype(BF16),
                                           cos_t, sa_t, sb_t,
                                           n_heads_a=n_heads_a, n_heads_b=n_heads_b)
        oa = _moba_call(qkv_heads, vt, n_heads_a=n_heads_a)
        ob = _dilated_call(qkv_heads, d4, d16, n_heads_b=n_heads_b, q_off=3 * n_heads_a,
                           k_off=3 * n_heads_a + n_heads_b, v_off=3 * n_heads_a + 2 * n_heads_b)
        x, hn = _attnout_call(x, oa, ob, moba_out_g[l][None], dil_out_g[l][None],
                              w_o[l].astype(BF16), attn_post_g[l][None], ffn_pre_g[l][None])
        x = _ffn_call(x, hn, w_gate[l].astype(BF16), w_up[l].astype(BF16),
                      conv_w[l], conv_b[l][None], w_down[l].astype(BF16), ffn_post_g[l][None])
    return x
```

```python
import functools

import jax
import jax.numpy as jnp
from jax import lax
from jax.experimental import pallas as pl
from jax.experimental.pallas import tpu as pltpu

F32 = jnp.float32
BF16 = jnp.bfloat16

HEAD_DIM = 128
ROT_DIM = HEAD_DIM // 4
ROT_HALF = ROT_DIM // 2
ROPE_THETA = 500000.0
MOBA_BLOCK = 256
MOBA_TOPK = 3
DIL_PAIRS = ((128, 1), (512, 4), (2048, 16))
DIL_BLOCK = 128
CONV_WIDTH = 3
RMS_EPS = 1e-6
SCALE = HEAD_DIM ** -0.5
SCALE_LOG2E = SCALE * 1.4426950408889634
NEG = -1e30

V7X_VMEM_LIMIT_BYTES = 56 * 1024 * 1024
BF16_SUBLANE_TILE = 16
V7X_MXU_WIDTH = 256
F32_SUBLANE_TILE = 8
QKV_SLAB_SLOTS = 2
MOBA_VT_ROWS = HEAD_DIM + BF16_SUBLANE_TILE

NT_DIMS = (((1,), (1,)), ((), ()))


def _rms_scale(x):
    return lax.rsqrt(jnp.mean(x * x, axis=-1, keepdims=True) + RMS_EPS)


def _qkv_kernel(x_ref, g_ref, w_ref, cos_ref, sa_ref, sb_ref,
                nat_ref, vt_ref, d4_ref, d16_ref, hn_ref, slab_ref, slab4_ref, *, heads_per_tile):
    j = pl.program_id(2)
    tm = hn_ref.shape[0]
    heads_per_dot = V7X_MXU_WIDTH // HEAD_DIM
    dot_width = heads_per_dot * HEAD_DIM

    @pl.when(j == 0)
    def _():
        x = x_ref[0]
        hn_ref[...] = (x * _rms_scale(x) * g_ref[...]).astype(BF16)

    def run(with_rope, with_vt, with_dilated):
        if with_rope:
            q_scale = jnp.where(jnp.logical_or(j == 0, j == 3), SCALE_LOG2E, 1.0).astype(F32)
            cos, sa, sb = cos_ref[0] * q_scale, sa_ref[0] * q_scale, sb_ref[0] * q_scale

            def rope(t):
                return (t * cos + pltpu.roll(t, HEAD_DIM - ROT_HALF, 1) * sa
                        + pltpu.roll(t, ROT_HALF, 1) * sb)

        for c in range(heads_per_tile // heads_per_dot):
            acc = jnp.dot(hn_ref[...], w_ref[:, c * dot_width:(c + 1) * dot_width],
                          preferred_element_type=F32)
            for hh in range(heads_per_dot):
                h = c * heads_per_dot + hh
                t = acc[:, hh * HEAD_DIM:(hh + 1) * HEAD_DIM]
                if with_rope:
                    t = rope(t)
                nat_ref[0, h] = t.astype(BF16)
                if with_vt:
                    for blk in range(tm // MOBA_BLOCK):
                        vt_ref[0, h, blk, 0:HEAD_DIM, :] = (
                            t[blk * MOBA_BLOCK:(blk + 1) * MOBA_BLOCK].T.astype(BF16))
                        vt_ref[0, h, blk, HEAD_DIM:, :] = jnp.ones(
                            (MOBA_VT_ROWS - HEAD_DIM, MOBA_BLOCK), BF16)
                if with_dilated:
                    slot = h % slab_ref.shape[0]
                    q4 = tm // 4
                    slab_ref[slot] = t
                    for r4 in range(4):
                        cls = slab_ref[slot, pl.ds(r4, q4, stride=4), :]
                        d4_ref[0, h, r4] = cls.astype(BF16)
                        slab4_ref[slot, r4 * q4:(r4 + 1) * q4, :] = cls
                    for r4 in range(4):
                        for m in range(4):
                            cls = slab4_ref[slot, pl.ds(r4 * q4 + m, q4 // 4, stride=4), :]
                            d16_ref[0, h, r4 + 4 * m] = cls.astype(BF16)

    pl.when(j < 2)(lambda: run(True, False, False))
    pl.when(j == 2)(lambda: run(False, True, False))
    pl.when(jnp.logical_or(j == 3, j == 4))(lambda: run(True, False, True))
    pl.when(j == 5)(lambda: run(False, False, True))


def _qkv_call(x, g, w_bf16, cos_t, sa_t, sb_t, *, n_heads_a, n_heads_b, tm=1024):
    B, S, D = x.shape
    N = w_bf16.shape[1]
    assert n_heads_a == n_heads_b and N == 3 * (n_heads_a + n_heads_b) * HEAD_DIM
    hpt = n_heads_a
    tn = hpt * HEAD_DIM
    first_b_tile = 3
    kern = functools.partial(_qkv_kernel, heads_per_tile=hpt)

    def dil_index(b, i, j):
        return (b, jnp.maximum(j - first_b_tile, 0), 0, i, 0)

    return pl.pallas_call(
        kern,
        grid=(B, S // tm, N // tn),
        in_specs=[
            pl.BlockSpec((1, tm, D), lambda b, i, j: (b, i, 0)),
            pl.BlockSpec((1, D), lambda b, i, j: (0, 0)),
            pl.BlockSpec((D, tn), lambda b, i, j: (0, j)),
            pl.BlockSpec((1, tm, HEAD_DIM), lambda b, i, j: (b, i, 0)),
            pl.BlockSpec((1, tm, HEAD_DIM), lambda b, i, j: (b, i, 0)),
            pl.BlockSpec((1, tm, HEAD_DIM), lambda b, i, j: (b, i, 0)),
        ],
        out_specs=[
            pl.BlockSpec((1, hpt, tm, HEAD_DIM), lambda b, i, j: (b, j, i, 0)),
            pl.BlockSpec((1, hpt, tm // MOBA_BLOCK, MOBA_VT_ROWS, MOBA_BLOCK),
                         lambda b, i, j: (b, 0, i, 0, 0)),
            pl.BlockSpec((1, hpt, 4, tm // 4, HEAD_DIM), dil_index),
            pl.BlockSpec((1, hpt, 16, tm // 16, HEAD_DIM), dil_index),
        ],
        out_shape=[
            jax.ShapeDtypeStruct((B, N // HEAD_DIM, S, HEAD_DIM), BF16),
            jax.ShapeDtypeStruct((B, n_heads_a, S // MOBA_BLOCK, MOBA_VT_ROWS, MOBA_BLOCK), BF16),
            jax.ShapeDtypeStruct((B, 3 * n_heads_b, 4, S // 4, HEAD_DIM), BF16),
            jax.ShapeDtypeStruct((B, 3 * n_heads_b, 16, S // 16, HEAD_DIM), BF16),
        ],
        scratch_shapes=[pltpu.VMEM((tm, D), BF16),
                        pltpu.VMEM((QKV_SLAB_SLOTS, tm, HEAD_DIM), F32),
                        pltpu.VMEM((QKV_SLAB_SLOTS, tm, HEAD_DIM), F32)],
        compiler_params=pltpu.CompilerParams(
            dimension_semantics=("parallel", "parallel", "arbitrary"),
            vmem_limit_bytes=V7X_VMEM_LIMIT_BYTES),
        name="qkv_rope",
    )(x, g, w_bf16, cos_t, sa_t, sb_t)


def _moba_kernel(q_ref, k_ref, vt_ref, o_ref, kmean_ref, bias_ref, acc_ref, *, n_blocks, heads,
                 lookahead=8, group=2):
    assert n_blocks % group == 0
    i = pl.program_id(2)
    blk_sz = MOBA_BLOCK

    @pl.when(i == 0)
    def _():
        for h in range(heads):
            for blk in range(n_blocks):
                kb = k_ref[0, h, blk * blk_sz:(blk + 1) * blk_sz, :].astype(F32)
                kmean_ref[h, blk:blk + 1, :] = jnp.mean(kb, axis=0, keepdims=True)

    def select_blocks(h, q):
        km = kmean_ref[h]
        km_hi = km.astype(BF16)
        km_lo = (km - km_hi.astype(F32)).astype(BF16)
        gate = (lax.dot_general(km_hi, q, NT_DIMS, preferred_element_type=F32)
                + lax.dot_general(km_lo, q, NT_DIMS, preferred_element_type=F32))
        blk_id = lax.broadcasted_iota(jnp.int32, gate.shape, 0).astype(F32)
        neg_inf = jnp.float32(-jnp.inf)
        g = jnp.where(blk_id < i.astype(F32), gate, neg_inf)
        sel = jnp.zeros(gate.shape, dtype=jnp.bool_)
        for _ in range(MOBA_TOPK):
            m = jnp.max(g, axis=0, keepdims=True)
            first = jnp.min(jnp.where(g == m, blk_id, float(n_blocks)), axis=0, keepdims=True)
            pick = jnp.logical_and(blk_id == first, m > neg_inf)
            sel = jnp.logical_or(sel, pick)
            g = jnp.where(pick, neg_inf, g)
        bias_ref[h] = jnp.where(sel, 0.0, NEG).astype(F32)

    def scores(h, first_blk, n_blk):
        rows = n_blk * blk_sz
        kb = k_ref[0, h, pl.ds(pl.multiple_of(first_blk * blk_sz, blk_sz), rows), :]
        return lax.dot_general(kb, q_ref[0, h], NT_DIMS, preferred_element_type=F32)

    def pipelined(stage_a, stage_b):
        ahead = [stage_a(h) for h in range(min(lookahead, heads))]
        outs = []
        for h in range(heads):
            if h + lookahead < heads:
                ahead.append(stage_a(h + lookahead))
            outs.append(stage_b(h, ahead[h]))
        return outs

    def own_a(h):
        select_blocks(h, q_ref[0, h])
        return scores(h, i, 1)

    def own_b(h, s):
        key_pos = lax.broadcasted_iota(jnp.int32, s.shape, 0)
        q_pos = lax.broadcasted_iota(jnp.int32, s.shape, 1)
        t = jnp.where(key_pos <= q_pos, s, NEG)
        m0 = jnp.max(t, axis=0, keepdims=True)
        p = jnp.exp2(t - m0)
        acc_ref[h] = jnp.dot(vt_ref[0, h, i], p.astype(BF16), preferred_element_type=F32)
        return m0

    init = pipelined(own_a, own_b)

    def body(c, carry):
        first = c * group

        def past_b(h, s):
            m_prev = carry[h]
            chunks = [s[g * blk_sz:(g + 1) * blk_sz] for g in range(group)]
            brows = [bias_ref[h, pl.ds(first + g, 1), :] for g in range(group)]
            m_new = m_prev
            for sg, brow in zip(chunks, brows):
                m_new = jnp.maximum(m_new, jnp.max(sg, axis=0, keepdims=True) + brow)
            acc_new = jnp.exp2(m_prev - m_new) * acc_ref[h]
            for g, (sg, brow) in enumerate(zip(chunks, brows)):
                pg = jnp.exp2(sg - (m_new - brow))
                acc_new = acc_new + jnp.dot(vt_ref[0, h, first + g], pg.astype(BF16),
                                            preferred_element_type=F32)
            acc_ref[h] = acc_new
            return m_new

        return tuple(pipelined(lambda h: scores(h, first, group), past_b))

    lax.fori_loop(0, (i + group - 1) // group, body, tuple(init))
    for h in range(heads):
        acc_fin = acc_ref[h]
        o_ref[0, h] = (acc_fin[:HEAD_DIM] / acc_fin[HEAD_DIM:HEAD_DIM + 1]).T


def _moba_call(qkv_heads, vt, *, n_heads_a, heads_per_step=8):
    B, _, S, _ = qkv_heads.shape
    n_blocks = S // MOBA_BLOCK
    tq = MOBA_BLOCK
    hps = heads_per_step
    assert n_heads_a % hps == 0
    k_first = n_heads_a // hps
    kern = functools.partial(_moba_kernel, n_blocks=n_blocks, heads=hps)
    return pl.pallas_call(
        kern,
        grid=(B, n_heads_a // hps, S // tq),
        in_specs=[
            pl.BlockSpec((1, hps, tq, HEAD_DIM), lambda b, h, i: (b, h, i, 0)),
            pl.BlockSpec((1, hps, S, HEAD_DIM), lambda b, h, i: (b, k_first + h, 0, 0)),
            pl.BlockSpec((1, hps, n_blocks, MOBA_VT_ROWS, MOBA_BLOCK),
                         lambda b, h, i: (b, h, 0, 0, 0)),
        ],
        out_specs=pl.BlockSpec((1, hps, tq, HEAD_DIM), lambda b, h, i: (b, h, i, 0)),
        out_shape=jax.ShapeDtypeStruct((B, n_heads_a, S, HEAD_DIM), F32),
        scratch_shapes=[pltpu.VMEM((hps, n_blocks, HEAD_DIM), F32),
                        pltpu.VMEM((hps, n_blocks, tq), F32),
                        pltpu.VMEM((hps, MOBA_VT_ROWS, tq), F32)],
        compiler_params=pltpu.CompilerParams(
            dimension_semantics=("parallel", "parallel", "arbitrary"),
            vmem_limit_bytes=V7X_VMEM_LIMIT_BYTES),
        name="moba_attn",
    )(qkv_heads, qkv_heads, vt)


def _dilated_kernel(q1_ref, k1_ref, v1_ref, q4_ref, k4_ref, v4_ref, q16_ref, k16_ref, v16_ref,
                    ob_ref, o_scr, lse_scr, bias_scr, *, seq_len, blocks_per_iter, combine_rows):
    blk = DIL_BLOCK
    branch_refs = ((q1_ref, k1_ref, v1_ref), (q4_ref, k4_ref, v4_ref), (q16_ref, k16_ref, v16_ref))

    qi = lax.broadcasted_iota(jnp.int32, (blk, 2 * blk), 0)
    ki = lax.broadcasted_iota(jnp.int32, (blk, 2 * blk), 1)
    dist = qi + blk - ki
    bias_scr[0] = jnp.where(jnp.logical_and(dist >= 0, dist <= blk), 0.0, NEG).astype(F32)
    bias_scr[1] = jnp.where(ki <= qi, 0.0, NEG).astype(F32)

    for g, (window, d) in enumerate(DIL_PAIRS):
        assert window // d == DIL_BLOCK
        q_ref, k_ref, v_ref = branch_refs[g]
        n_blk = seq_len // d // blk

        def rows_of(ref, r, start, size, d=d):
            if d == 1:
                return ref[0, 0, pl.ds(start, size), :]
            return ref[0, 0, r, pl.ds(start, size), :]

        def key_start(n):
            return pl.multiple_of(jnp.maximum(n - 1, 0) * blk, blk)

        def scores(r, n, q_ref=q_ref, k_ref=k_ref, rows_of=rows_of):
            qb = rows_of(q_ref, r, pl.multiple_of(n * blk, blk), blk)
            kb = rows_of(k_ref, r, key_start(n), 2 * blk)
            return lax.dot_general(qb, kb, NT_DIMS, preferred_element_type=F32)

        def finish(r, n, s, g=g, d=d, v_ref=v_ref, rows_of=rows_of):
            t = s + bias_scr[jnp.where(n == 0, 1, 0)]
            m = jnp.max(t, axis=-1, keepdims=True)
            p = jnp.exp2(t - m)
            vb = rows_of(v_ref, r, key_start(n), 2 * blk)
            v_ones = jnp.concatenate([vb, jnp.ones_like(vb)], axis=1)
            pv = jnp.dot(p.astype(BF16), v_ones, preferred_element_type=F32)
            den = pv[:, HEAD_DIM:]
            q_start = pl.multiple_of(n * blk, blk)
            if d == 1:
                rows = pl.ds(q_start, blk)
            elif d == 4:
                rows = pl.ds(r * (seq_len // 4) + q_start, blk)
            else:
                r4, m4 = r % 4, r // 4
                rows = pl.ds(r4 * (seq_len // 4) + 4 * q_start + m4, blk, stride=4)
            o_scr[g, rows, :] = pv[:, :HEAD_DIM] / den
            lse_scr[g, rows, :] = m + jnp.log2(den)

        def run_blocks(tasks, scores=scores, finish=finish):
            nxt = scores(*tasks[0])
            for idx, (r, n) in enumerate(tasks):
                cur = nxt
                if idx + 1 < len(tasks):
                    nxt = scores(*tasks[idx + 1])
                finish(r, n, cur)

        n_per_iter = max(1, min(blocks_per_iter // d, n_blk))

        def body(it, carry, run_blocks=run_blocks, d=d, n_per_iter=n_per_iter):
            run_blocks([(r, it * n_per_iter + u) for u in range(n_per_iter) for r in range(d)])
            return carry

        lax.fori_loop(0, n_blk // n_per_iter, body, 0)

    def combine(c, carry):
        i0 = pl.multiple_of(c * combine_rows, combine_rows)
        for r4 in range(4):
            tok = pl.ds(4 * i0 + r4, combine_rows, stride=4)
            cls = pl.ds(r4 * (seq_len // 4) + i0, combine_rows)
            l1, l2, l3 = lse_scr[0, tok, :], lse_scr[1, cls, :], lse_scr[2, cls, :]
            lmax = jnp.maximum(jnp.maximum(l1, l2), l3)
            e1, e2, e3 = jnp.exp2(l1 - lmax), jnp.exp2(l2 - lmax), jnp.exp2(l3 - lmax)
            num = e1 * o_scr[0, tok, :] + e2 * o_scr[1, cls, :] + e3 * o_scr[2, cls, :]
            ob_ref[0, 0, tok, :] = num / (e1 + e2 + e3)
        return carry

    lax.fori_loop(0, seq_len // 4 // combine_rows, combine, 0)


def _dilated_call(qkv_heads, d4, d16, *, n_heads_b, q_off, k_off, v_off, blocks_per_iter=16):
    B, _, S, _ = qkv_heads.shape
    offs_nat = (q_off, k_off, v_off)
    offs_dil = (0, n_heads_b, 2 * n_heads_b)
    in_arrays = [qkv_heads] * 3 + [d4] * 3 + [d16] * 3
    in_specs = (
        [pl.BlockSpec((1, 1, S, HEAD_DIM), lambda b, h, off=off: (b, off + h, 0, 0))
         for off in offs_nat]
        + [pl.BlockSpec((1, 1, 4, S // 4, HEAD_DIM), lambda b, h, off=off: (b, off + h, 0, 0, 0))
           for off in offs_dil]
        + [pl.BlockSpec((1, 1, 16, S // 16, HEAD_DIM), lambda b, h, off=off: (b, off + h, 0, 0, 0))
           for off in offs_dil])
    n_br = len(DIL_PAIRS)
    return pl.pallas_call(
        functools.partial(_dilated_kernel, seq_len=S, blocks_per_iter=blocks_per_iter,
                          combine_rows=64),
        grid=(B, n_heads_b),
        in_specs=in_specs,
        out_specs=pl.BlockSpec((1, 1, S, HEAD_DIM), lambda b, h: (b, h, 0, 0)),
        out_shape=jax.ShapeDtypeStruct((B, n_heads_b, S, HEAD_DIM), F32),
        scratch_shapes=[pltpu.VMEM((n_br, S, HEAD_DIM), F32),
                        pltpu.VMEM((n_br, S, HEAD_DIM), F32),
                        pltpu.VMEM((2, DIL_BLOCK, 2 * DIL_BLOCK), F32)],
        compiler_params=pltpu.CompilerParams(
            dimension_semantics=("parallel", "parallel"),
            vmem_limit_bytes=V7X_VMEM_LIMIT_BYTES),
        name="dilated_attn",
    )(*in_arrays)


def _attnout_kernel(x_ref, oa_ref, ob_ref, ga_ref, gb_ref, wo_ref, pg_ref, fg_ref,
                    out_ref, hn_ref, mix_ref, *, row_chunks):
    tm = x_ref.shape[1]
    rows_per_chunk = tm // row_chunks
    for c in range(row_chunks):
        rows = slice(c * rows_per_chunk, (c + 1) * rows_per_chunk)
        col = 0
        for o_ref, g_ref in ((oa_ref, ga_ref), (ob_ref, gb_ref)):
            n_heads = o_ref.shape[1]
            ssq = None
            for h in range(n_heads):
                t = o_ref[0, h, rows, :]
                part = jnp.sum(t * t, axis=-1, keepdims=True)
                ssq = part if ssq is None else ssq + part
            inv = lax.rsqrt(ssq / (n_heads * HEAD_DIM) + RMS_EPS)
            for h in range(n_heads):
                gs = slice(h * HEAD_DIM, (h + 1) * HEAD_DIM)
                mix_ref[rows, col:col + HEAD_DIM] = (
                    o_ref[0, h, rows, :] * inv * g_ref[:, gs]).astype(BF16)
                col += HEAD_DIM
        y = jnp.dot(mix_ref[rows, :], wo_ref[...], preferred_element_type=F32)
        x_new = x_ref[0, rows, :] + y * _rms_scale(y) * pg_ref[...]
        out_ref[0, rows, :] = x_new
        hn_ref[0, rows, :] = (x_new * _rms_scale(x_new) * fg_ref[...]).astype(BF16)


def _attnout_call(x, oa, ob, ga, gb, wo_bf16, pg, ffn_g, *, tm=512, row_chunks=2):
    B, S, D = x.shape
    head_spec_a = pl.BlockSpec((1, oa.shape[1], tm, HEAD_DIM), lambda b, i: (b, 0, i, 0))
    head_spec_b = pl.BlockSpec((1, ob.shape[1], tm, HEAD_DIM), lambda b, i: (b, 0, i, 0))
    row_spec = pl.BlockSpec((1, tm, D), lambda b, i: (b, i, 0))
    return pl.pallas_call(
        functools.partial(_attnout_kernel, row_chunks=row_chunks),
        grid=(B, S // tm),
        in_specs=[
            row_spec,
            head_spec_a, head_spec_b,
            pl.BlockSpec((1, ga.shape[1]), lambda b, i: (0, 0)),
            pl.BlockSpec((1, gb.shape[1]), lambda b, i: (0, 0)),
            pl.BlockSpec(wo_bf16.shape, lambda b, i: (0, 0)),
            pl.BlockSpec((1, D), lambda b, i: (0, 0)),
            pl.BlockSpec((1, D), lambda b, i: (0, 0)),
        ],
        out_specs=[row_spec, row_spec],
        out_shape=[jax.ShapeDtypeStruct((B, S, D), F32), jax.ShapeDtypeStruct((B, S, D), BF16)],
        scratch_shapes=[pltpu.VMEM((tm, wo_bf16.shape[0]), BF16)],
        compiler_params=pltpu.CompilerParams(
            dimension_semantics=("parallel", "parallel"),
            vmem_limit_bytes=V7X_VMEM_LIMIT_BYTES),
        name="attn_out",
    )(x, oa, ob, ga, gb, wo_bf16, pg, ffn_g)


def _ffn_kernel(x_ref, hn_ref, wg_ref, wu_ref, cw_ref, cb_ref, wd_ref, pg_ref,
                out_ref, acc_ref, tail_ref, act_ref):
    i = pl.program_id(1)
    f = pl.program_id(2)
    tm = x_ref.shape[1]
    halo = tail_ref.shape[1]

    @pl.when(i == 0)
    def _():
        tail_ref[f] = jnp.zeros(tail_ref.shape[1:], F32)

    @pl.when(f == 0)
    def _():
        acc_ref[...] = jnp.zeros_like(acc_ref)

    gate = jnp.dot(hn_ref[0], wg_ref[...], preferred_element_type=F32)
    up = jnp.dot(hn_ref[0], wu_ref[...], preferred_element_type=F32)
    cw = cw_ref[...]
    cb = cb_ref[...]

    def gated(g_m2, g_m1, g_0, u):
        gc = cb + g_m2 * cw[0:1, :] + g_m1 * cw[1:2, :] + g_0 * cw[2:3, :]
        gelu = 0.5 * gc * (1.0 + jnp.tanh(0.7978845608028654 * (gc + 0.044715 * (gc * gc * gc))))
        return (gelu * u).astype(BF16)

    act_ref[...] = gated(pltpu.roll(gate, 2, 0), pltpu.roll(gate, 1, 0), gate, up)
    head = BF16_SUBLANE_TILE
    ext = jnp.concatenate([tail_ref[f], gate[0:head]], axis=0)
    act_ref[0:head, :] = gated(ext[halo - 2:halo - 2 + head], ext[halo - 1:halo - 1 + head],
                               gate[0:head], up[0:head])
    tail_ref[f] = gate[tm - halo:, :]
    acc_ref[...] += jnp.dot(act_ref[...], wd_ref[...], preferred_element_type=F32)

    @pl.when(f == pl.num_programs(2) - 1)
    def _():
        y = acc_ref[...]
        out_ref[0] = x_ref[0] + y * _rms_scale(y) * pg_ref[...]


def _ffn_call(x, hn, wg_bf16, wu_bf16, conv_w, conv_b, wd_bf16, pg, *, tm=512, tf=1024):
    B, S, D = x.shape
    d_ff = wg_bf16.shape[1]
    halo = 8
    assert CONV_WIDTH - 1 <= halo
    return pl.pallas_call(
        _ffn_kernel,
        grid=(B, S // tm, d_ff // tf),
        in_specs=[
            pl.BlockSpec((1, tm, D), lambda b, i, f: (b, i, 0)),
            pl.BlockSpec((1, tm, D), lambda b, i, f: (b, i, 0)),
            pl.BlockSpec((D, tf), lambda b, i, f: (0, f)),
            pl.BlockSpec((D, tf), lambda b, i, f: (0, f)),
            pl.BlockSpec((CONV_WIDTH, tf), lambda b, i, f: (0, f)),
            pl.BlockSpec((1, tf), lambda b, i, f: (0, f)),
            pl.BlockSpec((tf, D), lambda b, i, f: (f, 0)),
            pl.BlockSpec((1, D), lambda b, i, f: (0, 0)),
        ],
        out_specs=pl.BlockSpec((1, tm, D), lambda b, i, f: (b, i, 0)),
        out_shape=jax.ShapeDtypeStruct((B, S, D), F32),
        scratch_shapes=[pltpu.VMEM((tm, D), F32), pltpu.VMEM((d_ff // tf, halo, tf), F32),
                        pltpu.VMEM((tm, tf), BF16)],
        compiler_params=pltpu.CompilerParams(
            dimension_semantics=("parallel", "arbitrary", "arbitrary"),
            vmem_limit_bytes=V7X_VMEM_LIMIT_BYTES + 2 * 1024 * 1024),
        name="conv_glu_ffn",
    )(x, hn, wg_bf16, wu_bf16, conv_w, conv_b, wd_bf16, pg)


def _rope_tables(positions):
    inv_freq = ROPE_THETA ** (-jnp.arange(ROT_HALF, dtype=F32) / ROT_HALF)
    ang = positions.astype(F32)[..., None] * inv_freq
    cos, sin = jnp.cos(ang), jnp.sin(ang)
    ones = jnp.ones(ang.shape[:-1] + (HEAD_DIM - ROT_DIM,), F32)
    zeros_tail = jnp.zeros(ang.shape[:-1] + (HEAD_DIM - ROT_HALF,), F32)
    zeros_head = jnp.zeros_like(sin)
    cos_t = jnp.concatenate([cos, cos, ones], axis=-1)
    sa_t = jnp.concatenate([-sin, zeros_tail], axis=-1)
    sb_t = jnp.concatenate([zeros_head, sin, zeros_tail[..., :HEAD_DIM - ROT_DIM]], axis=-1)
    return cos_t, sa_t, sb_t


def kernel(x, positions, attn_pre_g, w_qkv, moba_out_g, dil_out_g, w_o, attn_post_g, ffn_pre_g,
           w_gate, w_up, conv_w, conv_b, w_down, ffn_post_g):
    depth = w_qkv.shape[0]
    n_heads_a = moba_out_g.shape[1] // HEAD_DIM
    n_heads_b = dil_out_g.shape[1] // HEAD_DIM
    cos_t, sa_t, sb_t = _rope_tables(positions)
    for l in range(depth):
        qkv_heads, vt, d4, d16 = _qkv_call(x, attn_pre_g[l][None], w_qkv[l].astype(BF16),
                                           cos_t, sa_t, sb_t,
                                           n_heads_a=n_heads_a, n_heads_b=n_heads_b)
        oa = _moba_call(qkv_heads, vt, n_heads_a=n_heads_a)
        ob = _dilated_call(qkv_heads, d4, d16, n_heads_b=n_heads_b, q_off=3 * n_heads_a,
                           k_off=3 * n_heads_a + n_heads_b, v_off=3 * n_heads_a + 2 * n_heads_b)
        x, hn = _attnout_call(x, oa, ob, moba_out_g[l][None], dil_out_g[l][None],
                              w_o[l].astype(BF16), attn_post_g[l][None], ffn_pre_g[l][None])
        x = _ffn_call(x, hn, w_gate[l].astype(BF16), w_up[l].astype(BF16),
                      conv_w[l], conv_b[l][None], w_down[l].astype(BF16), ffn_post_g[l][None])
    return x
```

```python
import functools

import jax
import jax.numpy as jnp
from jax import lax
from jax.experimental import pallas as pl
from jax.experimental.pallas import tpu as pltpu

F32 = jnp.float32
BF16 = jnp.bfloat16

HEAD_DIM = 128
ROT_DIM = HEAD_DIM // 4
ROT_HALF = ROT_DIM // 2
ROPE_THETA = 500000.0
MOBA_BLOCK = 256
MOBA_TOPK = 3
DIL_PAIRS = ((128, 1), (512, 4), (2048, 16))
DIL_BLOCK = 128
CONV_WIDTH = 3
RMS_EPS = 1e-6
SCALE = HEAD_DIM ** -0.5
SCALE_LOG2E = SCALE * 1.4426950408889634
NEG = -1e30

V7X_VMEM_LIMIT_BYTES = 56 * 1024 * 1024
V7X_VMEM_LIMIT_FFN_BYTES = 58 * 1024 * 1024
BF16_SUBLANE_TILE = 16
V7X_MXU_WIDTH = 256
F32_SUBLANE_TILE = 8
QKV_SLAB_SLOTS = 2
MOBA_VT_ROWS = HEAD_DIM + BF16_SUBLANE_TILE

NT_DIMS = (((1,), (1,)), ((), ()))


def _rms_scale(x):
    return lax.rsqrt(jnp.mean(x * x, axis=-1, keepdims=True) + RMS_EPS)


def _qkv_kernel(x_ref, g_ref, w_ref, cos_ref, sa_ref, sb_ref,
                nat_ref, vt_ref, d4_ref, d16_ref, hn_ref, slab_ref, slab4_ref, *, heads_per_tile):
    j = pl.program_id(2)
    tm = hn_ref.shape[0]
    heads_per_dot = V7X_MXU_WIDTH // HEAD_DIM
    dot_width = heads_per_dot * HEAD_DIM

    @pl.when(j == 0)
    def _():
        x = x_ref[0]
        hn_ref[...] = (x * _rms_scale(x) * g_ref[...]).astype(BF16)

    def run(with_rope, with_vt, with_dilated):
        if with_rope:
            q_scale = jnp.where(jnp.logical_or(j == 0, j == 3), SCALE_LOG2E, 1.0).astype(F32)
            cos, sa, sb = cos_ref[0] * q_scale, sa_ref[0] * q_scale, sb_ref[0] * q_scale

            def rope(t):
                return (t * cos + pltpu.roll(t, HEAD_DIM - ROT_HALF, 1) * sa
                        + pltpu.roll(t, ROT_HALF, 1) * sb)

        for c in range(heads_per_tile // heads_per_dot):
            acc = jnp.dot(hn_ref[...], w_ref[:, c * dot_width:(c + 1) * dot_width],
                          preferred_element_type=F32)
            for hh in range(heads_per_dot):
                h = c * heads_per_dot + hh
                t = acc[:, hh * HEAD_DIM:(hh + 1) * HEAD_DIM]
                if with_rope:
                    t = rope(t)
                nat_ref[0, h] = t.astype(BF16)
                if with_vt:
                    for blk in range(tm // MOBA_BLOCK):
                        vt_ref[0, h, blk, 0:HEAD_DIM, :] = (
                            t[blk * MOBA_BLOCK:(blk + 1) * MOBA_BLOCK].T.astype(BF16))
                        vt_ref[0, h, blk, HEAD_DIM:, :] = jnp.ones(
                            (MOBA_VT_ROWS - HEAD_DIM, MOBA_BLOCK), BF16)
                if with_dilated:
                    slot = h % slab_ref.shape[0]
                    q4 = tm // 4
                    slab_ref[slot] = t
                    for r4 in range(4):
                        cls = slab_ref[slot, pl.ds(r4, q4, stride=4), :]
                        d4_ref[0, h, r4] = cls.astype(BF16)
                        slab4_ref[slot, r4 * q4:(r4 + 1) * q4, :] = cls
                    for r4 in range(4):
                        for m in range(4):
                            cls = slab4_ref[slot, pl.ds(r4 * q4 + m, q4 // 4, stride=4), :]
                            d16_ref[0, h, r4 + 4 * m] = cls.astype(BF16)

    pl.when(j < 2)(lambda: run(True, False, False))
    pl.when(j == 2)(lambda: run(False, True, False))
    pl.when(jnp.logical_or(j == 3, j == 4))(lambda: run(True, False, True))
    pl.when(j == 5)(lambda: run(False, False, True))


def _qkv_call(x, g, w_bf16, cos_t, sa_t, sb_t, *, n_heads_a, n_heads_b, tm=1024):
    B, S, D = x.shape
    N = w_bf16.shape[1]
    assert n_heads_a == n_heads_b and N == 3 * (n_heads_a + n_heads_b) * HEAD_DIM
    hpt = n_heads_a
    tn = hpt * HEAD_DIM
    first_b_tile = 3
    kern = functools.partial(_qkv_kernel, heads_per_tile=hpt)

    def dil_index(b, i, j):
        return (b, jnp.maximum(j - first_b_tile, 0), 0, i, 0)

    return pl.pallas_call(
        kern,
        grid=(B, S // tm, N // tn),
        in_specs=[
            pl.BlockSpec((1, tm, D), lambda b, i, j: (b, i, 0)),
            pl.BlockSpec((1, D), lambda b, i, j: (0, 0)),
            pl.BlockSpec((D, tn), lambda b, i, j: (0, j)),
            pl.BlockSpec((1, tm, HEAD_DIM), lambda b, i, j: (b, i, 0)),
            pl.BlockSpec((1, tm, HEAD_DIM), lambda b, i, j: (b, i, 0)),
            pl.BlockSpec((1, tm, HEAD_DIM), lambda b, i, j: (b, i, 0)),
        ],
        out_specs=[
            pl.BlockSpec((1, hpt, tm, HEAD_DIM), lambda b, i, j: (b, j, i, 0)),
            pl.BlockSpec((1, hpt, tm // MOBA_BLOCK, MOBA_VT_ROWS, MOBA_BLOCK),
                         lambda b, i, j: (b, 0, i, 0, 0)),
            pl.BlockSpec((1, hpt, 4, tm // 4, HEAD_DIM), dil_index),
            pl.BlockSpec((1, hpt, 16, tm // 16, HEAD_DIM), dil_index),
        ],
        out_shape=[
            jax.ShapeDtypeStruct((B, N // HEAD_DIM, S, HEAD_DIM), BF16),
            jax.ShapeDtypeStruct((B, n_heads_a, S // MOBA_BLOCK, MOBA_VT_ROWS, MOBA_BLOCK), BF16),
            jax.ShapeDtypeStruct((B, 3 * n_heads_b, 4, S // 4, HEAD_DIM), BF16),
            jax.ShapeDtypeStruct((B, 3 * n_heads_b, 16, S // 16, HEAD_DIM), BF16),
        ],
        scratch_shapes=[pltpu.VMEM((tm, D), BF16),
                        pltpu.VMEM((QKV_SLAB_SLOTS, tm, HEAD_DIM), F32),
                        pltpu.VMEM((QKV_SLAB_SLOTS, tm, HEAD_DIM), F32)],
        compiler_params=pltpu.CompilerParams(
            dimension_semantics=("parallel", "parallel", "arbitrary"),
            vmem_limit_bytes=V7X_VMEM_LIMIT_BYTES),
        name="qkv_rope",
    )(x, g, w_bf16, cos_t, sa_t, sb_t)


def _moba_kernel(*refs, n_blocks, heads, n_cast, lookahead=8, group=2):
    q_ref, k_ref, vt_ref = refs[:3]
    cast_in = refs[3:3 + n_cast]
    o_ref = refs[3 + n_cast]
    cast_out = refs[4 + n_cast:4 + 2 * n_cast]
    kmean_ref, bias_ref, acc_ref = refs[4 + 2 * n_cast:]
    assert n_blocks % group == 0
    i = pl.program_id(2)
    blk_sz = MOBA_BLOCK

    for w_in, w_out in zip(cast_in, cast_out):
        w_out[0] = w_in[0].astype(BF16)

    @pl.when(i == 0)
    def _():
        for h in range(heads):
            for blk in range(n_blocks):
                kb = k_ref[0, h, blk * blk_sz:(blk + 1) * blk_sz, :].astype(F32)
                kmean_ref[h, blk:blk + 1, :] = jnp.mean(kb, axis=0, keepdims=True)

    def select_blocks(h, q):
        km = kmean_ref[h]
        km_hi = km.astype(BF16)
        km_lo = (km - km_hi.astype(F32)).astype(BF16)
        gate = (lax.dot_general(km_hi, q, NT_DIMS, preferred_element_type=F32)
                + lax.dot_general(km_lo, q, NT_DIMS, preferred_element_type=F32))
        blk_id = lax.broadcasted_iota(jnp.int32, gate.shape, 0).astype(F32)
        neg_inf = jnp.float32(-jnp.inf)
        g = jnp.where(blk_id < i.astype(F32), gate, neg_inf)
        sel = jnp.zeros(gate.shape, dtype=jnp.bool_)
        for _ in range(MOBA_TOPK):
            m = jnp.max(g, axis=0, keepdims=True)
            first = jnp.min(jnp.where(g == m, blk_id, float(n_blocks)), axis=0, keepdims=True)
            pick = jnp.logical_and(blk_id == first, m > neg_inf)
            sel = jnp.logical_or(sel, pick)
            g = jnp.where(pick, neg_inf, g)
        bias_ref[h] = jnp.where(sel, 0.0, NEG).astype(F32)

    def scores(h, first_blk, n_blk):
        rows = n_blk * blk_sz
        kb = k_ref[0, h, pl.ds(pl.multiple_of(first_blk * blk_sz, blk_sz), rows), :]
        return lax.dot_general(kb, q_ref[0, h], NT_DIMS, preferred_element_type=F32)

    def pipelined(stage_a, stage_b):
        ahead = [stage_a(h) for h in range(min(lookahead, heads))]
        outs = []
        for h in range(heads):
            if h + lookahead < heads:
                ahead.append(stage_a(h + lookahead))
            outs.append(stage_b(h, ahead[h]))
        return outs

    def own_a(h):
        select_blocks(h, q_ref[0, h])
        return scores(h, i, 1)

    def own_b(h, s):
        key_pos = lax.broadcasted_iota(jnp.int32, s.shape, 0)
        q_pos = lax.broadcasted_iota(jnp.int32, s.shape, 1)
        t = jnp.where(key_pos <= q_pos, s, NEG)
        m0 = jnp.max(t, axis=0, keepdims=True)
        p = jnp.exp2(t - m0)
        acc_ref[h] = jnp.dot(vt_ref[0, h, i], p.astype(BF16), preferred_element_type=F32)
        return m0

    init = pipelined(own_a, own_b)

    def body(c, carry):
        first = c * group

        def past_b(h, s):
            m_prev = carry[h]
            chunks = [s[g * blk_sz:(g + 1) * blk_sz] for g in range(group)]
            brows = [bias_ref[h, pl.ds(first + g, 1), :] for g in range(group)]
            m_new = m_prev
            for sg, brow in zip(chunks, brows):
                m_new = jnp.maximum(m_new, jnp.max(sg, axis=0, keepdims=True) + brow)
            acc_new = jnp.exp2(m_prev - m_new) * acc_ref[h]
            for g, (sg, brow) in enumerate(zip(chunks, brows)):
                pg = jnp.exp2(sg - (m_new - brow))
                acc_new = acc_new + jnp.dot(vt_ref[0, h, first + g], pg.astype(BF16),
                                            preferred_element_type=F32)
            acc_ref[h] = acc_new
            return m_new

        return tuple(pipelined(lambda h: scores(h, first, group), past_b))

    lax.fori_loop(0, (i + group - 1) // group, body, tuple(init))
    for h in range(heads):
        acc_fin = acc_ref[h]
        o_ref[0, h] = (acc_fin[:HEAD_DIM] / acc_fin[HEAD_DIM:HEAD_DIM + 1]).T


def _moba_call(qkv_heads, vt, f32_weights, *, n_heads_a, heads_per_step=8):
    B, _, S, _ = qkv_heads.shape
    n_blocks = S // MOBA_BLOCK
    tq = MOBA_BLOCK
    hps = heads_per_step
    assert n_heads_a % hps == 0
    n_groups = n_heads_a // hps
    k_first = n_groups
    grid = (B, n_groups, S // tq)
    n_steps = grid[0] * grid[1] * grid[2]

    def slab_index(b, h, i):
        return ((b * grid[1] + h) * grid[2] + i, 0, 0)

    slabs, slab_specs, slab_shapes = [], [], []
    for w in f32_weights:
        rows, cols = w.shape
        assert rows % (n_steps * BF16_SUBLANE_TILE) == 0
        slabs.append(w.reshape(n_steps, rows // n_steps, cols))
        slab_specs.append(pl.BlockSpec((1, rows // n_steps, cols), slab_index))
        slab_shapes.append(jax.ShapeDtypeStruct((n_steps, rows // n_steps, cols), BF16))

    kern = functools.partial(_moba_kernel, n_blocks=n_blocks, heads=hps, n_cast=len(slabs))
    outs = pl.pallas_call(
        kern,
        grid=grid,
        in_specs=[
            pl.BlockSpec((1, hps, tq, HEAD_DIM), lambda b, h, i: (b, h, i, 0)),
            pl.BlockSpec((1, hps, S, HEAD_DIM), lambda b, h, i: (b, k_first + h, 0, 0)),
            pl.BlockSpec((1, hps, n_blocks, MOBA_VT_ROWS, MOBA_BLOCK),
                         lambda b, h, i: (b, h, 0, 0, 0)),
        ] + slab_specs,
        out_specs=[pl.BlockSpec((1, hps, tq, HEAD_DIM), lambda b, h, i: (b, h, i, 0))] + slab_specs,
        out_shape=[jax.ShapeDtypeStruct((B, n_heads_a, S, HEAD_DIM), F32)] + slab_shapes,
        scratch_shapes=[pltpu.VMEM((hps, n_blocks, HEAD_DIM), F32),
                        pltpu.VMEM((hps, n_blocks, tq), F32),
                        pltpu.VMEM((hps, MOBA_VT_ROWS, tq), F32)],
        compiler_params=pltpu.CompilerParams(
            dimension_semantics=("parallel", "parallel", "arbitrary"),
            vmem_limit_bytes=V7X_VMEM_LIMIT_BYTES),
        name="moba_attn",
    )(qkv_heads, qkv_heads, vt, *slabs)
    return outs[0], [o.reshape(w.shape) for o, w in zip(outs[1:], f32_weights)]


def _dilated_kernel(q1_ref, k1_ref, v1_ref, q4_ref, k4_ref, v4_ref, q16_ref, k16_ref, v16_ref,
                    ob_ref, o_scr, lse_scr, bias_scr, *, seq_len, blocks_per_iter, combine_rows):
    blk = DIL_BLOCK
    branch_refs = ((q1_ref, k1_ref, v1_ref), (q4_ref, k4_ref, v4_ref), (q16_ref, k16_ref, v16_ref))

    qi = lax.broadcasted_iota(jnp.int32, (blk, 2 * blk), 0)
    ki = lax.broadcasted_iota(jnp.int32, (blk, 2 * blk), 1)
    dist = qi + blk - ki
    bias_scr[0] = jnp.where(jnp.logical_and(dist >= 0, dist <= blk), 0.0, NEG).astype(F32)
    bias_scr[1] = jnp.where(ki <= qi, 0.0, NEG).astype(F32)

    for g, (window, d) in enumerate(DIL_PAIRS):
        assert window // d == DIL_BLOCK
        q_ref, k_ref, v_ref = branch_refs[g]
        n_blk = seq_len // d // blk

        def rows_of(ref, r, start, size, d=d):
            if d == 1:
                return ref[0, 0, pl.ds(start, size), :]
            return ref[0, 0, r, pl.ds(start, size), :]

        def key_start(n):
            return pl.multiple_of(jnp.maximum(n - 1, 0) * blk, blk)

        def scores(r, n, q_ref=q_ref, k_ref=k_ref, rows_of=rows_of):
            qb = rows_of(q_ref, r, pl.multiple_of(n * blk, blk), blk)
            kb = rows_of(k_ref, r, key_start(n), 2 * blk)
            return lax.dot_general(qb, kb, NT_DIMS, preferred_element_type=F32)

        def finish(r, n, s, g=g, d=d, v_ref=v_ref, rows_of=rows_of):
            t = s + bias_scr[jnp.where(n == 0, 1, 0)]
            m = jnp.max(t, axis=-1, keepdims=True)
            p = jnp.exp2(t - m)
            vb = rows_of(v_ref, r, key_start(n), 2 * blk)
            v_ones = jnp.concatenate([vb, jnp.ones_like(vb)], axis=1)
            pv = jnp.dot(p.astype(BF16), v_ones, preferred_element_type=F32)
            den = pv[:, HEAD_DIM:]
            q_start = pl.multiple_of(n * blk, blk)
            if d == 1:
                rows = pl.ds(q_start, blk)
            elif d == 4:
                rows = pl.ds(r * (seq_len // 4) + q_start, blk)
            else:
                r4, m4 = r % 4, r // 4
                rows = pl.ds(r4 * (seq_len // 4) + 4 * q_start + m4, blk, stride=4)
            o_scr[g, rows, :] = pv[:, :HEAD_DIM] / den
            lse_scr[g, rows, :] = m + jnp.log2(den)

        def run_blocks(tasks, scores=scores, finish=finish):
            nxt = scores(*tasks[0])
            for idx, (r, n) in enumerate(tasks):
                cur = nxt
                if idx + 1 < len(tasks):
                    nxt = scores(*tasks[idx + 1])
                finish(r, n, cur)

        n_per_iter = max(1, min(blocks_per_iter // d, n_blk))

        def body(it, carry, run_blocks=run_blocks, d=d, n_per_iter=n_per_iter):
            run_blocks([(r, it * n_per_iter + u) for u in range(n_per_iter) for r in range(d)])
            return carry

        lax.fori_loop(0, n_blk // n_per_iter, body, 0)

    def combine(c, carry):
        i0 = pl.multiple_of(c * combine_rows, combine_rows)
        for r4 in range(4):
            tok = pl.ds(4 * i0 + r4, combine_rows, stride=4)
            cls = pl.ds(r4 * (seq_len // 4) + i0, combine_rows)
            l1, l2, l3 = lse_scr[0, tok, :], lse_scr[1, cls, :], lse_scr[2, cls, :]
            lmax = jnp.maximum(jnp.maximum(l1, l2), l3)
            e1, e2, e3 = jnp.exp2(l1 - lmax), jnp.exp2(l2 - lmax), jnp.exp2(l3 - lmax)
            num = e1 * o_scr[0, tok, :] + e2 * o_scr[1, cls, :] + e3 * o_scr[2, cls, :]
            ob_ref[0, 0, tok, :] = num / (e1 + e2 + e3)
        return carry

    lax.fori_loop(0, seq_len // 4 // combine_rows, combine, 0)


def _dilated_call(qkv_heads, d4, d16, *, n_heads_b, q_off, k_off, v_off, blocks_per_iter=16):
    B, _, S, _ = qkv_heads.shape
    offs_nat = (q_off, k_off, v_off)
    offs_dil = (0, n_heads_b, 2 * n_heads_b)
    in_arrays = [qkv_heads] * 3 + [d4] * 3 + [d16] * 3
    in_specs = (
        [pl.BlockSpec((1, 1, S, HEAD_DIM), lambda b, h, off=off: (b, off + h, 0, 0))
         for off in offs_nat]
        + [pl.BlockSpec((1, 1, 4, S // 4, HEAD_DIM), lambda b, h, off=off: (b, off + h, 0, 0, 0))
           for off in offs_dil]
        + [pl.BlockSpec((1, 1, 16, S // 16, HEAD_DIM), lambda b, h, off=off: (b, off + h, 0, 0, 0))
           for off in offs_dil])
    n_br = len(DIL_PAIRS)
    return pl.pallas_call(
        functools.partial(_dilated_kernel, seq_len=S, blocks_per_iter=blocks_per_iter,
                          combine_rows=64),
        grid=(B, n_heads_b),
        in_specs=in_specs,
        out_specs=pl.BlockSpec((1, 1, S, HEAD_DIM), lambda b, h: (b, h, 0, 0)),
        out_shape=jax.ShapeDtypeStruct((B, n_heads_b, S, HEAD_DIM), F32),
        scratch_shapes=[pltpu.VMEM((n_br, S, HEAD_DIM), F32),
                        pltpu.VMEM((n_br, S, HEAD_DIM), F32),
                        pltpu.VMEM((2, DIL_BLOCK, 2 * DIL_BLOCK), F32)],
        compiler_params=pltpu.CompilerParams(
            dimension_semantics=("parallel", "parallel"),
            vmem_limit_bytes=V7X_VMEM_LIMIT_BYTES),
        name="dilated_attn",
    )(*in_arrays)


def _attnout_kernel(x_ref, oa_ref, ob_ref, ga_ref, gb_ref, wo_ref, pg_ref, fg_ref,
                    out_ref, hn_ref, mix_ref, *, row_chunks):
    tm = x_ref.shape[1]
    rows_per_chunk = tm // row_chunks
    for c in range(row_chunks):
        rows = slice(c * rows_per_chunk, (c + 1) * rows_per_chunk)
        col = 0
        for o_ref, g_ref in ((oa_ref, ga_ref), (ob_ref, gb_ref)):
            n_heads = o_ref.shape[1]
            ssq = None
            for h in range(n_heads):
                t = o_ref[0, h, rows, :]
                part = jnp.sum(t * t, axis=-1, keepdims=True)
                ssq = part if ssq is None else ssq + part
            inv = lax.rsqrt(ssq / (n_heads * HEAD_DIM) + RMS_EPS)
            for h in range(n_heads):
                gs = slice(h * HEAD_DIM, (h + 1) * HEAD_DIM)
                mix_ref[rows, col:col + HEAD_DIM] = (
                    o_ref[0, h, rows, :] * inv * g_ref[:, gs]).astype(BF16)
                col += HEAD_DIM
        y = jnp.dot(mix_ref[rows, :], wo_ref[...], preferred_element_type=F32)
        x_new = x_ref[0, rows, :] + y * _rms_scale(y) * pg_ref[...]
        out_ref[0, rows, :] = x_new
        hn_ref[0, rows, :] = (x_new * _rms_scale(x_new) * fg_ref[...]).astype(BF16)


def _attnout_call(x, oa, ob, ga, gb, wo_bf16, pg, ffn_g, *, tm=512, row_chunks=2):
    B, S, D = x.shape
    head_spec_a = pl.BlockSpec((1, oa.shape[1], tm, HEAD_DIM), lambda b, i: (b, 0, i, 0))
    head_spec_b = pl.BlockSpec((1, ob.shape[1], tm, HEAD_DIM), lambda b, i: (b, 0, i, 0))
    row_spec = pl.BlockSpec((1, tm, D), lambda b, i: (b, i, 0))
    return pl.pallas_call(
        functools.partial(_attnout_kernel, row_chunks=row_chunks),
        grid=(B, S // tm),
        in_specs=[
            row_spec,
            head_spec_a, head_spec_b,
            pl.BlockSpec((1, ga.shape[1]), lambda b, i: (0, 0)),
            pl.BlockSpec((1, gb.shape[1]), lambda b, i: (0, 0)),
            pl.BlockSpec(wo_bf16.shape, lambda b, i: (0, 0)),
            pl.BlockSpec((1, D), lambda b, i: (0, 0)),
            pl.BlockSpec((1, D), lambda b, i: (0, 0)),
        ],
        out_specs=[row_spec, row_spec],
        out_shape=[jax.ShapeDtypeStruct((B, S, D), F32), jax.ShapeDtypeStruct((B, S, D), BF16)],
        scratch_shapes=[pltpu.VMEM((tm, wo_bf16.shape[0]), BF16)],
        compiler_params=pltpu.CompilerParams(
            dimension_semantics=("parallel", "parallel"),
            vmem_limit_bytes=V7X_VMEM_LIMIT_BYTES),
        name="attn_out",
    )(x, oa, ob, ga, gb, wo_bf16, pg, ffn_g)


def _ffn_kernel(x_ref, hn_ref, wg_ref, wu_ref, cw_ref, cb_ref, wd_ref, pg_ref,
                out_ref, acc_ref, tail_ref, act_ref):
    i = pl.program_id(1)
    f = pl.program_id(2)
    tm = x_ref.shape[1]
    halo = tail_ref.shape[1]

    @pl.when(i == 0)
    def _():
        tail_ref[f] = jnp.zeros(tail_ref.shape[1:], F32)

    @pl.when(f == 0)
    def _():
        acc_ref[...] = jnp.zeros_like(acc_ref)

    gate = jnp.dot(hn_ref[0], wg_ref[...], preferred_element_type=F32)
    up = jnp.dot(hn_ref[0], wu_ref[...], preferred_element_type=F32)
    cw = cw_ref[...]
    cb = cb_ref[...]

    def gated(g_m2, g_m1, g_0, u):
        gc = cb + g_m2 * cw[0:1, :] + g_m1 * cw[1:2, :] + g_0 * cw[2:3, :]
        gelu = 0.5 * gc * (1.0 + jnp.tanh(0.7978845608028654 * (gc + 0.044715 * (gc * gc * gc))))
        return (gelu * u).astype(BF16)

    act_ref[...] = gated(pltpu.roll(gate, 2, 0), pltpu.roll(gate, 1, 0), gate, up)
    head = BF16_SUBLANE_TILE
    ext = jnp.concatenate([tail_ref[f], gate[0:head]], axis=0)
    act_ref[0:head, :] = gated(ext[halo - 2:halo - 2 + head], ext[halo - 1:halo - 1 + head],
                               gate[0:head], up[0:head])
    tail_ref[f] = gate[tm - halo:, :]
    acc_ref[...] += jnp.dot(act_ref[...], wd_ref[...], preferred_element_type=F32)

    @pl.when(f == pl.num_programs(2) - 1)
    def _():
        y = acc_ref[...]
        out_ref[0] = x_ref[0] + y * _rms_scale(y) * pg_ref[...]


def _ffn_call(x, hn, wg_bf16, wu_bf16, conv_w, conv_b, wd_bf16, pg, *, tm=512, tf=1024):
    B, S, D = x.shape
    d_ff = wg_bf16.shape[1]
    halo = F32_SUBLANE_TILE
    assert CONV_WIDTH - 1 <= halo
    return pl.pallas_call(
        _ffn_kernel,
        grid=(B, S // tm, d_ff // tf),
        in_specs=[
            pl.BlockSpec((1, tm, D), lambda b, i, f: (b, i, 0)),
            pl.BlockSpec((1, tm, D), lambda b, i, f: (b, i, 0)),
            pl.BlockSpec((D, tf), lambda b, i, f: (0, f)),
            pl.BlockSpec((D, tf), lambda b, i, f: (0, f)),
            pl.BlockSpec((CONV_WIDTH, tf), lambda b, i, f: (0, f)),
            pl.BlockSpec((1, tf), lambda b, i, f: (0, f)),
            pl.BlockSpec((tf, D), lambda b, i, f: (f, 0)),
            pl.BlockSpec((1, D), lambda b, i, f: (0, 0)),
        ],
        out_specs=pl.BlockSpec((1, tm, D), lambda b, i, f: (b, i, 0)),
        out_shape=jax.ShapeDtypeStruct((B, S, D), F32),
        scratch_shapes=[pltpu.VMEM((tm, D), F32), pltpu.VMEM((d_ff // tf, halo, tf), F32),
                        pltpu.VMEM((tm, tf), BF16)],
        compiler_params=pltpu.CompilerParams(
            dimension_semantics=("parallel", "arbitrary", "arbitrary"),
            vmem_limit_bytes=V7X_VMEM_LIMIT_FFN_BYTES),
        name="conv_glu_ffn",
    )(x, hn, wg_bf16, wu_bf16, conv_w, conv_b, wd_bf16, pg)


def _rope_tables(positions):
    inv_freq = ROPE_THETA ** (-jnp.arange(ROT_HALF, dtype=F32) / ROT_HALF)
    ang = positions.astype(F32)[..., None] * inv_freq
    cos, sin = jnp.cos(ang), jnp.sin(ang)
    ones = jnp.ones(ang.shape[:-1] + (HEAD_DIM - ROT_DIM,), F32)
    zeros_tail = jnp.zeros(ang.shape[:-1] + (HEAD_DIM - ROT_HALF,), F32)
    zeros_head = jnp.zeros_like(sin)
    cos_t = jnp.concatenate([cos, cos, ones], axis=-1)
    sa_t = jnp.concatenate([-sin, zeros_tail], axis=-1)
    sb_t = jnp.concatenate([zeros_head, sin, zeros_tail[..., :HEAD_DIM - ROT_DIM]], axis=-1)
    return cos_t, sa_t, sb_t


def kernel(x, positions, attn_pre_g, w_qkv, moba_out_g, dil_out_g, w_o, attn_post_g, ffn_pre_g,
           w_gate, w_up, conv_w, conv_b, w_down, ffn_post_g):
    depth = w_qkv.shape[0]
    n_heads_a = moba_out_g.shape[1] // HEAD_DIM
    n_heads_b = dil_out_g.shape[1] // HEAD_DIM
    cos_t, sa_t, sb_t = _rope_tables(positions)
    for l in range(depth):
        qkv_heads, vt, d4, d16 = _qkv_call(x, attn_pre_g[l][None], w_qkv[l].astype(BF16),
                                           cos_t, sa_t, sb_t,
                                           n_heads_a=n_heads_a, n_heads_b=n_heads_b)
        oa, (wo_bf16, wg_bf16, wu_bf16, wd_bf16) = _moba_call(
            qkv_heads, vt, [w_o[l], w_gate[l], w_up[l], w_down[l]], n_heads_a=n_heads_a)
        ob = _dilated_call(qkv_heads, d4, d16, n_heads_b=n_heads_b, q_off=3 * n_heads_a,
                           k_off=3 * n_heads_a + n_heads_b, v_off=3 * n_heads_a + 2 * n_heads_b)
        x, hn = _attnout_call(x, oa, ob, moba_out_g[l][None], dil_out_g[l][None],
                              wo_bf16, attn_post_g[l][None], ffn_pre_g[l][None])
        x = _ffn_call(x, hn, wg_bf16, wu_bf16, conv_w[l], conv_b[l][None], wd_bf16,
                      ffn_post_g[l][None])
    return x
```

```python
import functools

import jax
import jax.numpy as jnp
from jax import lax
from jax.experimental import pallas as pl
from jax.experimental.pallas import tpu as pltpu

F32 = jnp.float32
BF16 = jnp.bfloat16

HEAD_DIM = 128
ROT_DIM = HEAD_DIM // 4
ROT_HALF = ROT_DIM // 2
ROPE_THETA = 500000.0
MOBA_BLOCK = 256
MOBA_TOPK = 3
DIL_PAIRS = ((128, 1), (512, 4), (2048, 16))
DIL_BLOCK = 128
CONV_WIDTH = 3
RMS_EPS = 1e-6
SCALE = HEAD_DIM ** -0.5
SCALE_LOG2E = SCALE * 1.4426950408889634
NEG = -1e30

V7X_VMEM_LIMIT_BYTES = 56 * 1024 * 1024
BF16_SUBLANE_TILE = 16
V7X_MXU_WIDTH = 256
F32_SUBLANE_TILE = 8
QKV_SLAB_SLOTS = 2
MOBA_VT_ROWS = HEAD_DIM + BF16_SUBLANE_TILE

NT_DIMS = (((1,), (1,)), ((), ()))


def _rms_scale(x):
    return lax.rsqrt(jnp.mean(x * x, axis=-1, keepdims=True) + RMS_EPS)


def _qkv_kernel(x_ref, g_ref, w_ref, cos_ref, sa_ref, sb_ref,
                nat_ref, vt_ref, d4_ref, d16_ref, hn_ref, slab_ref, slab4_ref, *, heads_per_tile):
    j = pl.program_id(2)
    tm = hn_ref.shape[0]
    heads_per_dot = V7X_MXU_WIDTH // HEAD_DIM
    dot_width = heads_per_dot * HEAD_DIM

    @pl.when(j == 0)
    def _():
        x = x_ref[0]
        hn_ref[...] = (x * _rms_scale(x) * g_ref[...]).astype(BF16)

    def run(with_rope, with_vt, with_dilated):
        if with_rope:
            q_scale = jnp.where(jnp.logical_or(j == 0, j == 3), SCALE_LOG2E, 1.0).astype(F32)
            cos, sa, sb = cos_ref[0] * q_scale, sa_ref[0] * q_scale, sb_ref[0] * q_scale

            def rope(t):
                return (t * cos + pltpu.roll(t, HEAD_DIM - ROT_HALF, 1) * sa
                        + pltpu.roll(t, ROT_HALF, 1) * sb)

        for c in range(heads_per_tile // heads_per_dot):
            acc = jnp.dot(hn_ref[...], w_ref[:, c * dot_width:(c + 1) * dot_width],
                          preferred_element_type=F32)
            for hh in range(heads_per_dot):
                h = c * heads_per_dot + hh
                t = acc[:, hh * HEAD_DIM:(hh + 1) * HEAD_DIM]
                if with_rope:
                    t = rope(t)
                nat_ref[0, h] = t.astype(BF16)
                if with_vt:
                    for blk in range(tm // MOBA_BLOCK):
                        vt_ref[0, h, blk, 0:HEAD_DIM, :] = (
                            t[blk * MOBA_BLOCK:(blk + 1) * MOBA_BLOCK].T.astype(BF16))
                        vt_ref[0, h, blk, HEAD_DIM:, :] = jnp.ones(
                            (MOBA_VT_ROWS - HEAD_DIM, MOBA_BLOCK), BF16)
                if with_dilated:
                    slot = h % slab_ref.shape[0]
                    q4 = tm // 4
                    slab_ref[slot] = t
                    for r4 in range(4):
                        cls = slab_ref[slot, pl.ds(r4, q4, stride=4), :]
                        d4_ref[0, h, r4] = cls.astype(BF16)
                        slab4_ref[slot, r4 * q4:(r4 + 1) * q4, :] = cls
                    for r4 in range(4):
                        for m in range(4):
                            cls = slab4_ref[slot, pl.ds(r4 * q4 + m, q4 // 4, stride=4), :]
                            d16_ref[0, h, r4 + 4 * m] = cls.astype(BF16)

    pl.when(j < 2)(lambda: run(True, False, False))
    pl.when(j == 2)(lambda: run(False, True, False))
    pl.when(jnp.logical_or(j == 3, j == 4))(lambda: run(True, False, True))
    pl.when(j == 5)(lambda: run(False, False, True))


def _qkv_call(x, g, w_bf16, cos_t, sa_t, sb_t, *, n_heads_a, n_heads_b, tm=1024):
    B, S, D = x.shape
    N = w_bf16.shape[1]
    assert n_heads_a == n_heads_b and N == 3 * (n_heads_a + n_heads_b) * HEAD_DIM
    hpt = n_heads_a
    tn = hpt * HEAD_DIM
    first_b_tile = 3
    kern = functools.partial(_qkv_kernel, heads_per_tile=hpt)

    def dil_index(b, i, j):
        return (b, jnp.maximum(j - first_b_tile, 0), 0, i, 0)

    return pl.pallas_call(
        kern,
        grid=(B, S // tm, N // tn),
        in_specs=[
            pl.BlockSpec((1, tm, D), lambda b, i, j: (b, i, 0)),
            pl.BlockSpec((1, D), lambda b, i, j: (0, 0)),
            pl.BlockSpec((D, tn), lambda b, i, j: (0, j)),
            pl.BlockSpec((1, tm, HEAD_DIM), lambda b, i, j: (b, i, 0)),
            pl.BlockSpec((1, tm, HEAD_DIM), lambda b, i, j: (b, i, 0)),
            pl.BlockSpec((1, tm, HEAD_DIM), lambda b, i, j: (b, i, 0)),
        ],
        out_specs=[
            pl.BlockSpec((1, hpt, tm, HEAD_DIM), lambda b, i, j: (b, j, i, 0)),
            pl.BlockSpec((1, hpt, tm // MOBA_BLOCK, MOBA_VT_ROWS, MOBA_BLOCK),
                         lambda b, i, j: (b, 0, i, 0, 0)),
            pl.BlockSpec((1, hpt, 4, tm // 4, HEAD_DIM), dil_index),
            pl.BlockSpec((1, hpt, 16, tm // 16, HEAD_DIM), dil_index),
        ],
        out_shape=[
            jax.ShapeDtypeStruct((B, N // HEAD_DIM, S, HEAD_DIM), BF16),
            jax.ShapeDtypeStruct((B, n_heads_a, S // MOBA_BLOCK, MOBA_VT_ROWS, MOBA_BLOCK), BF16),
            jax.ShapeDtypeStruct((B, 3 * n_heads_b, 4, S // 4, HEAD_DIM), BF16),
            jax.ShapeDtypeStruct((B, 3 * n_heads_b, 16, S // 16, HEAD_DIM), BF16),
        ],
        scratch_shapes=[pltpu.VMEM((tm, D), BF16),
                        pltpu.VMEM((QKV_SLAB_SLOTS, tm, HEAD_DIM), F32),
                        pltpu.VMEM((QKV_SLAB_SLOTS, tm, HEAD_DIM), F32)],
        compiler_params=pltpu.CompilerParams(
            dimension_semantics=("parallel", "parallel", "arbitrary"),
            vmem_limit_bytes=V7X_VMEM_LIMIT_BYTES),
        name="qkv_rope",
    )(x, g, w_bf16, cos_t, sa_t, sb_t)


def _moba_kernel(*refs, n_blocks, heads, n_cast, lookahead=8, group=2):
    q_ref, k_ref, vt_ref = refs[:3]
    cast_in = refs[3:3 + n_cast]
    o_ref = refs[3 + n_cast]
    cast_out = refs[4 + n_cast:4 + 2 * n_cast]
    kmean_ref, bias_ref, acc_ref = refs[4 + 2 * n_cast:]
    assert n_blocks % group == 0
    i = pl.program_id(2)
    blk_sz = MOBA_BLOCK

    for w_in, w_out in zip(cast_in, cast_out):
        w_out[0] = w_in[0].astype(BF16)

    @pl.when(i == 0)
    def _():
        for h in range(heads):
            for blk in range(n_blocks):
                kb = k_ref[0, h, blk * blk_sz:(blk + 1) * blk_sz, :].astype(F32)
                kmean_ref[h, blk:blk + 1, :] = jnp.mean(kb, axis=0, keepdims=True)

    def select_blocks(h, q):
        km = kmean_ref[h]
        km_hi = km.astype(BF16)
        km_lo = (km - km_hi.astype(F32)).astype(BF16)
        gate = (lax.dot_general(km_hi, q, NT_DIMS, preferred_element_type=F32)
                + lax.dot_general(km_lo, q, NT_DIMS, preferred_element_type=F32))
        blk_id = lax.broadcasted_iota(jnp.int32, gate.shape, 0).astype(F32)
        neg_inf = jnp.float32(-jnp.inf)
        g = jnp.where(blk_id < i.astype(F32), gate, neg_inf)
        sel = jnp.zeros(gate.shape, dtype=jnp.bool_)
        for _ in range(MOBA_TOPK):
            m = jnp.max(g, axis=0, keepdims=True)
            first = jnp.min(jnp.where(g == m, blk_id, float(n_blocks)), axis=0, keepdims=True)
            pick = jnp.logical_and(blk_id == first, m > neg_inf)
            sel = jnp.logical_or(sel, pick)
            g = jnp.where(pick, neg_inf, g)
        bias_ref[h] = jnp.where(sel, 0.0, NEG).astype(F32)

    def scores(h, first_blk, n_blk):
        rows = n_blk * blk_sz
        kb = k_ref[0, h, pl.ds(pl.multiple_of(first_blk * blk_sz, blk_sz), rows), :]
        return lax.dot_general(kb, q_ref[0, h], NT_DIMS, preferred_element_type=F32)

    def pipelined(stage_a, stage_b):
        ahead = [stage_a(h) for h in range(min(lookahead, heads))]
        outs = []
        for h in range(heads):
            if h + lookahead < heads:
                ahead.append(stage_a(h + lookahead))
            outs.append(stage_b(h, ahead[h]))
        return outs

    def own_a(h):
        select_blocks(h, q_ref[0, h])
        return scores(h, i, 1)

    def own_b(h, s):
        key_pos = lax.broadcasted_iota(jnp.int32, s.shape, 0)
        q_pos = lax.broadcasted_iota(jnp.int32, s.shape, 1)
        t = jnp.where(key_pos <= q_pos, s, NEG)
        m0 = jnp.max(t, axis=0, keepdims=True)
        p = jnp.exp2(t - m0)
        acc_ref[h] = jnp.dot(vt_ref[0, h, i], p.astype(BF16), preferred_element_type=F32)
        return m0

    init = pipelined(own_a, own_b)

    def body(c, carry):
        first = c * group

        def past_b(h, s):
            m_prev = carry[h]
            chunks = [s[g * blk_sz:(g + 1) * blk_sz] for g in range(group)]
            brows = [bias_ref[h, pl.ds(first + g, 1), :] for g in range(group)]
            m_new = m_prev
            for sg, brow in zip(chunks, brows):
                m_new = jnp.maximum(m_new, jnp.max(sg, axis=0, keepdims=True) + brow)
            acc_new = jnp.exp2(m_prev - m_new) * acc_ref[h]
            for g, (sg, brow) in enumerate(zip(chunks, brows)):
                pg = jnp.exp2(sg - (m_new - brow))
                acc_new = acc_new + jnp.dot(vt_ref[0, h, first + g], pg.astype(BF16),
                                            preferred_element_type=F32)
            acc_ref[h] = acc_new
            return m_new

        return tuple(pipelined(lambda h: scores(h, first, group), past_b))

    lax.fori_loop(0, (i + group - 1) // group, body, tuple(init))
    for h in range(heads):
        acc_fin = acc_ref[h]
        o_ref[0, h] = (acc_fin[:HEAD_DIM] / acc_fin[HEAD_DIM:HEAD_DIM + 1]).T


def _moba_call(qkv_heads, vt, f32_weights, *, n_heads_a, heads_per_step=8):
    B, _, S, _ = qkv_heads.shape
    n_blocks = S // MOBA_BLOCK
    tq = MOBA_BLOCK
    hps = heads_per_step
    assert n_heads_a % hps == 0
    n_groups = n_heads_a // hps
    k_first = n_groups
    grid = (B, n_groups, S // tq)
    n_steps = grid[0] * grid[1] * grid[2]

    def slab_index(b, h, i):
        return ((b * grid[1] + h) * grid[2] + i, 0, 0)

    slabs, slab_specs, slab_shapes = [], [], []
    for w in f32_weights:
        rows, cols = w.shape
        assert rows % (n_steps * BF16_SUBLANE_TILE) == 0
        slabs.append(w.reshape(n_steps, rows // n_steps, cols))
        slab_specs.append(pl.BlockSpec((1, rows // n_steps, cols), slab_index))
        slab_shapes.append(jax.ShapeDtypeStruct((n_steps, rows // n_steps, cols), BF16))

    kern = functools.partial(_moba_kernel, n_blocks=n_blocks, heads=hps, n_cast=len(slabs))
    outs = pl.pallas_call(
        kern,
        grid=grid,
        in_specs=[
            pl.BlockSpec((1, hps, tq, HEAD_DIM), lambda b, h, i: (b, h, i, 0)),
            pl.BlockSpec((1, hps, S, HEAD_DIM), lambda b, h, i: (b, k_first + h, 0, 0)),
            pl.BlockSpec((1, hps, n_blocks, MOBA_VT_ROWS, MOBA_BLOCK),
                         lambda b, h, i: (b, h, 0, 0, 0)),
        ] + slab_specs,
        out_specs=[pl.BlockSpec((1, hps, tq, HEAD_DIM), lambda b, h, i: (b, h, i, 0))] + slab_specs,
        out_shape=[jax.ShapeDtypeStruct((B, n_heads_a, S, HEAD_DIM), F32)] + slab_shapes,
        scratch_shapes=[pltpu.VMEM((hps, n_blocks, HEAD_DIM), F32),
                        pltpu.VMEM((hps, n_blocks, tq), F32),
                        pltpu.VMEM((hps, MOBA_VT_ROWS, tq), F32)],
        compiler_params=pltpu.CompilerParams(
            dimension_semantics=("parallel", "parallel", "arbitrary"),
            vmem_limit_bytes=V7X_VMEM_LIMIT_BYTES),
        name="moba_attn",
    )(qkv_heads, qkv_heads, vt, *slabs)
    return outs[0], [o.reshape(w.shape) for o, w in zip(outs[1:], f32_weights)]


def _dilated_kernel(q1_ref, k1_ref, v1_ref, q4_ref, k4_ref, v4_ref, q16_ref, k16_ref, v16_ref,
                    ob_ref, o_scr, lse_scr, bias_scr, *, seq_len, blocks_per_iter, combine_rows):
    blk = DIL_BLOCK
    branch_refs = ((q1_ref, k1_ref, v1_ref), (q4_ref, k4_ref, v4_ref), (q16_ref, k16_ref, v16_ref))

    qi = lax.broadcasted_iota(jnp.int32, (blk, 2 * blk), 0)
    ki = lax.broadcasted_iota(jnp.int32, (blk, 2 * blk), 1)
    dist = qi + blk - ki
    bias_scr[0] = jnp.where(jnp.logical_and(dist >= 0, dist <= blk), 0.0, NEG).astype(F32)
    bias_scr[1] = jnp.where(ki <= qi, 0.0, NEG).astype(F32)

    for g, (window, d) in enumerate(DIL_PAIRS):
        assert window // d == DIL_BLOCK
        q_ref, k_ref, v_ref = branch_refs[g]
        n_blk = seq_len // d // blk

        def rows_of(ref, r, start, size, d=d):
            if d == 1:
                return ref[0, 0, pl.ds(start, size), :]
            return ref[0, 0, r, pl.ds(start, size), :]

        def key_start(n):
            return pl.multiple_of(jnp.maximum(n - 1, 0) * blk, blk)

        def scores(r, n, q_ref=q_ref, k_ref=k_ref, rows_of=rows_of):
            qb = rows_of(q_ref, r, pl.multiple_of(n * blk, blk), blk)
            kb = rows_of(k_ref, r, key_start(n), 2 * blk)
            return lax.dot_general(qb, kb, NT_DIMS, preferred_element_type=F32)

        def finish(r, n, s, g=g, d=d, v_ref=v_ref, rows_of=rows_of):
            t = s + bias_scr[jnp.where(n == 0, 1, 0)]
            m = jnp.max(t, axis=-1, keepdims=True)
            p = jnp.exp2(t - m)
            vb = rows_of(v_ref, r, key_start(n), 2 * blk)
            v_ones = jnp.concatenate([vb, jnp.ones_like(vb)], axis=1)
            pv = jnp.dot(p.astype(BF16), v_ones, preferred_element_type=F32)
            den = pv[:, HEAD_DIM:]
            q_start = pl.multiple_of(n * blk, blk)
            if d == 1:
                rows = pl.ds(q_start, blk)
            elif d == 4:
                rows = pl.ds(r * (seq_len // 4) + q_start, blk)
            else:
                r4, m4 = r % 4, r // 4
                rows = pl.ds(r4 * (seq_len // 4) + 4 * q_start + m4, blk, stride=4)
            o_scr[g, rows, :] = pv[:, :HEAD_DIM] / den
            lse_scr[g, rows, :] = m + jnp.log2(den)

        def run_blocks(tasks, scores=scores, finish=finish):
            nxt = scores(*tasks[0])
            for idx, (r, n) in enumerate(tasks):
                cur = nxt
                if idx + 1 < len(tasks):
                    nxt = scores(*tasks[idx + 1])
                finish(r, n, cur)

        n_per_iter = max(1, min(blocks_per_iter // d, n_blk))

        def body(it, carry, run_blocks=run_blocks, d=d, n_per_iter=n_per_iter):
            run_blocks([(r, it * n_per_iter + u) for u in range(n_per_iter) for r in range(d)])
            return carry

        lax.fori_loop(0, n_blk // n_per_iter, body, 0)

    def combine(c, carry):
        i0 = pl.multiple_of(c * combine_rows, combine_rows)
        for r4 in range(4):
            tok = pl.ds(4 * i0 + r4, combine_rows, stride=4)
            cls = pl.ds(r4 * (seq_len // 4) + i0, combine_rows)
            l1, l2, l3 = lse_scr[0, tok, :], lse_scr[1, cls, :], lse_scr[2, cls, :]
            lmax = jnp.maximum(jnp.maximum(l1, l2), l3)
            e1, e2, e3 = jnp.exp2(l1 - lmax), jnp.exp2(l2 - lmax), jnp.exp2(l3 - lmax)
            num = e1 * o_scr[0, tok, :] + e2 * o_scr[1, cls, :] + e3 * o_scr[2, cls, :]
            ob_ref[0, 0, tok, :] = num / (e1 + e2 + e3)
        return carry

    lax.fori_loop(0, seq_len // 4 // combine_rows, combine, 0)


def _dilated_call(qkv_heads, d4, d16, *, n_heads_b, q_off, k_off, v_off, blocks_per_iter=16):
    B, _, S, _ = qkv_heads.shape
    offs_nat = (q_off, k_off, v_off)
    offs_dil = (0, n_heads_b, 2 * n_heads_b)
    in_arrays = [qkv_heads] * 3 + [d4] * 3 + [d16] * 3
    in_specs = (
        [pl.BlockSpec((1, 1, S, HEAD_DIM), lambda b, h, off=off: (b, off + h, 0, 0))
         for off in offs_nat]
        + [pl.BlockSpec((1, 1, 4, S // 4, HEAD_DIM), lambda b, h, off=off: (b, off + h, 0, 0, 0))
           for off in offs_dil]
        + [pl.BlockSpec((1, 1, 16, S // 16, HEAD_DIM), lambda b, h, off=off: (b, off + h, 0, 0, 0))
           for off in offs_dil])
    n_br = len(DIL_PAIRS)
    return pl.pallas_call(
        functools.partial(_dilated_kernel, seq_len=S, blocks_per_iter=blocks_per_iter,
                          combine_rows=64),
        grid=(B, n_heads_b),
        in_specs=in_specs,
        out_specs=pl.BlockSpec((1, 1, S, HEAD_DIM), lambda b, h: (b, h, 0, 0)),
        out_shape=jax.ShapeDtypeStruct((B, n_heads_b, S, HEAD_DIM), F32),
        scratch_shapes=[pltpu.VMEM((n_br, S, HEAD_DIM), F32),
                        pltpu.VMEM((n_br, S, HEAD_DIM), F32),
                        pltpu.VMEM((2, DIL_BLOCK, 2 * DIL_BLOCK), F32)],
        compiler_params=pltpu.CompilerParams(
            dimension_semantics=("parallel", "parallel"),
            vmem_limit_bytes=V7X_VMEM_LIMIT_BYTES),
        name="dilated_attn",
    )(*in_arrays)


def _attnout_kernel(x_ref, oa_ref, ob_ref, ga_ref, gb_ref, wo_ref, pg_ref, fg_ref,
                    out_ref, hn_ref, mix_ref, *, row_chunks):
    tm = x_ref.shape[1]
    rows_per_chunk = tm // row_chunks
    for c in range(row_chunks):
        rows = slice(c * rows_per_chunk, (c + 1) * rows_per_chunk)
        col = 0
        for o_ref, g_ref in ((oa_ref, ga_ref), (ob_ref, gb_ref)):
            n_heads = o_ref.shape[1]
            ssq = None
            for h in range(n_heads):
                t = o_ref[0, h, rows, :]
                part = jnp.sum(t * t, axis=-1, keepdims=True)
                ssq = part if ssq is None else ssq + part
            inv = lax.rsqrt(ssq / (n_heads * HEAD_DIM) + RMS_EPS)
            for h in range(n_heads):
                gs = slice(h * HEAD_DIM, (h + 1) * HEAD_DIM)
                mix_ref[rows, col:col + HEAD_DIM] = (
                    o_ref[0, h, rows, :] * inv * g_ref[:, gs]).astype(BF16)
                col += HEAD_DIM
        y = jnp.dot(mix_ref[rows, :], wo_ref[...], preferred_element_type=F32)
        x_new = x_ref[0, rows, :] + y * _rms_scale(y) * pg_ref[...]
        out_ref[0, rows, :] = x_new
        hn_ref[0, rows, :] = (x_new * _rms_scale(x_new) * fg_ref[...]).astype(BF16)


def _attnout_call(x, oa, ob, ga, gb, wo_bf16, pg, ffn_g, *, tm=512, row_chunks=2):
    B, S, D = x.shape
    head_spec_a = pl.BlockSpec((1, oa.shape[1], tm, HEAD_DIM), lambda b, i: (b, 0, i, 0))
    head_spec_b = pl.BlockSpec((1, ob.shape[1], tm, HEAD_DIM), lambda b, i: (b, 0, i, 0))
    row_spec = pl.BlockSpec((1, tm, D), lambda b, i: (b, i, 0))
    return pl.pallas_call(
        functools.partial(_attnout_kernel, row_chunks=row_chunks),
        grid=(B, S // tm),
        in_specs=[
            row_spec,
            head_spec_a, head_spec_b,
            pl.BlockSpec((1, ga.shape[1]), lambda b, i: (0, 0)),
            pl.BlockSpec((1, gb.shape[1]), lambda b, i: (0, 0)),
            pl.BlockSpec(wo_bf16.shape, lambda b, i: (0, 0)),
            pl.BlockSpec((1, D), lambda b, i: (0, 0)),
            pl.BlockSpec((1, D), lambda b, i: (0, 0)),
        ],
        out_specs=[row_spec, row_spec],
        out_shape=[jax.ShapeDtypeStruct((B, S, D), F32), jax.ShapeDtypeStruct((B, S, D), BF16)],
        scratch_shapes=[pltpu.VMEM((tm, wo_bf16.shape[0]), BF16)],
        compiler_params=pltpu.CompilerParams(
            dimension_semantics=("parallel", "parallel"),
            vmem_limit_bytes=V7X_VMEM_LIMIT_BYTES),
        name="attn_out",
    )(x, oa, ob, ga, gb, wo_bf16, pg, ffn_g)


def _ffn_kernel(x_ref, hn_ref, wg_ref, wu_ref, cw_ref, cb_ref, wd_ref, pg_ref,
                out_ref, tail_ref, act_ref):
    i = pl.program_id(1)
    f = pl.program_id(2)
    tm = x_ref.shape[1]
    halo = tail_ref.shape[1]

    @pl.when(i == 0)
    def _():
        tail_ref[f] = jnp.zeros(tail_ref.shape[1:], F32)

    @pl.when(f == 0)
    def _():
        out_ref[0] = jnp.zeros(out_ref.shape[1:], F32)

    gate = jnp.dot(hn_ref[0], wg_ref[...], preferred_element_type=F32)
    up = jnp.dot(hn_ref[0], wu_ref[...], preferred_element_type=F32)
    cw = cw_ref[...]
    cb = cb_ref[...]

    def gated(g_m2, g_m1, g_0, u):
        gc = cb + g_m2 * cw[0:1, :] + g_m1 * cw[1:2, :] + g_0 * cw[2:3, :]
        gelu = 0.5 * gc * (1.0 + jnp.tanh(0.7978845608028654 * (gc + 0.044715 * (gc * gc * gc))))
        return (gelu * u).astype(BF16)

    act_ref[...] = gated(pltpu.roll(gate, 2, 0), pltpu.roll(gate, 1, 0), gate, up)
    head = BF16_SUBLANE_TILE
    ext = jnp.concatenate([tail_ref[f], gate[0:head]], axis=0)
    act_ref[0:head, :] = gated(ext[halo - 2:halo - 2 + head], ext[halo - 1:halo - 1 + head],
                               gate[0:head], up[0:head])
    tail_ref[f] = gate[tm - halo:, :]
    out_ref[0] += jnp.dot(act_ref[...], wd_ref[...], preferred_element_type=F32)

    @pl.when(f == pl.num_programs(2) - 1)
    def _():
        y = out_ref[0]
        out_ref[0] = x_ref[0] + y * _rms_scale(y) * pg_ref[...]


def _ffn_call(x, hn, wg_bf16, wu_bf16, conv_w, conv_b, wd_bf16, pg, *, tm=512, tf=1024):
    B, S, D = x.shape
    d_ff = wg_bf16.shape[1]
    halo = F32_SUBLANE_TILE
    assert CONV_WIDTH - 1 <= halo
    return pl.pallas_call(
        _ffn_kernel,
        grid=(B, S // tm, d_ff // tf),
        in_specs=[
            pl.BlockSpec((1, tm, D), lambda b, i, f: (b, i, 0)),
            pl.BlockSpec((1, tm, D), lambda b, i, f: (b, i, 0)),
            pl.BlockSpec((D, tf), lambda b, i, f: (0, f)),
            pl.BlockSpec((D, tf), lambda b, i, f: (0, f)),
            pl.BlockSpec((CONV_WIDTH, tf), lambda b, i, f: (0, f)),
            pl.BlockSpec((1, tf), lambda b, i, f: (0, f)),
            pl.BlockSpec((tf, D), lambda b, i, f: (f, 0)),
            pl.BlockSpec((1, D), lambda b, i, f: (0, 0)),
        ],
        out_specs=pl.BlockSpec((1, tm, D), lambda b, i, f: (b, i, 0)),
        out_shape=jax.ShapeDtypeStruct((B, S, D), F32),
        scratch_shapes=[pltpu.VMEM((d_ff // tf, halo, tf), F32), pltpu.VMEM((tm, tf), BF16)],
        compiler_params=pltpu.CompilerParams(
            dimension_semantics=("parallel", "arbitrary", "arbitrary"),
            vmem_limit_bytes=V7X_VMEM_LIMIT_BYTES),
        name="conv_glu_ffn",
    )(x, hn, wg_bf16, wu_bf16, conv_w, conv_b, wd_bf16, pg)


def _rope_tables(positions):
    inv_freq = ROPE_THETA ** (-jnp.arange(ROT_HALF, dtype=F32) / ROT_HALF)
    batch, seq = positions.shape
    per_row = HEAD_DIM // ROT_HALF
    pos = positions.astype(F32).reshape(batch, seq // per_row, per_row, 1)
    ang = (pos * inv_freq).reshape(batch, seq // per_row, HEAD_DIM)
    cos = jnp.cos(ang).reshape(batch, seq, ROT_HALF)
    sin = jnp.sin(ang).reshape(batch, seq, ROT_HALF)
    ones = jnp.ones((batch, seq, HEAD_DIM - ROT_DIM), F32)
    zeros_tail = jnp.zeros((batch, seq, HEAD_DIM - ROT_HALF), F32)
    zeros_head = jnp.zeros_like(sin)
    cos_t = jnp.concatenate([cos, cos, ones], axis=-1)
    sa_t = jnp.concatenate([-sin, zeros_tail], axis=-1)
    sb_t = jnp.concatenate([zeros_head, sin, zeros_tail[..., :HEAD_DIM - ROT_DIM]], axis=-1)
    return cos_t, sa_t, sb_t


def kernel(x, positions, attn_pre_g, w_qkv, moba_out_g, dil_out_g, w_o, attn_post_g, ffn_pre_g,
           w_gate, w_up, conv_w, conv_b, w_down, ffn_post_g):
    depth = w_qkv.shape[0]
    n_heads_a = moba_out_g.shape[1] // HEAD_DIM
    n_heads_b = dil_out_g.shape[1] // HEAD_DIM
    cos_t, sa_t, sb_t = _rope_tables(positions)
    for l in range(depth):
        qkv_heads, vt, d4, d16 = _qkv_call(x, attn_pre_g[l][None], w_qkv[l].astype(BF16),
                                           cos_t, sa_t, sb_t,
                                           n_heads_a=n_heads_a, n_heads_b=n_heads_b)
        oa, (wo_bf16, wg_bf16, wu_bf16, wd_bf16) = _moba_call(
            qkv_heads, vt, [w_o[l], w_gate[l], w_up[l], w_down[l]], n_heads_a=n_heads_a)
        ob = _dilated_call(qkv_heads, d4, d16, n_heads_b=n_heads_b, q_off=3 * n_heads_a,
                           k_off=3 * n_heads_a + n_heads_b, v_off=3 * n_heads_a + 2 * n_heads_b)
        x, hn = _attnout_call(x, oa, ob, moba_out_g[l][None], dil_out_g[l][None],
                              wo_bf16, attn_post_g[l][None], ffn_pre_g[l][None])
        x = _ffn_call(x, hn, wg_bf16, wu_bf16, conv_w[l], conv_b[l][None], wd_bf16,
                      ffn_post_g[l][None])
    return x
```

```python
import functools

import jax
import jax.numpy as jnp
from jax import lax
from jax.experimental import pallas as pl
from jax.experimental.pallas import tpu as pltpu

F32 = jnp.float32
BF16 = jnp.bfloat16

HEAD_DIM = 128
ROT_DIM = HEAD_DIM // 4
ROT_HALF = ROT_DIM // 2
ROPE_THETA = 500000.0
MOBA_BLOCK = 256
MOBA_TOPK = 3
DIL_PAIRS = ((128, 1), (512, 4), (2048, 16))
DIL_BLOCK = 128
CONV_WIDTH = 3
RMS_EPS = 1e-6
SCALE = HEAD_DIM ** -0.5
SCALE_LOG2E = SCALE * 1.4426950408889634
NEG = -1e30

V7X_VMEM_LIMIT_BYTES = 56 * 1024 * 1024
BF16_SUBLANE_TILE = 16
V7X_MXU_WIDTH = 256
F32_SUBLANE_TILE = 8
QKV_SLAB_SLOTS = 2
MOBA_VT_ROWS = HEAD_DIM + BF16_SUBLANE_TILE

NT_DIMS = (((1,), (1,)), ((), ()))


def _rms_scale(x):
    return lax.rsqrt(jnp.mean(x * x, axis=-1, keepdims=True) + RMS_EPS)


def _qkv_kernel(x_ref, g_ref, w_ref, cos_ref, sa_ref, sb_ref,
                nat_ref, vt_ref, d4_ref, d16_ref, hn_ref, slab_ref, slab4_ref, *, heads_per_tile):
    j = pl.program_id(2)
    tm = hn_ref.shape[0]
    heads_per_dot = V7X_MXU_WIDTH // HEAD_DIM
    dot_width = heads_per_dot * HEAD_DIM

    @pl.when(j == 0)
    def _():
        x = x_ref[0]
        hn_ref[...] = (x * _rms_scale(x) * g_ref[...]).astype(BF16)

    def run(with_rope, with_vt, with_dilated):
        if with_rope:
            q_scale = jnp.where(jnp.logical_or(j == 0, j == 3), SCALE_LOG2E, 1.0).astype(F32)
            cos, sa, sb = cos_ref[0] * q_scale, sa_ref[0] * q_scale, sb_ref[0] * q_scale

            def rope(t):
                return (t * cos + pltpu.roll(t, HEAD_DIM - ROT_HALF, 1) * sa
                        + pltpu.roll(t, ROT_HALF, 1) * sb)

        for c in range(heads_per_tile // heads_per_dot):
            acc = jnp.dot(hn_ref[...], w_ref[0, :, c * dot_width:(c + 1) * dot_width],
                          preferred_element_type=F32)
            for hh in range(heads_per_dot):
                h = c * heads_per_dot + hh
                t = acc[:, hh * HEAD_DIM:(hh + 1) * HEAD_DIM]
                if with_rope:
                    t = rope(t)
                nat_ref[0, h] = t.astype(BF16)
                if with_vt:
                    for blk in range(tm // MOBA_BLOCK):
                        vt_ref[0, h, blk, 0:HEAD_DIM, :] = (
                            t[blk * MOBA_BLOCK:(blk + 1) * MOBA_BLOCK].T.astype(BF16))
                        vt_ref[0, h, blk, HEAD_DIM:, :] = jnp.ones(
                            (MOBA_VT_ROWS - HEAD_DIM, MOBA_BLOCK), BF16)
                if with_dilated:
                    slot = h % slab_ref.shape[0]
                    q4 = tm // 4
                    slab_ref[slot] = t
                    for r4 in range(4):
                        cls = slab_ref[slot, pl.ds(r4, q4, stride=4), :]
                        d4_ref[0, h, r4] = cls.astype(BF16)
                        slab4_ref[slot, r4 * q4:(r4 + 1) * q4, :] = cls
                    for r4 in range(4):
                        for m in range(4):
                            cls = slab4_ref[slot, pl.ds(r4 * q4 + m, q4 // 4, stride=4), :]
                            d16_ref[0, h, r4 + 4 * m] = cls.astype(BF16)

    pl.when(j < 2)(lambda: run(True, False, False))
    pl.when(j == 2)(lambda: run(False, True, False))
    pl.when(jnp.logical_or(j == 3, j == 4))(lambda: run(True, False, True))
    pl.when(j == 5)(lambda: run(False, False, True))


def _qkv_call(x, g, w_tiles, cos_t, sa_t, sb_t, *, n_heads_a, n_heads_b, tm=1024):
    B, S, D = x.shape
    n_tiles, _, tn = w_tiles.shape
    N = n_tiles * tn
    hpt = n_heads_a
    assert n_heads_a == n_heads_b and tn == hpt * HEAD_DIM and n_tiles == 6
    first_b_tile = 3
    kern = functools.partial(_qkv_kernel, heads_per_tile=hpt)

    def dil_index(b, i, j):
        return (b, jnp.maximum(j - first_b_tile, 0), 0, i, 0)

    return pl.pallas_call(
        kern,
        grid=(B, S // tm, N // tn),
        in_specs=[
            pl.BlockSpec((1, tm, D), lambda b, i, j: (b, i, 0)),
            pl.BlockSpec((1, D), lambda b, i, j: (0, 0)),
            pl.BlockSpec((1, D, tn), lambda b, i, j: (j, 0, 0)),
            pl.BlockSpec((1, tm, HEAD_DIM), lambda b, i, j: (b, i, 0)),
            pl.BlockSpec((1, tm, HEAD_DIM), lambda b, i, j: (b, i, 0)),
            pl.BlockSpec((1, tm, HEAD_DIM), lambda b, i, j: (b, i, 0)),
        ],
        out_specs=[
            pl.BlockSpec((1, hpt, tm, HEAD_DIM), lambda b, i, j: (b, j, i, 0)),
            pl.BlockSpec((1, hpt, tm // MOBA_BLOCK, MOBA_VT_ROWS, MOBA_BLOCK),
                         lambda b, i, j: (b, 0, i, 0, 0)),
            pl.BlockSpec((1, hpt, 4, tm // 4, HEAD_DIM), dil_index),
            pl.BlockSpec((1, hpt, 16, tm // 16, HEAD_DIM), dil_index),
        ],
        out_shape=[
            jax.ShapeDtypeStruct((B, N // HEAD_DIM, S, HEAD_DIM), BF16),
            jax.ShapeDtypeStruct((B, n_heads_a, S // MOBA_BLOCK, MOBA_VT_ROWS, MOBA_BLOCK), BF16),
            jax.ShapeDtypeStruct((B, 3 * n_heads_b, 4, S // 4, HEAD_DIM), BF16),
            jax.ShapeDtypeStruct((B, 3 * n_heads_b, 16, S // 16, HEAD_DIM), BF16),
        ],
        scratch_shapes=[pltpu.VMEM((tm, D), BF16),
                        pltpu.VMEM((QKV_SLAB_SLOTS, tm, HEAD_DIM), F32),
                        pltpu.VMEM((QKV_SLAB_SLOTS, tm, HEAD_DIM), F32)],
        compiler_params=pltpu.CompilerParams(
            dimension_semantics=("parallel", "parallel", "arbitrary"),
            vmem_limit_bytes=V7X_VMEM_LIMIT_BYTES),
        name="qkv_rope",
    )(x, g, w_tiles, cos_t, sa_t, sb_t)


def _moba_kernel(*refs, n_blocks, heads, n_cast, lookahead=8, group=2):
    q_ref, k_ref, vt_ref = refs[:3]
    cast_in = refs[3:3 + n_cast]
    o_ref = refs[3 + n_cast]
    cast_out = refs[4 + n_cast:4 + 2 * n_cast]
    kmean_ref, bias_ref, acc_ref = refs[4 + 2 * n_cast:]
    assert n_blocks % group == 0
    i = pl.program_id(2)
    blk_sz = MOBA_BLOCK

    for w_in, w_out in zip(cast_in, cast_out):
        w_out[0] = w_in[0].astype(BF16)

    @pl.when(i == 0)
    def _():
        for h in range(heads):
            for blk in range(n_blocks):
                kb = k_ref[0, h, blk * blk_sz:(blk + 1) * blk_sz, :].astype(F32)
                kmean_ref[h, blk:blk + 1, :] = jnp.mean(kb, axis=0, keepdims=True)

    def select_blocks(h, q):
        km = kmean_ref[h]
        km_hi = km.astype(BF16)
        km_lo = (km - km_hi.astype(F32)).astype(BF16)
        gate = (lax.dot_general(km_hi, q, NT_DIMS, preferred_element_type=F32)
                + lax.dot_general(km_lo, q, NT_DIMS, preferred_element_type=F32))
        blk_id = lax.broadcasted_iota(jnp.int32, gate.shape, 0).astype(F32)
        neg_inf = jnp.float32(-jnp.inf)
        g = jnp.where(blk_id < i.astype(F32), gate, neg_inf)
        sel = jnp.zeros(gate.shape, dtype=jnp.bool_)
        for _ in range(MOBA_TOPK):
            m = jnp.max(g, axis=0, keepdims=True)
            first = jnp.min(jnp.where(g == m, blk_id, float(n_blocks)), axis=0, keepdims=True)
            pick = jnp.logical_and(blk_id == first, m > neg_inf)
            sel = jnp.logical_or(sel, pick)
            g = jnp.where(pick, neg_inf, g)
        bias_ref[h] = jnp.where(sel, 0.0, NEG).astype(F32)

    def scores(h, first_blk, n_blk):
        rows = n_blk * blk_sz
        kb = k_ref[0, h, pl.ds(pl.multiple_of(first_blk * blk_sz, blk_sz), rows), :]
        return lax.dot_general(kb, q_ref[0, h], NT_DIMS, preferred_element_type=F32)

    def pipelined(stage_a, stage_b):
        ahead = [stage_a(h) for h in range(min(lookahead, heads))]
        outs = []
        for h in range(heads):
            if h + lookahead < heads:
                ahead.append(stage_a(h + lookahead))
            outs.append(stage_b(h, ahead[h]))
        return outs

    def own_a(h):
        select_blocks(h, q_ref[0, h])
        return scores(h, i, 1)

    def own_b(h, s):
        key_pos = lax.broadcasted_iota(jnp.int32, s.shape, 0)
        q_pos = lax.broadcasted_iota(jnp.int32, s.shape, 1)
        t = jnp.where(key_pos <= q_pos, s, NEG)
        m0 = jnp.max(t, axis=0, keepdims=True)
        p = jnp.exp2(t - m0)
        acc_ref[h] = jnp.dot(vt_ref[0, h, i], p.astype(BF16), preferred_element_type=F32)
        return m0

    init = pipelined(own_a, own_b)

    def body(c, carry):
        first = c * group

        def past_b(h, s):
            m_prev = carry[h]
            chunks = [s[g * blk_sz:(g + 1) * blk_sz] for g in range(group)]
            brows = [bias_ref[h, pl.ds(first + g, 1), :] for g in range(group)]
            m_new = m_prev
            for sg, brow in zip(chunks, brows):
                m_new = jnp.maximum(m_new, jnp.max(sg, axis=0, keepdims=True) + brow)
            acc_new = jnp.exp2(m_prev - m_new) * acc_ref[h]
            for g, (sg, brow) in enumerate(zip(chunks, brows)):
                pg = jnp.exp2(sg - (m_new - brow))
                acc_new = acc_new + jnp.dot(vt_ref[0, h, first + g], pg.astype(BF16),
                                            preferred_element_type=F32)
            acc_ref[h] = acc_new
            return m_new

        return tuple(pipelined(lambda h: scores(h, first, group), past_b))

    lax.fori_loop(0, (i + group - 1) // group, body, tuple(init))
    for h in range(heads):
        acc_fin = acc_ref[h]
        o_ref[0, h] = (acc_fin[:HEAD_DIM] / acc_fin[HEAD_DIM:HEAD_DIM + 1]).T


def _moba_call(qkv_heads, vt, f32_weights, *, n_heads_a, heads_per_step=8):
    B, _, S, _ = qkv_heads.shape
    n_blocks = S // MOBA_BLOCK
    tq = MOBA_BLOCK
    hps = heads_per_step
    assert n_heads_a % hps == 0
    n_groups = n_heads_a // hps
    k_first = n_groups
    grid = (B, n_groups, S // tq)
    n_steps = grid[0] * grid[1] * grid[2]

    def slab_index(b, h, i):
        return ((b * grid[1] + h) * grid[2] + i, 0, 0)

    slabs, slab_specs, slab_shapes = [], [], []
    for w in f32_weights:
        rows, cols = w.shape
        assert rows % (n_steps * BF16_SUBLANE_TILE) == 0
        slabs.append(w.reshape(n_steps, rows // n_steps, cols))
        slab_specs.append(pl.BlockSpec((1, rows // n_steps, cols), slab_index))
        slab_shapes.append(jax.ShapeDtypeStruct((n_steps, rows // n_steps, cols), BF16))

    kern = functools.partial(_moba_kernel, n_blocks=n_blocks, heads=hps, n_cast=len(slabs))
    outs = pl.pallas_call(
        kern,
        grid=grid,
        in_specs=[
            pl.BlockSpec((1, hps, tq, HEAD_DIM), lambda b, h, i: (b, h, i, 0)),
            pl.BlockSpec((1, hps, S, HEAD_DIM), lambda b, h, i: (b, k_first + h, 0, 0)),
            pl.BlockSpec((1, hps, n_blocks, MOBA_VT_ROWS, MOBA_BLOCK),
                         lambda b, h, i: (b, h, 0, 0, 0)),
        ] + slab_specs,
        out_specs=[pl.BlockSpec((1, hps, tq, HEAD_DIM), lambda b, h, i: (b, h, i, 0))] + slab_specs,
        out_shape=[jax.ShapeDtypeStruct((B, n_heads_a, S, HEAD_DIM), F32)] + slab_shapes,
        scratch_shapes=[pltpu.VMEM((hps, n_blocks, HEAD_DIM), F32),
                        pltpu.VMEM((hps, n_blocks, tq), F32),
                        pltpu.VMEM((hps, MOBA_VT_ROWS, tq), F32)],
        compiler_params=pltpu.CompilerParams(
            dimension_semantics=("parallel", "parallel", "arbitrary"),
            vmem_limit_bytes=V7X_VMEM_LIMIT_BYTES),
        name="moba_attn",
    )(qkv_heads, qkv_heads, vt, *slabs)
    return outs[0], [o.reshape(w.shape) for o, w in zip(outs[1:], f32_weights)]


def _dilated_kernel(q1_ref, k1_ref, v1_ref, q4_ref, k4_ref, v4_ref, q16_ref, k16_ref, v16_ref,
                    ob_ref, o_scr, lse_scr, bias_scr, *, seq_len, blocks_per_iter, combine_rows):
    blk = DIL_BLOCK
    branch_refs = ((q1_ref, k1_ref, v1_ref), (q4_ref, k4_ref, v4_ref), (q16_ref, k16_ref, v16_ref))

    qi = lax.broadcasted_iota(jnp.int32, (blk, 2 * blk), 0)
    ki = lax.broadcasted_iota(jnp.int32, (blk, 2 * blk), 1)
    dist = qi + blk - ki
    bias_scr[0] = jnp.where(jnp.logical_and(dist >= 0, dist <= blk), 0.0, NEG).astype(F32)
    bias_scr[1] = jnp.where(ki <= qi, 0.0, NEG).astype(F32)

    for g, (window, d) in enumerate(DIL_PAIRS):
        assert window // d == DIL_BLOCK
        q_ref, k_ref, v_ref = branch_refs[g]
        n_blk = seq_len // d // blk

        def rows_of(ref, r, start, size, d=d):
            if d == 1:
                return ref[0, 0, pl.ds(start, size), :]
            return ref[0, 0, r, pl.ds(start, size), :]

        def key_start(n):
            return pl.multiple_of(jnp.maximum(n - 1, 0) * blk, blk)

        def scores(r, n, q_ref=q_ref, k_ref=k_ref, rows_of=rows_of):
            qb = rows_of(q_ref, r, pl.multiple_of(n * blk, blk), blk)
            kb = rows_of(k_ref, r, key_start(n), 2 * blk)
            return lax.dot_general(qb, kb, NT_DIMS, preferred_element_type=F32)

        def finish(r, n, s, g=g, d=d, v_ref=v_ref, rows_of=rows_of):
            t = s + bias_scr[jnp.where(n == 0, 1, 0)]
            m = jnp.max(t, axis=-1, keepdims=True)
            p = jnp.exp2(t - m)
            vb = rows_of(v_ref, r, key_start(n), 2 * blk)
            v_ones = jnp.concatenate([vb, jnp.ones_like(vb)], axis=1)
            pv = jnp.dot(p.astype(BF16), v_ones, preferred_element_type=F32)
            den = pv[:, HEAD_DIM:]
            q_start = pl.multiple_of(n * blk, blk)
            if d == 1:
                rows = pl.ds(q_start, blk)
            elif d == 4:
                rows = pl.ds(r * (seq_len // 4) + q_start, blk)
            else:
                r4, m4 = r % 4, r // 4
                rows = pl.ds(r4 * (seq_len // 4) + 4 * q_start + m4, blk, stride=4)
            o_scr[g, rows, :] = pv[:, :HEAD_DIM] / den
            lse_scr[g, rows, :] = m + jnp.log2(den)

        def run_blocks(tasks, scores=scores, finish=finish):
            nxt = scores(*tasks[0])
            for idx, (r, n) in enumerate(tasks):
                cur = nxt
                if idx + 1 < len(tasks):
                    nxt = scores(*tasks[idx + 1])
                finish(r, n, cur)

        n_per_iter = max(1, min(blocks_per_iter // d, n_blk))

        def body(it, carry, run_blocks=run_blocks, d=d, n_per_iter=n_per_iter):
            run_blocks([(r, it * n_per_iter + u) for u in range(n_per_iter) for r in range(d)])
            return carry

        lax.fori_loop(0, n_blk // n_per_iter, body, 0)

    def combine(c, carry):
        i0 = pl.multiple_of(c * combine_rows, combine_rows)
        for r4 in range(4):
            tok = pl.ds(4 * i0 + r4, combine_rows, stride=4)
            cls = pl.ds(r4 * (seq_len // 4) + i0, combine_rows)
            l1, l2, l3 = lse_scr[0, tok, :], lse_scr[1, cls, :], lse_scr[2, cls, :]
            lmax = jnp.maximum(jnp.maximum(l1, l2), l3)
            e1, e2, e3 = jnp.exp2(l1 - lmax), jnp.exp2(l2 - lmax), jnp.exp2(l3 - lmax)
            num = e1 * o_scr[0, tok, :] + e2 * o_scr[1, cls, :] + e3 * o_scr[2, cls, :]
            ob_ref[0, 0, tok, :] = num / (e1 + e2 + e3)
        return carry

    lax.fori_loop(0, seq_len // 4 // combine_rows, combine, 0)


def _dilated_call(qkv_heads, d4, d16, *, n_heads_b, q_off, k_off, v_off, blocks_per_iter=16):
    B, _, S, _ = qkv_heads.shape
    offs_nat = (q_off, k_off, v_off)
    offs_dil = (0, n_heads_b, 2 * n_heads_b)
    in_arrays = [qkv_heads] * 3 + [d4] * 3 + [d16] * 3
    in_specs = (
        [pl.BlockSpec((1, 1, S, HEAD_DIM), lambda b, h, off=off: (b, off + h, 0, 0))
         for off in offs_nat]
        + [pl.BlockSpec((1, 1, 4, S // 4, HEAD_DIM), lambda b, h, off=off: (b, off + h, 0, 0, 0))
           for off in offs_dil]
        + [pl.BlockSpec((1, 1, 16, S // 16, HEAD_DIM), lambda b, h, off=off: (b, off + h, 0, 0, 0))
           for off in offs_dil])
    n_br = len(DIL_PAIRS)
    return pl.pallas_call(
        functools.partial(_dilated_kernel, seq_len=S, blocks_per_iter=blocks_per_iter,
                          combine_rows=64),
        grid=(B, n_heads_b),
        in_specs=in_specs,
        out_specs=pl.BlockSpec((1, 1, S, HEAD_DIM), lambda b, h: (b, h, 0, 0)),
        out_shape=jax.ShapeDtypeStruct((B, n_heads_b, S, HEAD_DIM), F32),
        scratch_shapes=[pltpu.VMEM((n_br, S, HEAD_DIM), F32),
                        pltpu.VMEM((n_br, S, HEAD_DIM), F32),
                        pltpu.VMEM((2, DIL_BLOCK, 2 * DIL_BLOCK), F32)],
        compiler_params=pltpu.CompilerParams(
            dimension_semantics=("parallel", "parallel"),
            vmem_limit_bytes=V7X_VMEM_LIMIT_BYTES),
        name="dilated_attn",
    )(*in_arrays)


def _attnout_kernel(x_ref, oa_ref, ob_ref, ga_ref, gb_ref, wo_ref, pg_ref, fg_ref,
                    out_ref, hn_ref, mix_ref, *, row_chunks):
    tm = x_ref.shape[1]
    rows_per_chunk = tm // row_chunks
    for c in range(row_chunks):
        rows = slice(c * rows_per_chunk, (c + 1) * rows_per_chunk)
        col = 0
        for o_ref, g_ref in ((oa_ref, ga_ref), (ob_ref, gb_ref)):
            n_heads = o_ref.shape[1]
            ssq = None
            for h in range(n_heads):
                t = o_ref[0, h, rows, :]
                part = jnp.sum(t * t, axis=-1, keepdims=True)
                ssq = part if ssq is None else ssq + part
            inv = lax.rsqrt(ssq / (n_heads * HEAD_DIM) + RMS_EPS)
            for h in range(n_heads):
                gs = slice(h * HEAD_DIM, (h + 1) * HEAD_DIM)
                mix_ref[rows, col:col + HEAD_DIM] = (
                    o_ref[0, h, rows, :] * inv * g_ref[:, gs]).astype(BF16)
                col += HEAD_DIM
        y = jnp.dot(mix_ref[rows, :], wo_ref[...], preferred_element_type=F32)
        x_new = x_ref[0, rows, :] + y * _rms_scale(y) * pg_ref[...]
        out_ref[0, rows, :] = x_new
        hn_ref[0, rows, :] = (x_new * _rms_scale(x_new) * fg_ref[...]).astype(BF16)


def _attnout_call(x, oa, ob, ga, gb, wo_bf16, pg, ffn_g, *, tm=512, row_chunks=2):
    B, S, D = x.shape
    head_spec_a = pl.BlockSpec((1, oa.shape[1], tm, HEAD_DIM), lambda b, i: (b, 0, i, 0))
    head_spec_b = pl.BlockSpec((1, ob.shape[1], tm, HEAD_DIM), lambda b, i: (b, 0, i, 0))
    row_spec = pl.BlockSpec((1, tm, D), lambda b, i: (b, i, 0))
    return pl.pallas_call(
        functools.partial(_attnout_kernel, row_chunks=row_chunks),
        grid=(B, S // tm),
        in_specs=[
            row_spec,
            head_spec_a, head_spec_b,
            pl.BlockSpec((1, ga.shape[1]), lambda b, i: (0, 0)),
            pl.BlockSpec((1, gb.shape[1]), lambda b, i: (0, 0)),
            pl.BlockSpec(wo_bf16.shape, lambda b, i: (0, 0)),
            pl.BlockSpec((1, D), lambda b, i: (0, 0)),
            pl.BlockSpec((1, D), lambda b, i: (0, 0)),
        ],
        out_specs=[row_spec, row_spec],
        out_shape=[jax.ShapeDtypeStruct((B, S, D), F32), jax.ShapeDtypeStruct((B, S, D), BF16)],
        scratch_shapes=[pltpu.VMEM((tm, wo_bf16.shape[0]), BF16)],
        compiler_params=pltpu.CompilerParams(
            dimension_semantics=("parallel", "parallel"),
            vmem_limit_bytes=V7X_VMEM_LIMIT_BYTES),
        name="attn_out",
    )(x, oa, ob, ga, gb, wo_bf16, pg, ffn_g)


def _ffn_kernel(x_ref, hn_ref, wg_ref, wu_ref, cw_ref, cb_ref, wd_ref, pg_ref,
                out_ref, tail_ref, act_ref):
    i = pl.program_id(1)
    f = pl.program_id(2)
    tm = x_ref.shape[1]
    halo = tail_ref.shape[1]

    @pl.when(i == 0)
    def _():
        tail_ref[f] = jnp.zeros(tail_ref.shape[1:], F32)

    @pl.when(f == 0)
    def _():
        out_ref[0] = jnp.zeros(out_ref.shape[1:], F32)

    gate = jnp.dot(hn_ref[0], wg_ref[...], preferred_element_type=F32)
    up = jnp.dot(hn_ref[0], wu_ref[...], preferred_element_type=F32)
    cw = cw_ref[...]
    cb = cb_ref[...]

    def gated(g_m2, g_m1, g_0, u):
        gc = cb + g_m2 * cw[0:1, :] + g_m1 * cw[1:2, :] + g_0 * cw[2:3, :]
        gelu = 0.5 * gc * (1.0 + jnp.tanh(0.7978845608028654 * (gc + 0.044715 * (gc * gc * gc))))
        return (gelu * u).astype(BF16)

    act_ref[...] = gated(pltpu.roll(gate, 2, 0), pltpu.roll(gate, 1, 0), gate, up)
    head = BF16_SUBLANE_TILE
    ext = jnp.concatenate([tail_ref[f], gate[0:head]], axis=0)
    act_ref[0:head, :] = gated(ext[halo - 2:halo - 2 + head], ext[halo - 1:halo - 1 + head],
                               gate[0:head], up[0:head])
    tail_ref[f] = gate[tm - halo:, :]
    out_ref[0] += jnp.dot(act_ref[...], wd_ref[...], preferred_element_type=F32)

    @pl.when(f == pl.num_programs(2) - 1)
    def _():
        y = out_ref[0]
        out_ref[0] = x_ref[0] + y * _rms_scale(y) * pg_ref[...]


def _ffn_call(x, hn, wg_bf16, wu_bf16, conv_w, conv_b, wd_bf16, pg, *, tm=512, tf=1024):
    B, S, D = x.shape
    d_ff = wg_bf16.shape[1]
    halo = F32_SUBLANE_TILE
    assert CONV_WIDTH - 1 <= halo
    return pl.pallas_call(
        _ffn_kernel,
        grid=(B, S // tm, d_ff // tf),
        in_specs=[
            pl.BlockSpec((1, tm, D), lambda b, i, f: (b, i, 0)),
            pl.BlockSpec((1, tm, D), lambda b, i, f: (b, i, 0)),
            pl.BlockSpec((D, tf), lambda b, i, f: (0, f)),
            pl.BlockSpec((D, tf), lambda b, i, f: (0, f)),
            pl.BlockSpec((CONV_WIDTH, tf), lambda b, i, f: (0, f)),
            pl.BlockSpec((1, tf), lambda b, i, f: (0, f)),
            pl.BlockSpec((tf, D), lambda b, i, f: (f, 0)),
            pl.BlockSpec((1, D), lambda b, i, f: (0, 0)),
        ],
        out_specs=pl.BlockSpec((1, tm, D), lambda b, i, f: (b, i, 0)),
        out_shape=jax.ShapeDtypeStruct((B, S, D), F32),
        scratch_shapes=[pltpu.VMEM((d_ff // tf, halo, tf), F32), pltpu.VMEM((tm, tf), BF16)],
        compiler_params=pltpu.CompilerParams(
            dimension_semantics=("parallel", "arbitrary", "arbitrary"),
            vmem_limit_bytes=V7X_VMEM_LIMIT_BYTES),
        name="conv_glu_ffn",
    )(x, hn, wg_bf16, wu_bf16, conv_w, conv_b, wd_bf16, pg)


def _rope_tables(positions):
    inv_freq = ROPE_THETA ** (-jnp.arange(ROT_HALF, dtype=F32) / ROT_HALF)
    batch, seq = positions.shape
    per_row = HEAD_DIM // ROT_HALF
    pos = positions.astype(F32).reshape(batch, seq // per_row, per_row, 1)
    ang = (pos * inv_freq).reshape(batch, seq // per_row, HEAD_DIM)
    cos, sin = lax.optimization_barrier((jnp.cos(ang), jnp.sin(ang)))
    cos = cos.reshape(batch, seq, ROT_HALF)
    sin = sin.reshape(batch, seq, ROT_HALF)
    ones = jnp.ones((batch, seq, HEAD_DIM - ROT_DIM), F32)
    zeros_tail = jnp.zeros((batch, seq, HEAD_DIM - ROT_HALF), F32)
    zeros_head = jnp.zeros_like(sin)
    cos_t = jnp.concatenate([cos, cos, ones], axis=-1)
    sa_t = jnp.concatenate([-sin, zeros_tail], axis=-1)
    sb_t = jnp.concatenate([zeros_head, sin, zeros_tail[..., :HEAD_DIM - ROT_DIM]], axis=-1)
    return cos_t, sa_t, sb_t


def kernel(x, positions, attn_pre_g, w_qkv, moba_out_g, dil_out_g, w_o, attn_post_g, ffn_pre_g,
           w_gate, w_up, conv_w, conv_b, w_down, ffn_post_g):
    depth = w_qkv.shape[0]
    n_heads_a = moba_out_g.shape[1] // HEAD_DIM
    n_heads_b = dil_out_g.shape[1] // HEAD_DIM
    cos_t, sa_t, sb_t = _rope_tables(positions)
    for l in range(depth):
        d_model = w_qkv.shape[1]
        seg = n_heads_a * HEAD_DIM
        w_tiles = w_qkv[l].astype(BF16).reshape(d_model, -1, seg).transpose(1, 0, 2)
        qkv_heads, vt, d4, d16 = _qkv_call(x, attn_pre_g[l][None], w_tiles, cos_t, sa_t, sb_t,
                                           n_heads_a=n_heads_a, n_heads_b=n_heads_b)
        oa, (wo_bf16, wg_bf16, wu_bf16, wd_bf16) = _moba_call(
            qkv_heads, vt, [w_o[l], w_gate[l], w_up[l], w_down[l]], n_heads_a=n_heads_a)
        ob = _dilated_call(qkv_heads, d4, d16, n_heads_b=n_heads_b, q_off=3 * n_heads_a,
                           k_off=3 * n_heads_a + n_heads_b, v_off=3 * n_heads_a + 2 * n_heads_b)
        x, hn = _attnout_call(x, oa, ob, moba_out_g[l][None], dil_out_g[l][None],
                              wo_bf16, attn_post_g[l][None], ffn_pre_g[l][None])
        x = _ffn_call(x, hn, wg_bf16, wu_bf16, conv_w[l], conv_b[l][None], wd_bf16,
                      ffn_post_g[l][None])
    return x
```

```python
import functools

import jax
import jax.numpy as jnp
from jax import lax
from jax.experimental import pallas as pl
from jax.experimental.pallas import tpu as pltpu

F32 = jnp.float32
BF16 = jnp.bfloat16

HEAD_DIM = 128
ROT_DIM = HEAD_DIM // 4
ROT_HALF = ROT_DIM // 2
ROPE_THETA = 500000.0
MOBA_BLOCK = 256
MOBA_TOPK = 3
DIL_PAIRS = ((128, 1), (512, 4), (2048, 16))
DIL_BLOCK = 128
CONV_WIDTH = 3
RMS_EPS = 1e-6
SCALE = HEAD_DIM ** -0.5
SCALE_LOG2E = SCALE * 1.4426950408889634
NEG = -1e30

V7X_VMEM_LIMIT_BYTES = 56 * 1024 * 1024
BF16_SUBLANE_TILE = 16
V7X_MXU_WIDTH = 256
F32_SUBLANE_TILE = 8
FFN_LAST_STEP_ROW_CHUNKS = 2
QKV_SLAB_SLOTS = 2
MOBA_VT_ROWS = HEAD_DIM + BF16_SUBLANE_TILE

NT_DIMS = (((1,), (1,)), ((), ()))


def _rms_scale(x):
    return lax.rsqrt(jnp.mean(x * x, axis=-1, keepdims=True) + RMS_EPS)


def _qkv_kernel(x_ref, g_ref, w_ref, cos_ref, sa_ref, sb_ref,
                nat_ref, vt_ref, d4_ref, d16_ref, hn_ref, slab_ref, slab4_ref, *, heads_per_tile):
    j = pl.program_id(2)
    tm = hn_ref.shape[0]
    heads_per_dot = V7X_MXU_WIDTH // HEAD_DIM
    dot_width = heads_per_dot * HEAD_DIM

    @pl.when(j == 0)
    def _():
        x = x_ref[0]
        hn_ref[...] = (x * _rms_scale(x) * g_ref[...]).astype(BF16)

    def run(with_rope, with_vt, with_dilated):
        if with_rope:
            q_scale = jnp.where(jnp.logical_or(j == 0, j == 3), SCALE_LOG2E, 1.0).astype(F32)
            cos, sa, sb = cos_ref[0] * q_scale, sa_ref[0] * q_scale, sb_ref[0] * q_scale

            def rope(t):
                return (t * cos + pltpu.roll(t, HEAD_DIM - ROT_HALF, 1) * sa
                        + pltpu.roll(t, ROT_HALF, 1) * sb)

        for c in range(heads_per_tile // heads_per_dot):
            acc = jnp.dot(hn_ref[...], w_ref[:, c * dot_width:(c + 1) * dot_width],
                          preferred_element_type=F32)
            for hh in range(heads_per_dot):
                h = c * heads_per_dot + hh
                t = acc[:, hh * HEAD_DIM:(hh + 1) * HEAD_DIM]
                if with_rope:
                    t = rope(t)
                nat_ref[0, h] = t.astype(BF16)
                if with_vt:
                    for blk in range(tm // MOBA_BLOCK):
                        vt_ref[0, h, blk, 0:HEAD_DIM, :] = (
                            t[blk * MOBA_BLOCK:(blk + 1) * MOBA_BLOCK].T.astype(BF16))
                        vt_ref[0, h, blk, HEAD_DIM:, :] = jnp.ones(
                            (MOBA_VT_ROWS - HEAD_DIM, MOBA_BLOCK), BF16)
                if with_dilated:
                    slot = h % slab_ref.shape[0]
                    q4 = tm // 4
                    slab_ref[slot] = t
                    for r4 in range(4):
                        cls = slab_ref[slot, pl.ds(r4, q4, stride=4), :]
                        d4_ref[0, h, r4] = cls.astype(BF16)
                        slab4_ref[slot, r4 * q4:(r4 + 1) * q4, :] = cls
                    for r4 in range(4):
                        for m in range(4):
                            cls = slab4_ref[slot, pl.ds(r4 * q4 + m, q4 // 4, stride=4), :]
                            d16_ref[0, h, r4 + 4 * m] = cls.astype(BF16)

    pl.when(j < 2)(lambda: run(True, False, False))
    pl.when(j == 2)(lambda: run(False, True, False))
    pl.when(jnp.logical_or(j == 3, j == 4))(lambda: run(True, False, True))
    pl.when(j == 5)(lambda: run(False, False, True))


def _qkv_call(x, g, w_bf16, cos_t, sa_t, sb_t, *, n_heads_a, n_heads_b, tm=1024):
    B, S, D = x.shape
    N = w_bf16.shape[1]
    assert n_heads_a == n_heads_b and N == 3 * (n_heads_a + n_heads_b) * HEAD_DIM
    hpt = n_heads_a
    tn = hpt * HEAD_DIM
    first_b_tile = 3
    kern = functools.partial(_qkv_kernel, heads_per_tile=hpt)

    def dil_index(b, i, j):
        return (b, jnp.maximum(j - first_b_tile, 0), 0, i, 0)

    return pl.pallas_call(
        kern,
        grid=(B, S // tm, N // tn),
        in_specs=[
            pl.BlockSpec((1, tm, D), lambda b, i, j: (b, i, 0)),
            pl.BlockSpec((1, D), lambda b, i, j: (0, 0)),
            pl.BlockSpec((D, tn), lambda b, i, j: (0, j)),
            pl.BlockSpec((1, tm, HEAD_DIM), lambda b, i, j: (b, i, 0)),
            pl.BlockSpec((1, tm, HEAD_DIM), lambda b, i, j: (b, i, 0)),
            pl.BlockSpec((1, tm, HEAD_DIM), lambda b, i, j: (b, i, 0)),
        ],
        out_specs=[
            pl.BlockSpec((1, hpt, tm, HEAD_DIM), lambda b, i, j: (b, j, i, 0)),
            pl.BlockSpec((1, hpt, tm // MOBA_BLOCK, MOBA_VT_ROWS, MOBA_BLOCK),
                         lambda b, i, j: (b, 0, i, 0, 0)),
            pl.BlockSpec((1, hpt, 4, tm // 4, HEAD_DIM), dil_index),
            pl.BlockSpec((1, hpt, 16, tm // 16, HEAD_DIM), dil_index),
        ],
        out_shape=[
            jax.ShapeDtypeStruct((B, N // HEAD_DIM, S, HEAD_DIM), BF16),
            jax.ShapeDtypeStruct((B, n_heads_a, S // MOBA_BLOCK, MOBA_VT_ROWS, MOBA_BLOCK), BF16),
            jax.ShapeDtypeStruct((B, 3 * n_heads_b, 4, S // 4, HEAD_DIM), BF16),
            jax.ShapeDtypeStruct((B, 3 * n_heads_b, 16, S // 16, HEAD_DIM), BF16),
        ],
        scratch_shapes=[pltpu.VMEM((tm, D), BF16),
                        pltpu.VMEM((QKV_SLAB_SLOTS, tm, HEAD_DIM), F32),
                        pltpu.VMEM((QKV_SLAB_SLOTS, tm, HEAD_DIM), F32)],
        compiler_params=pltpu.CompilerParams(
            dimension_semantics=("parallel", "parallel", "arbitrary"),
            vmem_limit_bytes=V7X_VMEM_LIMIT_BYTES),
        name="qkv_rope",
    )(x, g, w_bf16, cos_t, sa_t, sb_t)


def _moba_kernel(*refs, n_blocks, heads, n_cast, lookahead=8, group=2):
    q_ref, k_ref, vt_ref = refs[:3]
    cast_in = refs[3:3 + n_cast]
    o_ref = refs[3 + n_cast]
    cast_out = refs[4 + n_cast:4 + 2 * n_cast]
    kmean_ref, bias_ref, acc_ref = refs[4 + 2 * n_cast:]
    assert n_blocks % group == 0
    i = pl.program_id(2)
    blk_sz = MOBA_BLOCK

    for w_in, w_out in zip(cast_in, cast_out):
        w_out[0] = w_in[0].astype(BF16)

    @pl.when(i == 0)
    def _():
        for h in range(heads):
            for blk in range(n_blocks):
                kb = k_ref[0, h, blk * blk_sz:(blk + 1) * blk_sz, :].astype(F32)
                kmean_ref[h, blk:blk + 1, :] = jnp.mean(kb, axis=0, keepdims=True)

    def select_blocks(h, q):
        km = kmean_ref[h]
        km_hi = km.astype(BF16)
        km_lo = (km - km_hi.astype(F32)).astype(BF16)
        gate = (lax.dot_general(km_hi, q, NT_DIMS, preferred_element_type=F32)
                + lax.dot_general(km_lo, q, NT_DIMS, preferred_element_type=F32))
        blk_id = lax.broadcasted_iota(jnp.int32, gate.shape, 0).astype(F32)
        neg_inf = jnp.float32(-jnp.inf)
        g = jnp.where(blk_id < i.astype(F32), gate, neg_inf)
        sel = jnp.zeros(gate.shape, dtype=jnp.bool_)
        for _ in range(MOBA_TOPK):
            m = jnp.max(g, axis=0, keepdims=True)
            first = jnp.min(jnp.where(g == m, blk_id, float(n_blocks)), axis=0, keepdims=True)
            pick = jnp.logical_and(blk_id == first, m > neg_inf)
            sel = jnp.logical_or(sel, pick)
            g = jnp.where(pick, neg_inf, g)
        bias_ref[h] = jnp.where(sel, 0.0, NEG).astype(F32)

    def scores(h, first_blk, n_blk):
        rows = n_blk * blk_sz
        kb = k_ref[0, h, pl.ds(pl.multiple_of(first_blk * blk_sz, blk_sz), rows), :]
        return lax.dot_general(kb, q_ref[0, h], NT_DIMS, preferred_element_type=F32)

    def pipelined(stage_a, stage_b):
        ahead = [stage_a(h) for h in range(min(lookahead, heads))]
        outs = []
        for h in range(heads):
            if h + lookahead < heads:
                ahead.append(stage_a(h + lookahead))
            outs.append(stage_b(h, ahead[h]))
        return outs

    def own_a(h):
        select_blocks(h, q_ref[0, h])
        return scores(h, i, 1)

    def own_b(h, s):
        key_pos = lax.broadcasted_iota(jnp.int32, s.shape, 0)
        q_pos = lax.broadcasted_iota(jnp.int32, s.shape, 1)
        t = jnp.where(key_pos <= q_pos, s, NEG)
        m0 = jnp.max(t, axis=0, keepdims=True)
        p = jnp.exp2(t - m0)
        acc_ref[h] = jnp.dot(vt_ref[0, h, i], p.astype(BF16), preferred_element_type=F32)
        return m0

    init = pipelined(own_a, own_b)

    def body(c, carry):
        first = c * group

        def past_b(h, s):
            m_prev = carry[h]
            chunks = [s[g * blk_sz:(g + 1) * blk_sz] for g in range(group)]
            brows = [bias_ref[h, pl.ds(first + g, 1), :] for g in range(group)]
            m_new = m_prev
            for sg, brow in zip(chunks, brows):
                m_new = jnp.maximum(m_new, jnp.max(sg, axis=0, keepdims=True) + brow)
            acc_new = jnp.exp2(m_prev - m_new) * acc_ref[h]
            for g, (sg, brow) in enumerate(zip(chunks, brows)):
                pg = jnp.exp2(sg - (m_new - brow))
                acc_new = acc_new + jnp.dot(vt_ref[0, h, first + g], pg.astype(BF16),
                                            preferred_element_type=F32)
            acc_ref[h] = acc_new
            return m_new

        return tuple(pipelined(lambda h: scores(h, first, group), past_b))

    lax.fori_loop(0, (i + group - 1) // group, body, tuple(init))
    for h in range(heads):
        acc_fin = acc_ref[h]
        o_ref[0, h] = (acc_fin[:HEAD_DIM] / acc_fin[HEAD_DIM:HEAD_DIM + 1]).T


def _moba_call(qkv_heads, vt, f32_weights, *, n_heads_a, heads_per_step=8):
    B, _, S, _ = qkv_heads.shape
    n_blocks = S // MOBA_BLOCK
    tq = MOBA_BLOCK
    hps = heads_per_step
    assert n_heads_a % hps == 0
    n_groups = n_heads_a // hps
    k_first = n_groups
    grid = (B, n_groups, S // tq)
    n_steps = grid[0] * grid[1] * grid[2]

    def slab_index(b, h, i):
        return ((b * grid[1] + h) * grid[2] + i, 0, 0)

    slabs, slab_specs, slab_shapes = [], [], []
    for w in f32_weights:
        rows, cols = w.shape
        assert rows % (n_steps * BF16_SUBLANE_TILE) == 0
        slabs.append(w.reshape(n_steps, rows // n_steps, cols))
        slab_specs.append(pl.BlockSpec((1, rows // n_steps, cols), slab_index))
        slab_shapes.append(jax.ShapeDtypeStruct((n_steps, rows // n_steps, cols), BF16))

    kern = functools.partial(_moba_kernel, n_blocks=n_blocks, heads=hps, n_cast=len(slabs))
    outs = pl.pallas_call(
        kern,
        grid=grid,
        in_specs=[
            pl.BlockSpec((1, hps, tq, HEAD_DIM), lambda b, h, i: (b, h, i, 0)),
            pl.BlockSpec((1, hps, S, HEAD_DIM), lambda b, h, i: (b, k_first + h, 0, 0)),
            pl.BlockSpec((1, hps, n_blocks, MOBA_VT_ROWS, MOBA_BLOCK),
                         lambda b, h, i: (b, h, 0, 0, 0)),
        ] + slab_specs,
        out_specs=[pl.BlockSpec((1, hps, tq, HEAD_DIM), lambda b, h, i: (b, h, i, 0))] + slab_specs,
        out_shape=[jax.ShapeDtypeStruct((B, n_heads_a, S, HEAD_DIM), F32)] + slab_shapes,
        scratch_shapes=[pltpu.VMEM((hps, n_blocks, HEAD_DIM), F32),
                        pltpu.VMEM((hps, n_blocks, tq), F32),
                        pltpu.VMEM((hps, MOBA_VT_ROWS, tq), F32)],
        compiler_params=pltpu.CompilerParams(
            dimension_semantics=("parallel", "parallel", "arbitrary"),
            vmem_limit_bytes=V7X_VMEM_LIMIT_BYTES),
        name="moba_attn",
    )(qkv_heads, qkv_heads, vt, *slabs)
    return outs[0], [o.reshape(w.shape) for o, w in zip(outs[1:], f32_weights)]


def _dilated_kernel(q1_ref, k1_ref, v1_ref, q4_ref, k4_ref, v4_ref, q16_ref, k16_ref, v16_ref,
                    ob_ref, o_scr, lse_scr, bias_scr, *, seq_len, blocks_per_iter, combine_rows):
    blk = DIL_BLOCK
    branch_refs = ((q1_ref, k1_ref, v1_ref), (q4_ref, k4_ref, v4_ref), (q16_ref, k16_ref, v16_ref))

    qi = lax.broadcasted_iota(jnp.int32, (blk, 2 * blk), 0)
    ki = lax.broadcasted_iota(jnp.int32, (blk, 2 * blk), 1)
    dist = qi + blk - ki
    bias_scr[0] = jnp.where(jnp.logical_and(dist >= 0, dist <= blk), 0.0, NEG).astype(F32)
    bias_scr[1] = jnp.where(ki <= qi, 0.0, NEG).astype(F32)

    for g, (window, d) in enumerate(DIL_PAIRS):
        assert window // d == DIL_BLOCK
        q_ref, k_ref, v_ref = branch_refs[g]
        n_blk = seq_len // d // blk

        def rows_of(ref, r, start, size, d=d):
            if d == 1:
                return ref[0, 0, pl.ds(start, size), :]
            return ref[0, 0, r, pl.ds(start, size), :]

        def key_start(n):
            return pl.multiple_of(jnp.maximum(n - 1, 0) * blk, blk)

        def scores(r, n, q_ref=q_ref, k_ref=k_ref, rows_of=rows_of):
            qb = rows_of(q_ref, r, pl.multiple_of(n * blk, blk), blk)
            kb = rows_of(k_ref, r, key_start(n), 2 * blk)
            return lax.dot_general(qb, kb, NT_DIMS, preferred_element_type=F32)

        def finish(r, n, s, g=g, d=d, v_ref=v_ref, rows_of=rows_of):
            t = s + bias_scr[jnp.where(n == 0, 1, 0)]
            m = jnp.max(t, axis=-1, keepdims=True)
            p = jnp.exp2(t - m)
            vb = rows_of(v_ref, r, key_start(n), 2 * blk)
            v_ones = jnp.concatenate([vb, jnp.ones_like(vb)], axis=1)
            pv = jnp.dot(p.astype(BF16), v_ones, preferred_element_type=F32)
            den = pv[:, HEAD_DIM:]
            q_start = pl.multiple_of(n * blk, blk)
            if d == 1:
                rows = pl.ds(q_start, blk)
            elif d == 4:
                rows = pl.ds(r * (seq_len // 4) + q_start, blk)
            else:
                r4, m4 = r % 4, r // 4
                rows = pl.ds(r4 * (seq_len // 4) + 4 * q_start + m4, blk, stride=4)
            o_scr[g, rows, :] = pv[:, :HEAD_DIM] / den
            lse_scr[g, rows, :] = m + jnp.log2(den)

        def run_blocks(tasks, scores=scores, finish=finish):
            nxt = scores(*tasks[0])
            for idx, (r, n) in enumerate(tasks):
                cur = nxt
                if idx + 1 < len(tasks):
                    nxt = scores(*tasks[idx + 1])
                finish(r, n, cur)

        n_per_iter = max(1, min(blocks_per_iter // d, n_blk))

        def body(it, carry, run_blocks=run_blocks, d=d, n_per_iter=n_per_iter):
            run_blocks([(r, it * n_per_iter + u) for u in range(n_per_iter) for r in range(d)])
            return carry

        lax.fori_loop(0, n_blk // n_per_iter, body, 0)

    def combine(c, carry):
        i0 = pl.multiple_of(c * combine_rows, combine_rows)
        for r4 in range(4):
            tok = pl.ds(4 * i0 + r4, combine_rows, stride=4)
            cls = pl.ds(r4 * (seq_len // 4) + i0, combine_rows)
            l1, l2, l3 = lse_scr[0, tok, :], lse_scr[1, cls, :], lse_scr[2, cls, :]
            lmax = jnp.maximum(jnp.maximum(l1, l2), l3)
            e1, e2, e3 = jnp.exp2(l1 - lmax), jnp.exp2(l2 - lmax), jnp.exp2(l3 - lmax)
            num = e1 * o_scr[0, tok, :] + e2 * o_scr[1, cls, :] + e3 * o_scr[2, cls, :]
            ob_ref[0, 0, tok, :] = num / (e1 + e2 + e3)
        return carry

    lax.fori_loop(0, seq_len // 4 // combine_rows, combine, 0)


def _dilated_call(qkv_heads, d4, d16, *, n_heads_b, q_off, k_off, v_off, blocks_per_iter=16):
    B, _, S, _ = qkv_heads.shape
    offs_nat = (q_off, k_off, v_off)
    offs_dil = (0, n_heads_b, 2 * n_heads_b)
    in_arrays = [qkv_heads] * 3 + [d4] * 3 + [d16] * 3
    in_specs = (
        [pl.BlockSpec((1, 1, S, HEAD_DIM), lambda b, h, off=off: (b, off + h, 0, 0))
         for off in offs_nat]
        + [pl.BlockSpec((1, 1, 4, S // 4, HEAD_DIM), lambda b, h, off=off: (b, off + h, 0, 0, 0))
           for off in offs_dil]
        + [pl.BlockSpec((1, 1, 16, S // 16, HEAD_DIM), lambda b, h, off=off: (b, off + h, 0, 0, 0))
           for off in offs_dil])
    n_br = len(DIL_PAIRS)
    return pl.pallas_call(
        functools.partial(_dilated_kernel, seq_len=S, blocks_per_iter=blocks_per_iter,
                          combine_rows=64),
        grid=(B, n_heads_b),
        in_specs=in_specs,
        out_specs=pl.BlockSpec((1, 1, S, HEAD_DIM), lambda b, h: (b, h, 0, 0)),
        out_shape=jax.ShapeDtypeStruct((B, n_heads_b, S, HEAD_DIM), F32),
        scratch_shapes=[pltpu.VMEM((n_br, S, HEAD_DIM), F32),
                        pltpu.VMEM((n_br, S, HEAD_DIM), F32),
                        pltpu.VMEM((2, DIL_BLOCK, 2 * DIL_BLOCK), F32)],
        compiler_params=pltpu.CompilerParams(
            dimension_semantics=("parallel", "parallel"),
            vmem_limit_bytes=V7X_VMEM_LIMIT_BYTES),
        name="dilated_attn",
    )(*in_arrays)


def _attnout_kernel(x_ref, oa_ref, ob_ref, ga_ref, gb_ref, wo_ref, pg_ref, fg_ref,
                    out_ref, hn_ref, mix_ref, *, row_chunks):
    tm = x_ref.shape[1]
    rows_per_chunk = tm // row_chunks
    for c in range(row_chunks):
        rows = slice(c * rows_per_chunk, (c + 1) * rows_per_chunk)
        col = 0
        for o_ref, g_ref in ((oa_ref, ga_ref), (ob_ref, gb_ref)):
            n_heads = o_ref.shape[1]
            ssq = None
            for h in range(n_heads):
                t = o_ref[0, h, rows, :]
                part = jnp.sum(t * t, axis=-1, keepdims=True)
                ssq = part if ssq is None else ssq + part
            inv = lax.rsqrt(ssq / (n_heads * HEAD_DIM) + RMS_EPS)
            for h in range(n_heads):
                gs = slice(h * HEAD_DIM, (h + 1) * HEAD_DIM)
                mix_ref[rows, col:col + HEAD_DIM] = (
                    o_ref[0, h, rows, :] * inv * g_ref[:, gs]).astype(BF16)
                col += HEAD_DIM
        y = jnp.dot(mix_ref[rows, :], wo_ref[...], preferred_element_type=F32)
        x_new = x_ref[0, rows, :] + y * _rms_scale(y) * pg_ref[...]
        out_ref[0, rows, :] = x_new
        hn_ref[0, rows, :] = (x_new * _rms_scale(x_new) * fg_ref[...]).astype(BF16)


def _attnout_call(x, oa, ob, ga, gb, wo_bf16, pg, ffn_g, *, tm=512, row_chunks=2):
    B, S, D = x.shape
    head_spec_a = pl.BlockSpec((1, oa.shape[1], tm, HEAD_DIM), lambda b, i: (b, 0, i, 0))
    head_spec_b = pl.BlockSpec((1, ob.shape[1], tm, HEAD_DIM), lambda b, i: (b, 0, i, 0))
    row_spec = pl.BlockSpec((1, tm, D), lambda b, i: (b, i, 0))
    return pl.pallas_call(
        functools.partial(_attnout_kernel, row_chunks=row_chunks),
        grid=(B, S // tm),
        in_specs=[
            row_spec,
            head_spec_a, head_spec_b,
            pl.BlockSpec((1, ga.shape[1]), lambda b, i: (0, 0)),
            pl.BlockSpec((1, gb.shape[1]), lambda b, i: (0, 0)),
            pl.BlockSpec(wo_bf16.shape, lambda b, i: (0, 0)),
            pl.BlockSpec((1, D), lambda b, i: (0, 0)),
            pl.BlockSpec((1, D), lambda b, i: (0, 0)),
        ],
        out_specs=[row_spec, row_spec],
        out_shape=[jax.ShapeDtypeStruct((B, S, D), F32), jax.ShapeDtypeStruct((B, S, D), BF16)],
        scratch_shapes=[pltpu.VMEM((tm, wo_bf16.shape[0]), BF16)],
        compiler_params=pltpu.CompilerParams(
            dimension_semantics=("parallel", "parallel"),
            vmem_limit_bytes=V7X_VMEM_LIMIT_BYTES),
        name="attn_out",
    )(x, oa, ob, ga, gb, wo_bf16, pg, ffn_g)


def _ffn_kernel(x_ref, hn_ref, wg_ref, wu_ref, cw_ref, cb_ref, wd_ref, pg_ref,
                out_ref, tail_ref, act_ref):
    i = pl.program_id(1)
    f = pl.program_id(2)
    tm = x_ref.shape[1]
    halo = tail_ref.shape[1]

    @pl.when(i == 0)
    def _():
        tail_ref[f] = jnp.zeros(tail_ref.shape[1:], F32)

    @pl.when(f == 0)
    def _():
        out_ref[0] = jnp.zeros(out_ref.shape[1:], F32)

    def step(is_last):
        gate = jnp.dot(hn_ref[0], wg_ref[...], preferred_element_type=F32)
        up = jnp.dot(hn_ref[0], wu_ref[...], preferred_element_type=F32)
        cw = cw_ref[...]
        cb = cb_ref[...]

        def gated(g_m2, g_m1, g_0, u):
            gc = cb + g_m2 * cw[0:1, :] + g_m1 * cw[1:2, :] + g_0 * cw[2:3, :]
            gelu = 0.5 * gc * (1.0 + jnp.tanh(0.7978845608028654 * (gc + 0.044715 * (gc * gc * gc))))
            return (gelu * u).astype(BF16)

        act_ref[...] = gated(pltpu.roll(gate, 2, 0), pltpu.roll(gate, 1, 0), gate, up)
        head = BF16_SUBLANE_TILE
        ext = jnp.concatenate([tail_ref[f], gate[0:head]], axis=0)
        act_ref[0:head, :] = gated(ext[halo - 2:halo - 2 + head], ext[halo - 1:halo - 1 + head],
                                   gate[0:head], up[0:head])
        tail_ref[f] = gate[tm - halo:, :]
        if not is_last:
            out_ref[0] += jnp.dot(act_ref[...], wd_ref[...], preferred_element_type=F32)
        else:
            half = tm // FFN_LAST_STEP_ROW_CHUNKS
            for r0 in range(0, tm, half):
                rows = slice(r0, r0 + half)
                y = out_ref[0, rows, :] + jnp.dot(act_ref[rows, :], wd_ref[...],
                                                  preferred_element_type=F32)
                out_ref[0, rows, :] = x_ref[0, rows, :] + y * _rms_scale(y) * pg_ref[...]

    last = pl.num_programs(2) - 1
    pl.when(f < last)(lambda: step(False))
    pl.when(f == last)(lambda: step(True))


def _ffn_call(x, hn, wg_bf16, wu_bf16, conv_w, conv_b, wd_bf16, pg, *, tm=512, tf=1024):
    B, S, D = x.shape
    d_ff = wg_bf16.shape[1]
    halo = F32_SUBLANE_TILE
    assert CONV_WIDTH - 1 <= halo
    return pl.pallas_call(
        _ffn_kernel,
        grid=(B, S // tm, d_ff // tf),
        in_specs=[
            pl.BlockSpec((1, tm, D), lambda b, i, f: (b, i, 0)),
            pl.BlockSpec((1, tm, D), lambda b, i, f: (b, i, 0)),
            pl.BlockSpec((D, tf), lambda b, i, f: (0, f)),
            pl.BlockSpec((D, tf), lambda b, i, f: (0, f)),
            pl.BlockSpec((CONV_WIDTH, tf), lambda b, i, f: (0, f)),
            pl.BlockSpec((1, tf), lambda b, i, f: (0, f)),
            pl.BlockSpec((tf, D), lambda b, i, f: (f, 0)),
            pl.BlockSpec((1, D), lambda b, i, f: (0, 0)),
        ],
        out_specs=pl.BlockSpec((1, tm, D), lambda b, i, f: (b, i, 0)),
        out_shape=jax.ShapeDtypeStruct((B, S, D), F32),
        scratch_shapes=[pltpu.VMEM((d_ff // tf, halo, tf), F32), pltpu.VMEM((tm, tf), BF16)],
        compiler_params=pltpu.CompilerParams(
            dimension_semantics=("parallel", "arbitrary", "arbitrary"),
            vmem_limit_bytes=V7X_VMEM_LIMIT_BYTES),
        name="conv_glu_ffn",
    )(x, hn, wg_bf16, wu_bf16, conv_w, conv_b, wd_bf16, pg)


def _rope_tables(positions):
    inv_freq = ROPE_THETA ** (-jnp.arange(ROT_HALF, dtype=F32) / ROT_HALF)
    batch, seq = positions.shape
    per_row = HEAD_DIM // ROT_HALF
    pos = positions.astype(F32).reshape(batch, seq // per_row, per_row, 1)
    ang = (pos * inv_freq).reshape(batch, seq // per_row, HEAD_DIM)
    cos, sin = lax.optimization_barrier((jnp.cos(ang), jnp.sin(ang)))
    cos = cos.reshape(batch, seq, ROT_HALF)
    sin = sin.reshape(batch, seq, ROT_HALF)
    ones = jnp.ones((batch, seq, HEAD_DIM - ROT_DIM), F32)
    zeros_tail = jnp.zeros((batch, seq, HEAD_DIM - ROT_HALF), F32)
    zeros_head = jnp.zeros_like(sin)
    cos_t = jnp.concatenate([cos, cos, ones], axis=-1)
    sa_t = jnp.concatenate([-sin, zeros_tail], axis=-1)
    sb_t = jnp.concatenate([zeros_head, sin, zeros_tail[..., :HEAD_DIM - ROT_DIM]], axis=-1)
    return cos_t, sa_t, sb_t


def kernel(x, positions, attn_pre_g, w_qkv, moba_out_g, dil_out_g, w_o, attn_post_g, ffn_pre_g,
           w_gate, w_up, conv_w, conv_b, w_down, ffn_post_g):
    depth = w_qkv.shape[0]
    n_heads_a = moba_out_g.shape[1] // HEAD_DIM
    n_heads_b = dil_out_g.shape[1] // HEAD_DIM
    cos_t, sa_t, sb_t = _rope_tables(positions)
    for l in range(depth):
        qkv_heads, vt, d4, d16 = _qkv_call(x, attn_pre_g[l][None], w_qkv[l].astype(BF16),
                                           cos_t, sa_t, sb_t,
                                           n_heads_a=n_heads_a, n_heads_b=n_heads_b)
        oa, (wo_bf16, wg_bf16, wu_bf16, wd_bf16) = _moba_call(
            qkv_heads, vt, [w_o[l], w_gate[l], w_up[l], w_down[l]], n_heads_a=n_heads_a)
        ob = _dilated_call(qkv_heads, d4, d16, n_heads_b=n_heads_b, q_off=3 * n_heads_a,
                           k_off=3 * n_heads_a + n_heads_b, v_off=3 * n_heads_a + 2 * n_heads_b)
        x, hn = _attnout_call(x, oa, ob, moba_out_g[l][None], dil_out_g[l][None],
                              wo_bf16, attn_post_g[l][None], ffn_pre_g[l][None])
        x = _ffn_call(x, hn, wg_bf16, wu_bf16, conv_w[l], conv_b[l][None], wd_bf16,
                      ffn_post_g[l][None])
    return x
```

```python
import functools

import jax
import jax.numpy as jnp
from jax import lax
from jax.experimental import pallas as pl
from jax.experimental.pallas import tpu as pltpu

F32 = jnp.float32
BF16 = jnp.bfloat16

HEAD_DIM = 128
ROT_DIM = HEAD_DIM // 4
ROT_HALF = ROT_DIM // 2
ROPE_THETA = 500000.0
MOBA_BLOCK = 256
MOBA_TOPK = 3
DIL_PAIRS = ((128, 1), (512, 4), (2048, 16))
DIL_BLOCK = 128
CONV_WIDTH = 3
RMS_EPS = 1e-6
SCALE = HEAD_DIM ** -0.5
SCALE_LOG2E = SCALE * 1.4426950408889634
NEG = -1e30

V7X_VMEM_LIMIT_BYTES = 56 * 1024 * 1024
BF16_SUBLANE_TILE = 16
V7X_MXU_WIDTH = 256
F32_SUBLANE_TILE = 8
FFN_LAST_STEP_ROW_CHUNKS = 2
QKV_SLAB_SLOTS = 2
MOBA_VT_ROWS = HEAD_DIM + BF16_SUBLANE_TILE

NT_DIMS = (((1,), (1,)), ((), ()))


def _rms_scale(x):
    return lax.rsqrt(jnp.mean(x * x, axis=-1, keepdims=True) + RMS_EPS)


def _qkv_kernel(x_ref, g_ref, w_ref, cos_ref, sa_ref, sb_ref,
                nat_ref, vt_ref, d4_ref, d16_ref, hn_ref, slab_ref, slab4_ref, *, heads_per_seg):
    tm = hn_ref.shape[0]
    heads_per_dot = V7X_MXU_WIDTH // HEAD_DIM
    dot_width = heads_per_dot * HEAD_DIM
    seg_width = heads_per_seg * HEAD_DIM

    x = x_ref[0]
    hn_ref[...] = (x * _rms_scale(x) * g_ref[...]).astype(BF16)

    def make_rope(scale):
        cos, sa, sb = cos_ref[0] * scale, sa_ref[0] * scale, sb_ref[0] * scale

        def rope(t):
            return (t * cos + pltpu.roll(t, HEAD_DIM - ROT_HALF, 1) * sa
                    + pltpu.roll(t, ROT_HALF, 1) * sb)
        return rope

    rope_q, rope_k = make_rope(SCALE_LOG2E), make_rope(1.0)

    for seg in range(6):
        rope = {0: rope_q, 1: rope_k, 3: rope_q, 4: rope_k}.get(seg)
        with_vt = seg == 2
        with_dilated = seg >= 3
        for c in range(heads_per_seg // heads_per_dot):
            col = seg * seg_width + c * dot_width
            acc = jnp.dot(hn_ref[...], w_ref[:, col:col + dot_width], preferred_element_type=F32)
            for hh in range(heads_per_dot):
                h = c * heads_per_dot + hh
                t = acc[:, hh * HEAD_DIM:(hh + 1) * HEAD_DIM]
                if rope is not None:
                    t = rope(t)
                nat_ref[0, seg * heads_per_seg + h] = t.astype(BF16)
                if with_vt:
                    for blk in range(tm // MOBA_BLOCK):
                        vt_ref[0, h, blk, 0:HEAD_DIM, :] = (
                            t[blk * MOBA_BLOCK:(blk + 1) * MOBA_BLOCK].T.astype(BF16))
                        vt_ref[0, h, blk, HEAD_DIM:, :] = jnp.ones(
                            (MOBA_VT_ROWS - HEAD_DIM, MOBA_BLOCK), BF16)
                if with_dilated:
                    hd = (seg - 3) * heads_per_seg + h
                    slot = h % slab_ref.shape[0]
                    q4 = tm // 4
                    slab_ref[slot] = t
                    for r4 in range(4):
                        cls = slab_ref[slot, pl.ds(r4, q4, stride=4), :]
                        d4_ref[0, hd, r4] = cls.astype(BF16)
                        slab4_ref[slot, r4 * q4:(r4 + 1) * q4, :] = cls
                    for r4 in range(4):
                        for m in range(4):
                            cls = slab4_ref[slot, pl.ds(r4 * q4 + m, q4 // 4, stride=4), :]
                            d16_ref[0, hd, r4 + 4 * m] = cls.astype(BF16)


def _qkv_call(x, g, w_bf16, cos_t, sa_t, sb_t, *, n_heads_a, n_heads_b, tm=256):
    B, S, D = x.shape
    N = w_bf16.shape[1]
    assert n_heads_a == n_heads_b and N == 3 * (n_heads_a + n_heads_b) * HEAD_DIM
    assert tm % MOBA_BLOCK == 0 and (tm // 16) % BF16_SUBLANE_TILE == 0
    n_col_heads = N // HEAD_DIM
    kern = functools.partial(_qkv_kernel, heads_per_seg=n_heads_a)
    row_tab = pl.BlockSpec((1, tm, HEAD_DIM), lambda b, i: (b, i, 0))
    return pl.pallas_call(
        kern,
        grid=(B, S // tm),
        in_specs=[
            pl.BlockSpec((1, tm, D), lambda b, i: (b, i, 0)),
            pl.BlockSpec((1, D), lambda b, i: (0, 0)),
            pl.BlockSpec((D, N), lambda b, i: (0, 0), pipeline_mode=pl.Buffered(1)),
            row_tab, row_tab, row_tab,
        ],
        out_specs=[
            pl.BlockSpec((1, n_col_heads, tm, HEAD_DIM), lambda b, i: (b, 0, i, 0)),
            pl.BlockSpec((1, n_heads_a, tm // MOBA_BLOCK, MOBA_VT_ROWS, MOBA_BLOCK),
                         lambda b, i: (b, 0, i, 0, 0)),
            pl.BlockSpec((1, 3 * n_heads_b, 4, tm // 4, HEAD_DIM), lambda b, i: (b, 0, 0, i, 0)),
            pl.BlockSpec((1, 3 * n_heads_b, 16, tm // 16, HEAD_DIM), lambda b, i: (b, 0, 0, i, 0)),
        ],
        out_shape=[
            jax.ShapeDtypeStruct((B, n_col_heads, S, HEAD_DIM), BF16),
            jax.ShapeDtypeStruct((B, n_heads_a, S // MOBA_BLOCK, MOBA_VT_ROWS, MOBA_BLOCK), BF16),
            jax.ShapeDtypeStruct((B, 3 * n_heads_b, 4, S // 4, HEAD_DIM), BF16),
            jax.ShapeDtypeStruct((B, 3 * n_heads_b, 16, S // 16, HEAD_DIM), BF16),
        ],
        scratch_shapes=[pltpu.VMEM((tm, D), BF16),
                        pltpu.VMEM((QKV_SLAB_SLOTS, tm, HEAD_DIM), F32),
                        pltpu.VMEM((QKV_SLAB_SLOTS, tm, HEAD_DIM), F32)],
        compiler_params=pltpu.CompilerParams(
            dimension_semantics=("parallel", "parallel"),
            vmem_limit_bytes=V7X_VMEM_LIMIT_BYTES),
        name="qkv_rope",
    )(x, g, w_bf16, cos_t, sa_t, sb_t)


def _moba_kernel(*refs, n_blocks, heads, n_cast, lookahead=8, group=2):
    q_ref, k_ref, vt_ref = refs[:3]
    cast_in = refs[3:3 + n_cast]
    o_ref = refs[3 + n_cast]
    cast_out = refs[4 + n_cast:4 + 2 * n_cast]
    kmean_ref, bias_ref, acc_ref = refs[4 + 2 * n_cast:]
    assert n_blocks % group == 0
    i = pl.program_id(2)
    blk_sz = MOBA_BLOCK

    for w_in, w_out in zip(cast_in, cast_out):
        w_out[0] = w_in[0].astype(BF16)

    @pl.when(i == 0)
    def _():
        for h in range(heads):
            for blk in range(n_blocks):
                kb = k_ref[0, h, blk * blk_sz:(blk + 1) * blk_sz, :].astype(F32)
                kmean_ref[h, blk:blk + 1, :] = jnp.mean(kb, axis=0, keepdims=True)

    def select_blocks(h, q):
        km = kmean_ref[h]
        km_hi = km.astype(BF16)
        km_lo = (km - km_hi.astype(F32)).astype(BF16)
        gate = (lax.dot_general(km_hi, q, NT_DIMS, preferred_element_type=F32)
                + lax.dot_general(km_lo, q, NT_DIMS, preferred_element_type=F32))
        blk_id = lax.broadcasted_iota(jnp.int32, gate.shape, 0).astype(F32)
        neg_inf = jnp.float32(-jnp.inf)
        g = jnp.where(blk_id < i.astype(F32), gate, neg_inf)
        sel = jnp.zeros(gate.shape, dtype=jnp.bool_)
        for _ in range(MOBA_TOPK):
            m = jnp.max(g, axis=0, keepdims=True)
            first = jnp.min(jnp.where(g == m, blk_id, float(n_blocks)), axis=0, keepdims=True)
            pick = jnp.logical_and(blk_id == first, m > neg_inf)
            sel = jnp.logical_or(sel, pick)
            g = jnp.where(pick, neg_inf, g)
        bias_ref[h] = jnp.where(sel, 0.0, NEG).astype(F32)

    def scores(h, first_blk, n_blk):
        rows = n_blk * blk_sz
        kb = k_ref[0, h, pl.ds(pl.multiple_of(first_blk * blk_sz, blk_sz), rows), :]
        return lax.dot_general(kb, q_ref[0, h], NT_DIMS, preferred_element_type=F32)

    def pipelined(stage_a, stage_b):
        ahead = [stage_a(h) for h in range(min(lookahead, heads))]
        outs = []
        for h in range(heads):
            if h + lookahead < heads:
                ahead.append(stage_a(h + lookahead))
            outs.append(stage_b(h, ahead[h]))
        return outs

    def own_a(h):
        select_blocks(h, q_ref[0, h])
        return scores(h, i, 1)

    def own_b(h, s):
        key_pos = lax.broadcasted_iota(jnp.int32, s.shape, 0)
        q_pos = lax.broadcasted_iota(jnp.int32, s.shape, 1)
        t = jnp.where(key_pos <= q_pos, s, NEG)
        m0 = jnp.max(t, axis=0, keepdims=True)
        p = jnp.exp2(t - m0)
        acc_ref[h] = jnp.dot(vt_ref[0, h, i], p.astype(BF16), preferred_element_type=F32)
        return m0

    init = pipelined(own_a, own_b)

    def body(c, carry):
        first = c * group

        def past_b(h, s):
            m_prev = carry[h]
            chunks = [s[g * blk_sz:(g + 1) * blk_sz] for g in range(group)]
            brows = [bias_ref[h, pl.ds(first + g, 1), :] for g in range(group)]
            m_new = m_prev
            for sg, brow in zip(chunks, brows):
                m_new = jnp.maximum(m_new, jnp.max(sg, axis=0, keepdims=True) + brow)
            acc_new = jnp.exp2(m_prev - m_new) * acc_ref[h]
            for g, (sg, brow) in enumerate(zip(chunks, brows)):
                pg = jnp.exp2(sg - (m_new - brow))
                acc_new = acc_new + jnp.dot(vt_ref[0, h, first + g], pg.astype(BF16),
                                            preferred_element_type=F32)
            acc_ref[h] = acc_new
            return m_new

        return tuple(pipelined(lambda h: scores(h, first, group), past_b))

    lax.fori_loop(0, (i + group - 1) // group, body, tuple(init))
    for h in range(heads):
        acc_fin = acc_ref[h]
        o_ref[0, h] = (acc_fin[:HEAD_DIM] / acc_fin[HEAD_DIM:HEAD_DIM + 1]).T


def _moba_call(qkv_heads, vt, f32_weights, *, n_heads_a, heads_per_step=8):
    B, _, S, _ = qkv_heads.shape
    n_blocks = S // MOBA_BLOCK
    tq = MOBA_BLOCK
    hps = heads_per_step
    assert n_heads_a % hps == 0
    n_groups = n_heads_a // hps
    k_first = n_groups
    grid = (B, n_groups, S // tq)
    n_steps = grid[0] * grid[1] * grid[2]

    def slab_index(b, h, i):
        return ((b * grid[1] + h) * grid[2] + i, 0, 0)

    slabs, slab_specs, slab_shapes = [], [], []
    for w in f32_weights:
        rows, cols = w.shape
        assert rows % (n_steps * BF16_SUBLANE_TILE) == 0
        slabs.append(w.reshape(n_steps, rows // n_steps, cols))
        slab_specs.append(pl.BlockSpec((1, rows // n_steps, cols), slab_index))
        slab_shapes.append(jax.ShapeDtypeStruct((n_steps, rows // n_steps, cols), BF16))

    kern = functools.partial(_moba_kernel, n_blocks=n_blocks, heads=hps, n_cast=len(slabs))
    outs = pl.pallas_call(
        kern,
        grid=grid,
        in_specs=[
            pl.BlockSpec((1, hps, tq, HEAD_DIM), lambda b, h, i: (b, h, i, 0)),
            pl.BlockSpec((1, hps, S, HEAD_DIM), lambda b, h, i: (b, k_first + h, 0, 0)),
            pl.BlockSpec((1, hps, n_blocks, MOBA_VT_ROWS, MOBA_BLOCK),
                         lambda b, h, i: (b, h, 0, 0, 0)),
        ] + slab_specs,
        out_specs=[pl.BlockSpec((1, hps, tq, HEAD_DIM), lambda b, h, i: (b, h, i, 0))] + slab_specs,
        out_shape=[jax.ShapeDtypeStruct((B, n_heads_a, S, HEAD_DIM), F32)] + slab_shapes,
        scratch_shapes=[pltpu.VMEM((hps, n_blocks, HEAD_DIM), F32),
                        pltpu.VMEM((hps, n_blocks, tq), F32),
                        pltpu.VMEM((hps, MOBA_VT_ROWS, tq), F32)],
        compiler_params=pltpu.CompilerParams(
            dimension_semantics=("parallel", "parallel", "arbitrary"),
            vmem_limit_bytes=V7X_VMEM_LIMIT_BYTES),
        name="moba_attn",
    )(qkv_heads, qkv_heads, vt, *slabs)
    return outs[0], [o.reshape(w.shape) for o, w in zip(outs[1:], f32_weights)]


def _dilated_kernel(q1_ref, k1_ref, v1_ref, q4_ref, k4_ref, v4_ref, q16_ref, k16_ref, v16_ref,
                    ob_ref, o_scr, lse_scr, bias_scr, *, seq_len, blocks_per_iter, combine_rows):
    blk = DIL_BLOCK
    branch_refs = ((q1_ref, k1_ref, v1_ref), (q4_ref, k4_ref, v4_ref), (q16_ref, k16_ref, v16_ref))

    qi = lax.broadcasted_iota(jnp.int32, (blk, 2 * blk), 0)
    ki = lax.broadcasted_iota(jnp.int32, (blk, 2 * blk), 1)
    dist = qi + blk - ki
    bias_scr[0] = jnp.where(jnp.logical_and(dist >= 0, dist <= blk), 0.0, NEG).astype(F32)
    bias_scr[1] = jnp.where(ki <= qi, 0.0, NEG).astype(F32)

    for g, (window, d) in enumerate(DIL_PAIRS):
        assert window // d == DIL_BLOCK
        q_ref, k_ref, v_ref = branch_refs[g]
        n_blk = seq_len // d // blk

        def rows_of(ref, r, start, size, d=d):
            if d == 1:
                return ref[0, 0, pl.ds(start, size), :]
            return ref[0, 0, r, pl.ds(start, size), :]

        def key_start(n):
            return pl.multiple_of(jnp.maximum(n - 1, 0) * blk, blk)

        def scores(r, n, q_ref=q_ref, k_ref=k_ref, rows_of=rows_of):
            qb = rows_of(q_ref, r, pl.multiple_of(n * blk, blk), blk)
            kb = rows_of(k_ref, r, key_start(n), 2 * blk)
            return lax.dot_general(qb, kb, NT_DIMS, preferred_element_type=F32)

        def finish(r, n, s, g=g, d=d, v_ref=v_ref, rows_of=rows_of):
            t = s + bias_scr[jnp.where(n == 0, 1, 0)]
            m = jnp.max(t, axis=-1, keepdims=True)
            p = jnp.exp2(t - m)
            vb = rows_of(v_ref, r, key_start(n), 2 * blk)
            v_ones = jnp.concatenate([vb, jnp.ones_like(vb)], axis=1)
            pv = jnp.dot(p.astype(BF16), v_ones, preferred_element_type=F32)
            den = pv[:, HEAD_DIM:]
            q_start = pl.multiple_of(n * blk, blk)
            if d == 1:
                rows = pl.ds(q_start, blk)
            elif d == 4:
                rows = pl.ds(r * (seq_len // 4) + q_start, blk)
            else:
                r4, m4 = r % 4, r // 4
                rows = pl.ds(r4 * (seq_len // 4) + 4 * q_start + m4, blk, stride=4)
            o_scr[g, rows, :] = pv[:, :HEAD_DIM] / den
            lse_scr[g, rows, :] = m + jnp.log2(den)

        def run_blocks(tasks, scores=scores, finish=finish):
            nxt = scores(*tasks[0])
            for idx, (r, n) in enumerate(tasks):
                cur = nxt
                if idx + 1 < len(tasks):
                    nxt = scores(*tasks[idx + 1])
                finish(r, n, cur)

        n_per_iter = max(1, min(blocks_per_iter // d, n_blk))

        def body(it, carry, run_blocks=run_blocks, d=d, n_per_iter=n_per_iter):
            run_blocks([(r, it * n_per_iter + u) for u in range(n_per_iter) for r in range(d)])
            return carry

        lax.fori_loop(0, n_blk // n_per_iter, body, 0)

    def combine(c, carry):
        i0 = pl.multiple_of(c * combine_rows, combine_rows)
        for r4 in range(4):
            tok = pl.ds(4 * i0 + r4, combine_rows, stride=4)
            cls = pl.ds(r4 * (seq_len // 4) + i0, combine_rows)
            l1, l2, l3 = lse_scr[0, tok, :], lse_scr[1, cls, :], lse_scr[2, cls, :]
            lmax = jnp.maximum(jnp.maximum(l1, l2), l3)
            e1, e2, e3 = jnp.exp2(l1 - lmax), jnp.exp2(l2 - lmax), jnp.exp2(l3 - lmax)
            num = e1 * o_scr[0, tok, :] + e2 * o_scr[1, cls, :] + e3 * o_scr[2, cls, :]
            ob_ref[0, 0, tok, :] = num / (e1 + e2 + e3)
        return carry

    lax.fori_loop(0, seq_len // 4 // combine_rows, combine, 0)


def _dilated_call(qkv_heads, d4, d16, *, n_heads_b, q_off, k_off, v_off, blocks_per_iter=16):
    B, _, S, _ = qkv_heads.shape
    offs_nat = (q_off, k_off, v_off)
    offs_dil = (0, n_heads_b, 2 * n_heads_b)
    in_arrays = [qkv_heads] * 3 + [d4] * 3 + [d16] * 3
    in_specs = (
        [pl.BlockSpec((1, 1, S, HEAD_DIM), lambda b, h, off=off: (b, off + h, 0, 0))
         for off in offs_nat]
        + [pl.BlockSpec((1, 1, 4, S // 4, HEAD_DIM), lambda b, h, off=off: (b, off + h, 0, 0, 0))
           for off in offs_dil]
        + [pl.BlockSpec((1, 1, 16, S // 16, HEAD_DIM), lambda b, h, off=off: (b, off + h, 0, 0, 0))
           for off in offs_dil])
    n_br = len(DIL_PAIRS)
    return pl.pallas_call(
        functools.partial(_dilated_kernel, seq_len=S, blocks_per_iter=blocks_per_iter,
                          combine_rows=64),
        grid=(B, n_heads_b),
        in_specs=in_specs,
        out_specs=pl.BlockSpec((1, 1, S, HEAD_DIM), lambda b, h: (b, h, 0, 0)),
        out_shape=jax.ShapeDtypeStruct((B, n_heads_b, S, HEAD_DIM), F32),
        scratch_shapes=[pltpu.VMEM((n_br, S, HEAD_DIM), F32),
                        pltpu.VMEM((n_br, S, HEAD_DIM), F32),
                        pltpu.VMEM((2, DIL_BLOCK, 2 * DIL_BLOCK), F32)],
        compiler_params=pltpu.CompilerParams(
            dimension_semantics=("parallel", "parallel"),
            vmem_limit_bytes=V7X_VMEM_LIMIT_BYTES),
        name="dilated_attn",
    )(*in_arrays)


def _attnout_kernel(x_ref, oa_ref, ob_ref, ga_ref, gb_ref, wo_ref, pg_ref, fg_ref,
                    out_ref, hn_ref, mix_ref, *, row_chunks):
    tm = x_ref.shape[1]
    rows_per_chunk = tm // row_chunks
    for c in range(row_chunks):
        rows = slice(c * rows_per_chunk, (c + 1) * rows_per_chunk)
        col = 0
        for o_ref, g_ref in ((oa_ref, ga_ref), (ob_ref, gb_ref)):
            n_heads = o_ref.shape[1]
            ssq = None
            for h in range(n_heads):
                t = o_ref[0, h, rows, :]
                part = jnp.sum(t * t, axis=-1, keepdims=True)
                ssq = part if ssq is None else ssq + part
            inv = lax.rsqrt(ssq / (n_heads * HEAD_DIM) + RMS_EPS)
            for h in range(n_heads):
                gs = slice(h * HEAD_DIM, (h + 1) * HEAD_DIM)
                mix_ref[rows, col:col + HEAD_DIM] = (
                    o_ref[0, h, rows, :] * inv * g_ref[:, gs]).astype(BF16)
                col += HEAD_DIM
        y = jnp.dot(mix_ref[rows, :], wo_ref[...], preferred_element_type=F32)
        x_new = x_ref[0, rows, :] + y * _rms_scale(y) * pg_ref[...]
        out_ref[0, rows, :] = x_new
        hn_ref[0, rows, :] = (x_new * _rms_scale(x_new) * fg_ref[...]).astype(BF16)


def _attnout_call(x, oa, ob, ga, gb, wo_bf16, pg, ffn_g, *, tm=512, row_chunks=2):
    B, S, D = x.shape
    head_spec_a = pl.BlockSpec((1, oa.shape[1], tm, HEAD_DIM), lambda b, i: (b, 0, i, 0))
    head_spec_b = pl.BlockSpec((1, ob.shape[1], tm, HEAD_DIM), lambda b, i: (b, 0, i, 0))
    row_spec = pl.BlockSpec((1, tm, D), lambda b, i: (b, i, 0))
    return pl.pallas_call(
        functools.partial(_attnout_kernel, row_chunks=row_chunks),
        grid=(B, S // tm),
        in_specs=[
            row_spec,
            head_spec_a, head_spec_b,
            pl.BlockSpec((1, ga.shape[1]), lambda b, i: (0, 0)),
            pl.BlockSpec((1, gb.shape[1]), lambda b, i: (0, 0)),
            pl.BlockSpec(wo_bf16.shape, lambda b, i: (0, 0)),
            pl.BlockSpec((1, D), lambda b, i: (0, 0)),
            pl.BlockSpec((1, D), lambda b, i: (0, 0)),
        ],
        out_specs=[row_spec, row_spec],
        out_shape=[jax.ShapeDtypeStruct((B, S, D), F32), jax.ShapeDtypeStruct((B, S, D), BF16)],
        scratch_shapes=[pltpu.VMEM((tm, wo_bf16.shape[0]), BF16)],
        compiler_params=pltpu.CompilerParams(
            dimension_semantics=("parallel", "parallel"),
            vmem_limit_bytes=V7X_VMEM_LIMIT_BYTES),
        name="attn_out",
    )(x, oa, ob, ga, gb, wo_bf16, pg, ffn_g)


def _ffn_kernel(x_ref, hn_ref, wg_ref, wu_ref, cw_ref, cb_ref, wd_ref, pg_ref,
                out_ref, tail_ref, act_ref):
    i = pl.program_id(1)
    f = pl.program_id(2)
    tm = x_ref.shape[1]
    halo = tail_ref.shape[1]

    @pl.when(i == 0)
    def _():
        tail_ref[f] = jnp.zeros(tail_ref.shape[1:], F32)

    @pl.when(f == 0)
    def _():
        out_ref[0] = jnp.zeros(out_ref.shape[1:], F32)

    def step(is_last):
        gate = jnp.dot(hn_ref[0], wg_ref[...], preferred_element_type=F32)
        up = jnp.dot(hn_ref[0], wu_ref[...], preferred_element_type=F32)
        cw = cw_ref[...]
        cb = cb_ref[...]

        def gated(g_m2, g_m1, g_0, u):
            gc = cb + g_m2 * cw[0:1, :] + g_m1 * cw[1:2, :] + g_0 * cw[2:3, :]
            gelu = 0.5 * gc * (1.0 + jnp.tanh(0.7978845608028654 * (gc + 0.044715 * (gc * gc * gc))))
            return (gelu * u).astype(BF16)

        act_ref[...] = gated(pltpu.roll(gate, 2, 0), pltpu.roll(gate, 1, 0), gate, up)
        head = BF16_SUBLANE_TILE
        ext = jnp.concatenate([tail_ref[f], gate[0:head]], axis=0)
        act_ref[0:head, :] = gated(ext[halo - 2:halo - 2 + head], ext[halo - 1:halo - 1 + head],
                                   gate[0:head], up[0:head])
        tail_ref[f] = gate[tm - halo:, :]
        if not is_last:
            out_ref[0] += jnp.dot(act_ref[...], wd_ref[...], preferred_element_type=F32)
        else:
            half = tm // FFN_LAST_STEP_ROW_CHUNKS
            for r0 in range(0, tm, half):
                rows = slice(r0, r0 + half)
                y = out_ref[0, rows, :] + jnp.dot(act_ref[rows, :], wd_ref[...],
                                                  preferred_element_type=F32)
                out_ref[0, rows, :] = x_ref[0, rows, :] + y * _rms_scale(y) * pg_ref[...]

    last = pl.num_programs(2) - 1
    pl.when(f < last)(lambda: step(False))
    pl.when(f == last)(lambda: step(True))


def _ffn_call(x, hn, wg_bf16, wu_bf16, conv_w, conv_b, wd_bf16, pg, *, tm=512, tf=1024):
    B, S, D = x.shape
    d_ff = wg_bf16.shape[1]
    halo = F32_SUBLANE_TILE
    assert CONV_WIDTH - 1 <= halo
    return pl.pallas_call(
        _ffn_kernel,
        grid=(B, S // tm, d_ff // tf),
        in_specs=[
            pl.BlockSpec((1, tm, D), lambda b, i, f: (b, i, 0)),
            pl.BlockSpec((1, tm, D), lambda b, i, f: (b, i, 0)),
            pl.BlockSpec((D, tf), lambda b, i, f: (0, f)),
            pl.BlockSpec((D, tf), lambda b, i, f: (0, f)),
            pl.BlockSpec((CONV_WIDTH, tf), lambda b, i, f: (0, f)),
            pl.BlockSpec((1, tf), lambda b, i, f: (0, f)),
            pl.BlockSpec((tf, D), lambda b, i, f: (f, 0)),
            pl.BlockSpec((1, D), lambda b, i, f: (0, 0)),
        ],
        out_specs=pl.BlockSpec((1, tm, D), lambda b, i, f: (b, i, 0)),
        out_shape=jax.ShapeDtypeStruct((B, S, D), F32),
        scratch_shapes=[pltpu.VMEM((d_ff // tf, halo, tf), F32), pltpu.VMEM((tm, tf), BF16)],
        compiler_params=pltpu.CompilerParams(
            dimension_semantics=("parallel", "arbitrary", "arbitrary"),
            vmem_limit_bytes=V7X_VMEM_LIMIT_BYTES),
        name="conv_glu_ffn",
    )(x, hn, wg_bf16, wu_bf16, conv_w, conv_b, wd_bf16, pg)


def _rope_tables(positions):
    inv_freq = ROPE_THETA ** (-jnp.arange(ROT_HALF, dtype=F32) / ROT_HALF)
    batch, seq = positions.shape
    per_row = HEAD_DIM // ROT_HALF
    pos = positions.astype(F32).reshape(batch, seq // per_row, per_row, 1)
    ang = (pos * inv_freq).reshape(batch, seq // per_row, HEAD_DIM)
    cos, sin = lax.optimization_barrier((jnp.cos(ang), jnp.sin(ang)))
    cos = cos.reshape(batch, seq, ROT_HALF)
    sin = sin.reshape(batch, seq, ROT_HALF)
    ones = jnp.ones((batch, seq, HEAD_DIM - ROT_DIM), F32)
    zeros_tail = jnp.zeros((batch, seq, HEAD_DIM - ROT_HALF), F32)
    zeros_head = jnp.zeros_like(sin)
    cos_t = jnp.concatenate([cos, cos, ones], axis=-1)
    sa_t = jnp.concatenate([-sin, zeros_tail], axis=-1)
    sb_t = jnp.concatenate([zeros_head, sin, zeros_tail[..., :HEAD_DIM - ROT_DIM]], axis=-1)
    return cos_t, sa_t, sb_t


def kernel(x, positions, attn_pre_g, w_qkv, moba_out_g, dil_out_g, w_o, attn_post_g, ffn_pre_g,
           w_gate, w_up, conv_w, conv_b, w_down, ffn_post_g):
    depth = w_qkv.shape[0]
    n_heads_a = moba_out_g.shape[1] // HEAD_DIM
    n_heads_b = dil_out_g.shape[1] // HEAD_DIM
    cos_t, sa_t, sb_t = _rope_tables(positions)
    for l in range(depth):
        qkv_heads, vt, d4, d16 = _qkv_call(x, attn_pre_g[l][None], w_qkv[l].astype(BF16),
                                           cos_t, sa_t, sb_t,
                                           n_heads_a=n_heads_a, n_heads_b=n_heads_b)
        oa, (wo_bf16, wg_bf16, wu_bf16, wd_bf16) = _moba_call(
            qkv_heads, vt, [w_o[l], w_gate[l], w_up[l], w_down[l]], n_heads_a=n_heads_a)
        ob = _dilated_call(qkv_heads, d4, d16, n_heads_b=n_heads_b, q_off=3 * n_heads_a,
                           k_off=3 * n_heads_a + n_heads_b, v_off=3 * n_heads_a + 2 * n_heads_b)
        x, hn = _attnout_call(x, oa, ob, moba_out_g[l][None], dil_out_g[l][None],
                              wo_bf16, attn_post_g[l][None], ffn_pre_g[l][None])
        x = _ffn_call(x, hn, wg_bf16, wu_bf16, conv_w[l], conv_b[l][None], wd_bf16,
                      ffn_post_g[l][None])
    return x
```

```python
import functools

import jax
import jax.numpy as jnp
from jax import lax
from jax.experimental import pallas as pl
from jax.experimental.pallas import tpu as pltpu

F32 = jnp.float32
BF16 = jnp.bfloat16

HEAD_DIM = 128
ROT_DIM = HEAD_DIM // 4
ROT_HALF = ROT_DIM // 2
ROPE_THETA = 500000.0
MOBA_BLOCK = 256
MOBA_TOPK = 3
DIL_PAIRS = ((128, 1), (512, 4), (2048, 16))
DIL_BLOCK = 128
CONV_WIDTH = 3
RMS_EPS = 1e-6
SCALE = HEAD_DIM ** -0.5
SCALE_LOG2E = SCALE * 1.4426950408889634
NEG = -1e30

V7X_VMEM_LIMIT_BYTES = 56 * 1024 * 1024
BF16_SUBLANE_TILE = 16
V7X_MXU_WIDTH = 256
F32_SUBLANE_TILE = 8
FFN_LAST_STEP_ROW_CHUNKS = 2
QKV_SLAB_SLOTS = 2
MOBA_VT_ROWS = HEAD_DIM + BF16_SUBLANE_TILE
MOBA_LOOP_GROUP = 2

NT_DIMS = (((1,), (1,)), ((), ()))


def _rms_scale(x):
    return lax.rsqrt(jnp.mean(x * x, axis=-1, keepdims=True) + RMS_EPS)


def _qkv_kernel(x_ref, g_ref, w_ref, cos_ref, sa_ref, sb_ref,
                nat_ref, vt_ref, d4_ref, d16_ref, hn_ref, slab_ref, slab4_ref, *, heads_per_seg):
    tm = hn_ref.shape[0]
    heads_per_dot = V7X_MXU_WIDTH // HEAD_DIM
    dot_width = heads_per_dot * HEAD_DIM
    seg_width = heads_per_seg * HEAD_DIM

    x = x_ref[0]
    hn_ref[...] = (x * _rms_scale(x) * g_ref[...]).astype(BF16)

    def make_rope(scale):
        cos, sa, sb = cos_ref[0] * scale, sa_ref[0] * scale, sb_ref[0] * scale

        def rope(t):
            return (t * cos + pltpu.roll(t, HEAD_DIM - ROT_HALF, 1) * sa
                    + pltpu.roll(t, ROT_HALF, 1) * sb)
        return rope

    rope_q, rope_k = make_rope(SCALE_LOG2E), make_rope(1.0)

    for seg in range(6):
        rope = {0: rope_q, 1: rope_k, 3: rope_q, 4: rope_k}.get(seg)
        with_vt = seg == 2
        with_dilated = seg >= 3
        for c in range(heads_per_seg // heads_per_dot):
            col = seg * seg_width + c * dot_width
            acc = jnp.dot(hn_ref[...], w_ref[:, col:col + dot_width], preferred_element_type=F32)
            for hh in range(heads_per_dot):
                h = c * heads_per_dot + hh
                t = acc[:, hh * HEAD_DIM:(hh + 1) * HEAD_DIM]
                if rope is not None:
                    t = rope(t)
                nat_ref[0, seg * heads_per_seg + h] = t.astype(BF16)
                if with_vt:
                    for blk in range(tm // MOBA_BLOCK):
                        vt_ref[0, h, blk, 0:HEAD_DIM, :] = (
                            t[blk * MOBA_BLOCK:(blk + 1) * MOBA_BLOCK].T.astype(BF16))
                        vt_ref[0, h, blk, HEAD_DIM:, :] = jnp.ones(
                            (MOBA_VT_ROWS - HEAD_DIM, MOBA_BLOCK), BF16)
                if with_dilated:
                    hd = (seg - 3) * heads_per_seg + h
                    slot = h % slab_ref.shape[0]
                    q4 = tm // 4
                    slab_ref[slot] = t
                    for r4 in range(4):
                        cls = slab_ref[slot, pl.ds(r4, q4, stride=4), :]
                        d4_ref[0, hd, r4] = cls.astype(BF16)
                        slab4_ref[slot, r4 * q4:(r4 + 1) * q4, :] = cls
                    for r4 in range(4):
                        for m in range(4):
                            cls = slab4_ref[slot, pl.ds(r4 * q4 + m, q4 // 4, stride=4), :]
                            d16_ref[0, hd, r4 + 4 * m] = cls.astype(BF16)


def _qkv_call(x, g, w_bf16, cos_t, sa_t, sb_t, *, n_heads_a, n_heads_b, tm=256):
    B, S, D = x.shape
    N = w_bf16.shape[1]
    assert n_heads_a == n_heads_b and N == 3 * (n_heads_a + n_heads_b) * HEAD_DIM
    assert tm % MOBA_BLOCK == 0 and (tm // 16) % BF16_SUBLANE_TILE == 0
    n_col_heads = N // HEAD_DIM
    kern = functools.partial(_qkv_kernel, heads_per_seg=n_heads_a)
    row_tab = pl.BlockSpec((1, tm, HEAD_DIM), lambda b, i: (b, i, 0))
    return pl.pallas_call(
        kern,
        grid=(B, S // tm),
        in_specs=[
            pl.BlockSpec((1, tm, D), lambda b, i: (b, i, 0)),
            pl.BlockSpec((1, D), lambda b, i: (0, 0)),
            pl.BlockSpec((D, N), lambda b, i: (0, 0), pipeline_mode=pl.Buffered(1)),
            row_tab, row_tab, row_tab,
        ],
        out_specs=[
            pl.BlockSpec((1, n_col_heads, tm, HEAD_DIM), lambda b, i: (b, 0, i, 0)),
            pl.BlockSpec((1, n_heads_a, tm // MOBA_BLOCK, MOBA_VT_ROWS, MOBA_BLOCK),
                         lambda b, i: (b, 0, i, 0, 0)),
            pl.BlockSpec((1, 3 * n_heads_b, 4, tm // 4, HEAD_DIM), lambda b, i: (b, 0, 0, i, 0)),
            pl.BlockSpec((1, 3 * n_heads_b, 16, tm // 16, HEAD_DIM), lambda b, i: (b, 0, 0, i, 0)),
        ],
        out_shape=[
            jax.ShapeDtypeStruct((B, n_col_heads, S, HEAD_DIM), BF16),
            jax.ShapeDtypeStruct((B, n_heads_a, S // MOBA_BLOCK, MOBA_VT_ROWS, MOBA_BLOCK), BF16),
            jax.ShapeDtypeStruct((B, 3 * n_heads_b, 4, S // 4, HEAD_DIM), BF16),
            jax.ShapeDtypeStruct((B, 3 * n_heads_b, 16, S // 16, HEAD_DIM), BF16),
        ],
        scratch_shapes=[pltpu.VMEM((tm, D), BF16),
                        pltpu.VMEM((QKV_SLAB_SLOTS, tm, HEAD_DIM), F32),
                        pltpu.VMEM((QKV_SLAB_SLOTS, tm, HEAD_DIM), F32)],
        compiler_params=pltpu.CompilerParams(
            dimension_semantics=("parallel", "parallel"),
            vmem_limit_bytes=V7X_VMEM_LIMIT_BYTES),
        name="qkv_rope",
    )(x, g, w_bf16, cos_t, sa_t, sb_t)


def _moba_kernel(*refs, n_blocks, heads, n_cast, lookahead=8, loop_lookahead=2,
                 group=MOBA_LOOP_GROUP):
    q_ref, k_ref, vt_ref = refs[:3]
    cast_in = refs[3:3 + n_cast]
    o_ref = refs[3 + n_cast]
    cast_out = refs[4 + n_cast:4 + 2 * n_cast]
    kmean_ref, bias_ref, acc_ref, m_ref, sa_ref, sb_ref = refs[4 + 2 * n_cast:]
    assert n_blocks % group == 0
    i = pl.program_id(2)
    blk_sz = MOBA_BLOCK

    for w_in, w_out in zip(cast_in, cast_out):
        w_out[0] = w_in[0].astype(BF16)

    @pl.when(i == 0)
    def _():
        for h in range(heads):
            for blk in range(n_blocks):
                kb = k_ref[0, h, blk * blk_sz:(blk + 1) * blk_sz, :].astype(F32)
                kmean_ref[h, blk:blk + 1, :] = jnp.mean(kb, axis=0, keepdims=True)

    def select_blocks(h, q):
        km = kmean_ref[h]
        km_hi = km.astype(BF16)
        km_lo = (km - km_hi.astype(F32)).astype(BF16)
        gate = (lax.dot_general(km_hi, q, NT_DIMS, preferred_element_type=F32)
                + lax.dot_general(km_lo, q, NT_DIMS, preferred_element_type=F32))
        blk_id = lax.broadcasted_iota(jnp.int32, gate.shape, 0).astype(F32)
        neg_inf = jnp.float32(-jnp.inf)
        g = jnp.where(blk_id < i.astype(F32), gate, neg_inf)
        sel = jnp.zeros(gate.shape, dtype=jnp.bool_)
        for _ in range(MOBA_TOPK):
            m = jnp.max(g, axis=0, keepdims=True)
            first = jnp.min(jnp.where(g == m, blk_id, float(n_blocks)), axis=0, keepdims=True)
            pick = jnp.logical_and(blk_id == first, m > neg_inf)
            sel = jnp.logical_or(sel, pick)
            g = jnp.where(pick, neg_inf, g)
        bias_ref[h] = jnp.where(sel, 0.0, NEG).astype(F32)

    def scores(h, first_blk, n_blk):
        rows = n_blk * blk_sz
        kb = k_ref[0, h, pl.ds(pl.multiple_of(first_blk * blk_sz, blk_sz), rows), :]
        return lax.dot_general(kb, q_ref[0, h], NT_DIMS, preferred_element_type=F32)

    def pipelined(stage_a, stage_b, depth=lookahead):
        ahead = [stage_a(h) for h in range(min(depth, heads))]
        outs = []
        for h in range(heads):
            if h + depth < heads:
                ahead.append(stage_a(h + depth))
            outs.append(stage_b(h, ahead[h]))
        return outs

    stage = (sa_ref, sb_ref)

    def own_a(h):
        select_blocks(h, q_ref[0, h])
        s_own = scores(h, i, 1)
        stage[0][h] = scores(h, 0, group)
        return s_own

    def own_b(h, s):
        key_pos = lax.broadcasted_iota(jnp.int32, s.shape, 0)
        q_pos = lax.broadcasted_iota(jnp.int32, s.shape, 1)
        t = jnp.where(key_pos <= q_pos, s, NEG)
        m0 = jnp.max(t, axis=0, keepdims=True)
        p = jnp.exp2(t - m0)
        acc_ref[h] = jnp.dot(vt_ref[0, h, i], p.astype(BF16), preferred_element_type=F32)
        return m0

    for h, m0 in enumerate(pipelined(own_a, own_b)):
        m_ref[h] = m0

    n_groups = (i + group - 1) // group
    last_group = n_blocks // group - 1

    def consume_and_prefetch(c, cur_ref, nxt_ref):
        first = c * group
        nxt_first = jnp.minimum(c + 1, last_group) * group

        def next_scores(h):
            nxt_ref[h] = scores(h, nxt_first, group)

        def past_b(h, _):
            m_prev = m_ref[h]
            s = cur_ref[h]
            chunks = [s[g * blk_sz:(g + 1) * blk_sz] for g in range(group)]
            brows = [bias_ref[h, pl.ds(first + g, 1), :] for g in range(group)]
            m_new = m_prev
            for sg, brow in zip(chunks, brows):
                m_new = jnp.maximum(m_new, jnp.max(sg, axis=0, keepdims=True) + brow)
            acc_new = jnp.exp2(m_prev - m_new) * acc_ref[h]
            for g, (sg, brow) in enumerate(zip(chunks, brows)):
                pg = jnp.exp2(sg - (m_new - brow))
                acc_new = acc_new + jnp.dot(vt_ref[0, h, first + g], pg.astype(BF16),
                                            preferred_element_type=F32)
            acc_ref[h] = acc_new
            m_ref[h] = m_new

        pipelined(next_scores, past_b, depth=loop_lookahead)

    def body(c, carry):
        for parity in range(2):
            @pl.when(c % 2 == parity)
            def _(parity=parity):
                consume_and_prefetch(c, stage[parity], stage[1 - parity])
        return carry

    lax.fori_loop(0, n_groups, body, 0)
    for h in range(heads):
        acc_fin = acc_ref[h]
        o_ref[0, h] = (acc_fin[:HEAD_DIM] / acc_fin[HEAD_DIM:HEAD_DIM + 1]).T


def _moba_call(qkv_heads, vt, f32_weights, *, n_heads_a, heads_per_step=8):
    B, _, S, _ = qkv_heads.shape
    n_blocks = S // MOBA_BLOCK
    tq = MOBA_BLOCK
    hps = heads_per_step
    assert n_heads_a % hps == 0
    n_groups = n_heads_a // hps
    k_first = n_groups
    grid = (B, n_groups, S // tq)
    n_steps = grid[0] * grid[1] * grid[2]

    def slab_index(b, h, i):
        return ((b * grid[1] + h) * grid[2] + i, 0, 0)

    slabs, slab_specs, slab_shapes = [], [], []
    for w in f32_weights:
        rows, cols = w.shape
        assert rows % (n_steps * BF16_SUBLANE_TILE) == 0
        slabs.append(w.reshape(n_steps, rows // n_steps, cols))
        slab_specs.append(pl.BlockSpec((1, rows // n_steps, cols), slab_index))
        slab_shapes.append(jax.ShapeDtypeStruct((n_steps, rows // n_steps, cols), BF16))

    kern = functools.partial(_moba_kernel, n_blocks=n_blocks, heads=hps, n_cast=len(slabs))
    outs = pl.pallas_call(
        kern,
        grid=grid,
        in_specs=[
            pl.BlockSpec((1, hps, tq, HEAD_DIM), lambda b, h, i: (b, h, i, 0)),
            pl.BlockSpec((1, hps, S, HEAD_DIM), lambda b, h, i: (b, k_first + h, 0, 0)),
            pl.BlockSpec((1, hps, n_blocks, MOBA_VT_ROWS, MOBA_BLOCK),
                         lambda b, h, i: (b, h, 0, 0, 0), pipeline_mode=pl.Buffered(1)),
        ] + slab_specs,
        out_specs=[pl.BlockSpec((1, hps, tq, HEAD_DIM), lambda b, h, i: (b, h, i, 0))] + slab_specs,
        out_shape=[jax.ShapeDtypeStruct((B, n_heads_a, S, HEAD_DIM), F32)] + slab_shapes,
        scratch_shapes=[pltpu.VMEM((hps, n_blocks, HEAD_DIM), F32),
                        pltpu.VMEM((hps, n_blocks, tq), F32),
                        pltpu.VMEM((hps, MOBA_VT_ROWS, tq), F32),
                        pltpu.VMEM((hps, 1, tq), F32),
                        pltpu.VMEM((hps, MOBA_LOOP_GROUP * MOBA_BLOCK, tq), F32),
                        pltpu.VMEM((hps, MOBA_LOOP_GROUP * MOBA_BLOCK, tq), F32)],
        compiler_params=pltpu.CompilerParams(
            dimension_semantics=("parallel", "parallel", "arbitrary"),
            vmem_limit_bytes=V7X_VMEM_LIMIT_BYTES),
        name="moba_attn",
    )(qkv_heads, qkv_heads, vt, *slabs)
    return outs[0], [o.reshape(w.shape) for o, w in zip(outs[1:], f32_weights)]


def _dilated_kernel(q1_ref, k1_ref, v1_ref, q4_ref, k4_ref, v4_ref, q16_ref, k16_ref, v16_ref,
                    ob_ref, o_scr, lse_scr, bias_scr, *, seq_len, blocks_per_iter, combine_rows):
    blk = DIL_BLOCK
    branch_refs = ((q1_ref, k1_ref, v1_ref), (q4_ref, k4_ref, v4_ref), (q16_ref, k16_ref, v16_ref))

    qi = lax.broadcasted_iota(jnp.int32, (blk, 2 * blk), 0)
    ki = lax.broadcasted_iota(jnp.int32, (blk, 2 * blk), 1)
    dist = qi + blk - ki
    bias_scr[0] = jnp.where(jnp.logical_and(dist >= 0, dist <= blk), 0.0, NEG).astype(F32)
    bias_scr[1] = jnp.where(ki <= qi, 0.0, NEG).astype(F32)

    for g, (window, d) in enumerate(DIL_PAIRS):
        assert window // d == DIL_BLOCK
        q_ref, k_ref, v_ref = branch_refs[g]
        n_blk = seq_len // d // blk

        def rows_of(ref, r, start, size, d=d):
            if d == 1:
                return ref[0, 0, pl.ds(start, size), :]
            return ref[0, 0, r, pl.ds(start, size), :]

        def key_start(n):
            return pl.multiple_of(jnp.maximum(n - 1, 0) * blk, blk)

        def scores(r, n, q_ref=q_ref, k_ref=k_ref, rows_of=rows_of):
            qb = rows_of(q_ref, r, pl.multiple_of(n * blk, blk), blk)
            kb = rows_of(k_ref, r, key_start(n), 2 * blk)
            return lax.dot_general(qb, kb, NT_DIMS, preferred_element_type=F32)

        def finish(r, n, s, g=g, d=d, v_ref=v_ref, rows_of=rows_of):
            t = s + bias_scr[jnp.where(n == 0, 1, 0)]
            m = jnp.max(t, axis=-1, keepdims=True)
            p = jnp.exp2(t - m)
            vb = rows_of(v_ref, r, key_start(n), 2 * blk)
            v_ones = jnp.concatenate([vb, jnp.ones_like(vb)], axis=1)
            pv = jnp.dot(p.astype(BF16), v_ones, preferred_element_type=F32)
            den = pv[:, HEAD_DIM:]
            q_start = pl.multiple_of(n * blk, blk)
            if d == 1:
                rows = pl.ds(q_start, blk)
            elif d == 4:
                rows = pl.ds(r * (seq_len // 4) + q_start, blk)
            else:
                r4, m4 = r % 4, r // 4
                rows = pl.ds(r4 * (seq_len // 4) + 4 * q_start + m4, blk, stride=4)
            o_scr[g, rows, :] = pv[:, :HEAD_DIM] / den
            lse_scr[g, rows, :] = m + jnp.log2(den)

        def run_blocks(tasks, scores=scores, finish=finish):
            nxt = scores(*tasks[0])
            for idx, (r, n) in enumerate(tasks):
                cur = nxt
                if idx + 1 < len(tasks):
                    nxt = scores(*tasks[idx + 1])
                finish(r, n, cur)

        n_per_iter = max(1, min(blocks_per_iter // d, n_blk))

        def body(it, carry, run_blocks=run_blocks, d=d, n_per_iter=n_per_iter):
            run_blocks([(r, it * n_per_iter + u) for u in range(n_per_iter) for r in range(d)])
            return carry

        lax.fori_loop(0, n_blk // n_per_iter, body, 0)

    def combine(c, carry):
        i0 = pl.multiple_of(c * combine_rows, combine_rows)
        for r4 in range(4):
            tok = pl.ds(4 * i0 + r4, combine_rows, stride=4)
            cls = pl.ds(r4 * (seq_len // 4) + i0, combine_rows)
            l1, l2, l3 = lse_scr[0, tok, :], lse_scr[1, cls, :], lse_scr[2, cls, :]
            lmax = jnp.maximum(jnp.maximum(l1, l2), l3)
            e1, e2, e3 = jnp.exp2(l1 - lmax), jnp.exp2(l2 - lmax), jnp.exp2(l3 - lmax)
            num = e1 * o_scr[0, tok, :] + e2 * o_scr[1, cls, :] + e3 * o_scr[2, cls, :]
            ob_ref[0, 0, tok, :] = num / (e1 + e2 + e3)
        return carry

    lax.fori_loop(0, seq_len // 4 // combine_rows, combine, 0)


def _dilated_call(qkv_heads, d4, d16, *, n_heads_b, q_off, k_off, v_off, blocks_per_iter=16):
    B, _, S, _ = qkv_heads.shape
    offs_nat = (q_off, k_off, v_off)
    offs_dil = (0, n_heads_b, 2 * n_heads_b)
    in_arrays = [qkv_heads] * 3 + [d4] * 3 + [d16] * 3
    in_specs = (
        [pl.BlockSpec((1, 1, S, HEAD_DIM), lambda b, h, off=off: (b, off + h, 0, 0))
         for off in offs_nat]
        + [pl.BlockSpec((1, 1, 4, S // 4, HEAD_DIM), lambda b, h, off=off: (b, off + h, 0, 0, 0))
           for off in offs_dil]
        + [pl.BlockSpec((1, 1, 16, S // 16, HEAD_DIM), lambda b, h, off=off: (b, off + h, 0, 0, 0))
           for off in offs_dil])
    n_br = len(DIL_PAIRS)
    return pl.pallas_call(
        functools.partial(_dilated_kernel, seq_len=S, blocks_per_iter=blocks_per_iter,
                          combine_rows=64),
        grid=(B, n_heads_b),
        in_specs=in_specs,
        out_specs=pl.BlockSpec((1, 1, S, HEAD_DIM), lambda b, h: (b, h, 0, 0)),
        out_shape=jax.ShapeDtypeStruct((B, n_heads_b, S, HEAD_DIM), F32),
        scratch_shapes=[pltpu.VMEM((n_br, S, HEAD_DIM), F32),
                        pltpu.VMEM((n_br, S, HEAD_DIM), F32),
                        pltpu.VMEM((2, DIL_BLOCK, 2 * DIL_BLOCK), F32)],
        compiler_params=pltpu.CompilerParams(
            dimension_semantics=("parallel", "parallel"),
            vmem_limit_bytes=V7X_VMEM_LIMIT_BYTES),
        name="dilated_attn",
    )(*in_arrays)


def _attnout_kernel(x_ref, oa_ref, ob_ref, ga_ref, gb_ref, wo_ref, pg_ref, fg_ref,
                    out_ref, hn_ref, mix_ref, *, row_chunks):
    tm = x_ref.shape[1]
    rows_per_chunk = tm // row_chunks
    for c in range(row_chunks):
        rows = slice(c * rows_per_chunk, (c + 1) * rows_per_chunk)
        col = 0
        for o_ref, g_ref in ((oa_ref, ga_ref), (ob_ref, gb_ref)):
            n_heads = o_ref.shape[1]
            ssq = None
            for h in range(n_heads):
                t = o_ref[0, h, rows, :]
                part = jnp.sum(t * t, axis=-1, keepdims=True)
                ssq = part if ssq is None else ssq + part
            inv = lax.rsqrt(ssq / (n_heads * HEAD_DIM) + RMS_EPS)
            for h in range(n_heads):
                gs = slice(h * HEAD_DIM, (h + 1) * HEAD_DIM)
                mix_ref[rows, col:col + HEAD_DIM] = (
                    o_ref[0, h, rows, :] * inv * g_ref[:, gs]).astype(BF16)
                col += HEAD_DIM
        y = jnp.dot(mix_ref[rows, :], wo_ref[...], preferred_element_type=F32)
        x_new = x_ref[0, rows, :] + y * _rms_scale(y) * pg_ref[...]
        out_ref[0, rows, :] = x_new
        hn_ref[0, rows, :] = (x_new * _rms_scale(x_new) * fg_ref[...]).astype(BF16)


def _attnout_call(x, oa, ob, ga, gb, wo_bf16, pg, ffn_g, *, tm=512, row_chunks=2):
    B, S, D = x.shape
    head_spec_a = pl.BlockSpec((1, oa.shape[1], tm, HEAD_DIM), lambda b, i: (b, 0, i, 0))
    head_spec_b = pl.BlockSpec((1, ob.shape[1], tm, HEAD_DIM), lambda b, i: (b, 0, i, 0))
    row_spec = pl.BlockSpec((1, tm, D), lambda b, i: (b, i, 0))
    return pl.pallas_call(
        functools.partial(_attnout_kernel, row_chunks=row_chunks),
        grid=(B, S // tm),
        in_specs=[
            row_spec,
            head_spec_a, head_spec_b,
            pl.BlockSpec((1, ga.shape[1]), lambda b, i: (0, 0)),
            pl.BlockSpec((1, gb.shape[1]), lambda b, i: (0, 0)),
            pl.BlockSpec(wo_bf16.shape, lambda b, i: (0, 0)),
            pl.BlockSpec((1, D), lambda b, i: (0, 0)),
            pl.BlockSpec((1, D), lambda b, i: (0, 0)),
        ],
        out_specs=[row_spec, row_spec],
        out_shape=[jax.ShapeDtypeStruct((B, S, D), F32), jax.ShapeDtypeStruct((B, S, D), BF16)],
        scratch_shapes=[pltpu.VMEM((tm, wo_bf16.shape[0]), BF16)],
        compiler_params=pltpu.CompilerParams(
            dimension_semantics=("parallel", "parallel"),
            vmem_limit_bytes=V7X_VMEM_LIMIT_BYTES),
        name="attn_out",
    )(x, oa, ob, ga, gb, wo_bf16, pg, ffn_g)


def _ffn_kernel(x_ref, hn_ref, wg_ref, wu_ref, cw_ref, cb_ref, wd_ref, pg_ref,
                out_ref, tail_ref, act_ref):
    i = pl.program_id(1)
    f = pl.program_id(2)
    tm = x_ref.shape[1]
    halo = tail_ref.shape[1]

    @pl.when(i == 0)
    def _():
        tail_ref[f] = jnp.zeros(tail_ref.shape[1:], F32)

    @pl.when(f == 0)
    def _():
        out_ref[0] = jnp.zeros(out_ref.shape[1:], F32)

    def step(is_last):
        gate = jnp.dot(hn_ref[0], wg_ref[...], preferred_element_type=F32)
        up = jnp.dot(hn_ref[0], wu_ref[...], preferred_element_type=F32)
        cw = cw_ref[...]
        cb = cb_ref[...]

        def gated(g_m2, g_m1, g_0, u):
            gc = cb + g_m2 * cw[0:1, :] + g_m1 * cw[1:2, :] + g_0 * cw[2:3, :]
            gelu = 0.5 * gc * (1.0 + jnp.tanh(0.7978845608028654 * (gc + 0.044715 * (gc * gc * gc))))
            return (gelu * u).astype(BF16)

        act_ref[...] = gated(pltpu.roll(gate, 2, 0), pltpu.roll(gate, 1, 0), gate, up)
        head = BF16_SUBLANE_TILE
        ext = jnp.concatenate([tail_ref[f], gate[0:head]], axis=0)
        act_ref[0:head, :] = gated(ext[halo - 2:halo - 2 + head], ext[halo - 1:halo - 1 + head],
                                   gate[0:head], up[0:head])
        tail_ref[f] = gate[tm - halo:, :]
        if not is_last:
            out_ref[0] += jnp.dot(act_ref[...], wd_ref[...], preferred_element_type=F32)
        else:
            half = tm // FFN_LAST_STEP_ROW_CHUNKS
            for r0 in range(0, tm, half):
                rows = slice(r0, r0 + half)
                y = out_ref[0, rows, :] + jnp.dot(act_ref[rows, :], wd_ref[...],
                                                  preferred_element_type=F32)
                out_ref[0, rows, :] = x_ref[0, rows, :] + y * _rms_scale(y) * pg_ref[...]

    last = pl.num_programs(2) - 1
    pl.when(f < last)(lambda: step(False))
    pl.when(f == last)(lambda: step(True))


def _ffn_call(x, hn, wg_bf16, wu_bf16, conv_w, conv_b, wd_bf16, pg, *, tm=512, tf=1024):
    B, S, D = x.shape
    d_ff = wg_bf16.shape[1]
    halo = F32_SUBLANE_TILE
    assert CONV_WIDTH - 1 <= halo
    return pl.pallas_call(
        _ffn_kernel,
        grid=(B, S // tm, d_ff // tf),
        in_specs=[
            pl.BlockSpec((1, tm, D), lambda b, i, f: (b, i, 0)),
            pl.BlockSpec((1, tm, D), lambda b, i, f: (b, i, 0)),
            pl.BlockSpec((D, tf), lambda b, i, f: (0, f)),
            pl.BlockSpec((D, tf), lambda b, i, f: (0, f)),
            pl.BlockSpec((CONV_WIDTH, tf), lambda b, i, f: (0, f)),
            pl.BlockSpec((1, tf), lambda b, i, f: (0, f)),
            pl.BlockSpec((tf, D), lambda b, i, f: (f, 0)),
            pl.BlockSpec((1, D), lambda b, i, f: (0, 0)),
        ],
        out_specs=pl.BlockSpec((1, tm, D), lambda b, i, f: (b, i, 0)),
        out_shape=jax.ShapeDtypeStruct((B, S, D), F32),
        scratch_shapes=[pltpu.VMEM((d_ff // tf, halo, tf), F32), pltpu.VMEM((tm, tf), BF16)],
        compiler_params=pltpu.CompilerParams(
            dimension_semantics=("parallel", "arbitrary", "arbitrary"),
            vmem_limit_bytes=V7X_VMEM_LIMIT_BYTES),
        name="conv_glu_ffn",
    )(x, hn, wg_bf16, wu_bf16, conv_w, conv_b, wd_bf16, pg)


def _rope_tables(positions):
    inv_freq = ROPE_THETA ** (-jnp.arange(ROT_HALF, dtype=F32) / ROT_HALF)
    batch, seq = positions.shape
    per_row = HEAD_DIM // ROT_HALF
    pos = positions.astype(F32).reshape(batch, seq // per_row, per_row, 1)
    ang = (pos * inv_freq).reshape(batch, seq // per_row, HEAD_DIM)
    cos, sin = lax.optimization_barrier((jnp.cos(ang), jnp.sin(ang)))
    cos = cos.reshape(batch, seq, ROT_HALF)
    sin = sin.reshape(batch, seq, ROT_HALF)
    ones = jnp.ones((batch, seq, HEAD_DIM - ROT_DIM), F32)
    zeros_tail = jnp.zeros((batch, seq, HEAD_DIM - ROT_HALF), F32)
    zeros_head = jnp.zeros_like(sin)
    cos_t = jnp.concatenate([cos, cos, ones], axis=-1)
    sa_t = jnp.concatenate([-sin, zeros_tail], axis=-1)
    sb_t = jnp.concatenate([zeros_head, sin, zeros_tail[..., :HEAD_DIM - ROT_DIM]], axis=-1)
    return cos_t, sa_t, sb_t


def kernel(x, positions, attn_pre_g, w_qkv, moba_out_g, dil_out_g, w_o, attn_post_g, ffn_pre_g,
           w_gate, w_up, conv_w, conv_b, w_down, ffn_post_g):
    depth = w_qkv.shape[0]
    n_heads_a = moba_out_g.shape[1] // HEAD_DIM
    n_heads_b = dil_out_g.shape[1] // HEAD_DIM
    cos_t, sa_t, sb_t = _rope_tables(positions)
    for l in range(depth):
        qkv_heads, vt, d4, d16 = _qkv_call(x, attn_pre_g[l][None], w_qkv[l].astype(BF16),
                                           cos_t, sa_t, sb_t,
                                           n_heads_a=n_heads_a, n_heads_b=n_heads_b)
        oa, (wo_bf16, wg_bf16, wu_bf16, wd_bf16) = _moba_call(
            qkv_heads, vt, [w_o[l], w_gate[l], w_up[l], w_down[l]], n_heads_a=n_heads_a)
        ob = _dilated_call(qkv_heads, d4, d16, n_heads_b=n_heads_b, q_off=3 * n_heads_a,
                           k_off=3 * n_heads_a + n_heads_b, v_off=3 * n_heads_a + 2 * n_heads_b)
        x, hn = _attnout_call(x, oa, ob, moba_out_g[l][None], dil_out_g[l][None],
                              wo_bf16, attn_post_g[l][None], ffn_pre_g[l][None])
        x = _ffn_call(x, hn, wg_bf16, wu_bf16, conv_w[l], conv_b[l][None], wd_bf16,
                      ffn_post_g[l][None])
    return x
```

```python
import functools

import jax
import jax.numpy as jnp
from jax import lax
from jax.experimental import pallas as pl
from jax.experimental.pallas import tpu as pltpu

F32 = jnp.float32
BF16 = jnp.bfloat16

HEAD_DIM = 128
ROT_DIM = HEAD_DIM // 4
ROT_HALF = ROT_DIM // 2
ROPE_THETA = 500000.0
MOBA_BLOCK = 256
MOBA_TOPK = 3
DIL_PAIRS = ((128, 1), (512, 4), (2048, 16))
DIL_BLOCK = 128
CONV_WIDTH = 3
RMS_EPS = 1e-6
SCALE = HEAD_DIM ** -0.5
SCALE_LOG2E = SCALE * 1.4426950408889634
NEG = -1e30

V7X_VMEM_LIMIT_BYTES = 56 * 1024 * 1024
BF16_SUBLANE_TILE = 16
V7X_MXU_WIDTH = 256
F32_SUBLANE_TILE = 8
FFN_LAST_STEP_ROW_CHUNKS = 2
QKV_SLAB_SLOTS = 2
MOBA_VT_ROWS = HEAD_DIM + BF16_SUBLANE_TILE
MOBA_LOOP_GROUP = 2

NT_DIMS = (((1,), (1,)), ((), ()))


def _rms_scale(x):
    return lax.rsqrt(jnp.mean(x * x, axis=-1, keepdims=True) + RMS_EPS)


def _qkv_kernel(x_ref, g_ref, w_ref, pos_ref, freq_ref,
                nat_ref, vt_ref, d4_ref, d16_ref, hn_ref, slab_ref, slab4_ref, *, heads_per_seg):
    tm = hn_ref.shape[0]
    heads_per_dot = V7X_MXU_WIDTH // HEAD_DIM
    dot_width = heads_per_dot * HEAD_DIM
    seg_width = heads_per_seg * HEAD_DIM

    x = x_ref[0]
    hn_ref[...] = (x * _rms_scale(x) * g_ref[...]).astype(BF16)

    ang = pos_ref[0].astype(F32) * freq_ref[...]
    cos_t, sin_t = jnp.cos(ang), jnp.sin(ang)
    lane = lax.broadcasted_iota(jnp.int32, ang.shape, 1)
    sa_t = jnp.where(lane < ROT_HALF, -sin_t, 0.0)
    sb_t = jnp.where(lane >= ROT_HALF, sin_t, 0.0)

    def make_rope(scale):
        cos, sa, sb = cos_t * scale, sa_t * scale, sb_t * scale

        def rope(t):
            return (t * cos + pltpu.roll(t, HEAD_DIM - ROT_HALF, 1) * sa
                    + pltpu.roll(t, ROT_HALF, 1) * sb)
        return rope

    rope_q, rope_k = make_rope(SCALE_LOG2E), make_rope(1.0)

    for seg in (2, 0, 1, 3, 4, 5):
        rope = {0: rope_q, 1: rope_k, 3: rope_q, 4: rope_k}.get(seg)
        with_vt = seg == 2
        with_dilated = seg >= 3
        for c in range(heads_per_seg // heads_per_dot):
            col = seg * seg_width + c * dot_width
            acc = jnp.dot(hn_ref[...], w_ref[:, col:col + dot_width], preferred_element_type=F32)
            for hh in range(heads_per_dot):
                h = c * heads_per_dot + hh
                t = acc[:, hh * HEAD_DIM:(hh + 1) * HEAD_DIM]
                if rope is not None:
                    t = rope(t)
                nat_ref[0, seg * heads_per_seg + h] = t.astype(BF16)
                if with_vt:
                    for blk in range(tm // MOBA_BLOCK):
                        vt_ref[0, h, blk, 0:HEAD_DIM, :] = (
                            t[blk * MOBA_BLOCK:(blk + 1) * MOBA_BLOCK].T.astype(BF16))
                        vt_ref[0, h, blk, HEAD_DIM:, :] = jnp.ones(
                            (MOBA_VT_ROWS - HEAD_DIM, MOBA_BLOCK), BF16)
                if with_dilated:
                    hd = (seg - 3) * heads_per_seg + h
                    slot = h % slab_ref.shape[0]
                    q4 = tm // 4
                    slab_ref[slot] = t
                    for r4 in range(4):
                        cls = slab_ref[slot, pl.ds(r4, q4, stride=4), :]
                        d4_ref[0, hd, r4] = cls.astype(BF16)
                        slab4_ref[slot, r4 * q4:(r4 + 1) * q4, :] = cls
                    for r4 in range(4):
                        for m in range(4):
                            cls = slab4_ref[slot, pl.ds(r4 * q4 + m, q4 // 4, stride=4), :]
                            d16_ref[0, hd, r4 + 4 * m] = cls.astype(BF16)


def _qkv_call(x, g, w_bf16, positions, *, n_heads_a, n_heads_b, tm=256):
    B, S, D = x.shape
    half_freq = ROPE_THETA ** (-jnp.arange(ROT_HALF, dtype=F32) / ROT_HALF)
    lane_freq = jnp.concatenate(
        [half_freq, half_freq, jnp.zeros((HEAD_DIM - ROT_DIM,), F32)])[None]
    pos_col = positions.reshape(B, S, 1)
    N = w_bf16.shape[1]
    assert n_heads_a == n_heads_b and N == 3 * (n_heads_a + n_heads_b) * HEAD_DIM
    assert tm % MOBA_BLOCK == 0 and (tm // 16) % BF16_SUBLANE_TILE == 0
    n_col_heads = N // HEAD_DIM
    kern = functools.partial(_qkv_kernel, heads_per_seg=n_heads_a)
    return pl.pallas_call(
        kern,
        grid=(B, S // tm),
        in_specs=[
            pl.BlockSpec((1, tm, D), lambda b, i: (b, i, 0)),
            pl.BlockSpec((1, D), lambda b, i: (0, 0)),
            pl.BlockSpec((D, N), lambda b, i: (0, 0), pipeline_mode=pl.Buffered(1)),
            pl.BlockSpec((1, tm, 1), lambda b, i: (b, i, 0)),
            pl.BlockSpec((1, HEAD_DIM), lambda b, i: (0, 0)),
        ],
        out_specs=[
            pl.BlockSpec((1, n_col_heads, tm, HEAD_DIM), lambda b, i: (b, 0, i, 0)),
            pl.BlockSpec((1, n_heads_a, tm // MOBA_BLOCK, MOBA_VT_ROWS, MOBA_BLOCK),
                         lambda b, i: (b, 0, i, 0, 0)),
            pl.BlockSpec((1, 3 * n_heads_b, 4, tm // 4, HEAD_DIM), lambda b, i: (b, 0, 0, i, 0)),
            pl.BlockSpec((1, 3 * n_heads_b, 16, tm // 16, HEAD_DIM), lambda b, i: (b, 0, 0, i, 0)),
        ],
        out_shape=[
            jax.ShapeDtypeStruct((B, n_col_heads, S, HEAD_DIM), BF16),
            jax.ShapeDtypeStruct((B, n_heads_a, S // MOBA_BLOCK, MOBA_VT_ROWS, MOBA_BLOCK), BF16),
            jax.ShapeDtypeStruct((B, 3 * n_heads_b, 4, S // 4, HEAD_DIM), BF16),
            jax.ShapeDtypeStruct((B, 3 * n_heads_b, 16, S // 16, HEAD_DIM), BF16),
        ],
        scratch_shapes=[pltpu.VMEM((tm, D), BF16),
                        pltpu.VMEM((QKV_SLAB_SLOTS, tm, HEAD_DIM), F32),
                        pltpu.VMEM((QKV_SLAB_SLOTS, tm, HEAD_DIM), F32)],
        compiler_params=pltpu.CompilerParams(
            dimension_semantics=("parallel", "parallel"),
            vmem_limit_bytes=V7X_VMEM_LIMIT_BYTES),
        name="qkv_rope",
    )(x, g, w_bf16, pos_col, lane_freq)


def _moba_kernel(*refs, n_blocks, heads, n_cast, lookahead=8, loop_lookahead=2,
                 group=MOBA_LOOP_GROUP):
    q_ref, k_ref, vt_ref = refs[:3]
    cast_in = refs[3:3 + n_cast]
    o_ref = refs[3 + n_cast]
    cast_out = refs[4 + n_cast:4 + 2 * n_cast]
    kmean_ref, bias_ref, acc_ref, m_ref, sa_ref, sb_ref = refs[4 + 2 * n_cast:]
    assert n_blocks % group == 0
    i = pl.program_id(2)
    blk_sz = MOBA_BLOCK

    for w_in, w_out in zip(cast_in, cast_out):
        w_out[0] = w_in[0].astype(BF16)

    @pl.when(i == 0)
    def _():
        for h in range(heads):
            for blk in range(n_blocks):
                kb = k_ref[0, h, blk * blk_sz:(blk + 1) * blk_sz, :].astype(F32)
                kmean_ref[h, blk:blk + 1, :] = jnp.mean(kb, axis=0, keepdims=True)

    def select_blocks(h, q):
        km = kmean_ref[h]
        km_hi = km.astype(BF16)
        km_lo = (km - km_hi.astype(F32)).astype(BF16)
        gate = (lax.dot_general(km_hi, q, NT_DIMS, preferred_element_type=F32)
                + lax.dot_general(km_lo, q, NT_DIMS, preferred_element_type=F32))
        blk_id = lax.broadcasted_iota(jnp.int32, gate.shape, 0).astype(F32)
        neg_inf = jnp.float32(-jnp.inf)
        g = jnp.where(blk_id < i.astype(F32), gate, neg_inf)
        sel = jnp.zeros(gate.shape, dtype=jnp.bool_)
        for _ in range(MOBA_TOPK):
            m = jnp.max(g, axis=0, keepdims=True)
            first = jnp.min(jnp.where(g == m, blk_id, float(n_blocks)), axis=0, keepdims=True)
            pick = jnp.logical_and(blk_id == first, m > neg_inf)
            sel = jnp.logical_or(sel, pick)
            g = jnp.where(pick, neg_inf, g)
        bias_ref[h] = jnp.where(sel, 0.0, NEG).astype(F32)

    def scores(h, first_blk, n_blk):
        rows = n_blk * blk_sz
        kb = k_ref[0, h, pl.ds(pl.multiple_of(first_blk * blk_sz, blk_sz), rows), :]
        return lax.dot_general(kb, q_ref[0, h], NT_DIMS, preferred_element_type=F32)

    def pipelined(stage_a, stage_b, depth=lookahead):
        ahead = [stage_a(h) for h in range(min(depth, heads))]
        outs = []
        for h in range(heads):
            if h + depth < heads:
                ahead.append(stage_a(h + depth))
            outs.append(stage_b(h, ahead[h]))
        return outs

    stage = (sa_ref, sb_ref)

    def own_a(h):
        select_blocks(h, q_ref[0, h])
        s_own = scores(h, i, 1)
        stage[0][h] = scores(h, 0, group)
        return s_own

    def own_b(h, s):
        key_pos = lax.broadcasted_iota(jnp.int32, s.shape, 0)
        q_pos = lax.broadcasted_iota(jnp.int32, s.shape, 1)
        t = jnp.where(key_pos <= q_pos, s, NEG)
        m0 = jnp.max(t, axis=0, keepdims=True)
        p = jnp.exp2(t - m0)
        acc_ref[h] = jnp.dot(vt_ref[0, h, i], p.astype(BF16), preferred_element_type=F32)
        return m0

    for h, m0 in enumerate(pipelined(own_a, own_b)):
        m_ref[h] = m0

    n_groups = (i + group - 1) // group
    last_group = n_blocks // group - 1

    def consume_and_prefetch(c, cur_ref, nxt_ref):
        first = c * group
        nxt_first = jnp.minimum(c + 1, last_group) * group

        def next_scores(h):
            nxt_ref[h] = scores(h, nxt_first, group)

        def past_b(h, _):
            m_prev = m_ref[h]
            s = cur_ref[h]
            chunks = [s[g * blk_sz:(g + 1) * blk_sz] for g in range(group)]
            brows = [bias_ref[h, pl.ds(first + g, 1), :] for g in range(group)]
            m_new = m_prev
            for sg, brow in zip(chunks, brows):
                m_new = jnp.maximum(m_new, jnp.max(sg, axis=0, keepdims=True) + brow)
            acc_new = jnp.exp2(m_prev - m_new) * acc_ref[h]
            for g, (sg, brow) in enumerate(zip(chunks, brows)):
                pg = jnp.exp2(sg - (m_new - brow))
                acc_new = acc_new + jnp.dot(vt_ref[0, h, first + g], pg.astype(BF16),
                                            preferred_element_type=F32)
            acc_ref[h] = acc_new
            m_ref[h] = m_new

        pipelined(next_scores, past_b, depth=loop_lookahead)

    def body(c, carry):
        for parity in range(2):
            @pl.when(c % 2 == parity)
            def _(parity=parity):
                consume_and_prefetch(c, stage[parity], stage[1 - parity])
        return carry

    lax.fori_loop(0, n_groups, body, 0)
    for h in range(heads):
        acc_fin = acc_ref[h]
        o_ref[0, h] = (acc_fin[:HEAD_DIM] / acc_fin[HEAD_DIM:HEAD_DIM + 1]).T


def _moba_call(qkv_heads, vt, f32_weights, *, n_heads_a, heads_per_step=8):
    B, _, S, _ = qkv_heads.shape
    n_blocks = S // MOBA_BLOCK
    tq = MOBA_BLOCK
    hps = heads_per_step
    assert n_heads_a % hps == 0
    n_groups = n_heads_a // hps
    k_first = n_groups
    grid = (B, n_groups, S // tq)
    n_steps = grid[0] * grid[1] * grid[2]

    def slab_index(b, h, i):
        return ((b * grid[1] + h) * grid[2] + i, 0, 0)

    slabs, slab_specs, slab_shapes = [], [], []
    for w in f32_weights:
        rows, cols = w.shape
        assert rows % (n_steps * BF16_SUBLANE_TILE) == 0
        slabs.append(w.reshape(n_steps, rows // n_steps, cols))
        slab_specs.append(pl.BlockSpec((1, rows // n_steps, cols), slab_index))
        slab_shapes.append(jax.ShapeDtypeStruct((n_steps, rows // n_steps, cols), BF16))

    kern = functools.partial(_moba_kernel, n_blocks=n_blocks, heads=hps, n_cast=len(slabs))
    outs = pl.pallas_call(
        kern,
        grid=grid,
        in_specs=[
            pl.BlockSpec((1, hps, tq, HEAD_DIM), lambda b, h, i: (b, h, i, 0)),
            pl.BlockSpec((1, hps, S, HEAD_DIM), lambda b, h, i: (b, k_first + h, 0, 0)),
            pl.BlockSpec((1, hps, n_blocks, MOBA_VT_ROWS, MOBA_BLOCK),
                         lambda b, h, i: (b, h, 0, 0, 0), pipeline_mode=pl.Buffered(1)),
        ] + slab_specs,
        out_specs=[pl.BlockSpec((1, hps, tq, HEAD_DIM), lambda b, h, i: (b, h, i, 0))] + slab_specs,
        out_shape=[jax.ShapeDtypeStruct((B, n_heads_a, S, HEAD_DIM), F32)] + slab_shapes,
        scratch_shapes=[pltpu.VMEM((hps, n_blocks, HEAD_DIM), F32),
                        pltpu.VMEM((hps, n_blocks, tq), F32),
                        pltpu.VMEM((hps, MOBA_VT_ROWS, tq), F32),
                        pltpu.VMEM((hps, 1, tq), F32),
                        pltpu.VMEM((hps, MOBA_LOOP_GROUP * MOBA_BLOCK, tq), F32),
                        pltpu.VMEM((hps, MOBA_LOOP_GROUP * MOBA_BLOCK, tq), F32)],
        compiler_params=pltpu.CompilerParams(
            dimension_semantics=("parallel", "parallel", "arbitrary"),
            vmem_limit_bytes=V7X_VMEM_LIMIT_BYTES),
        name="moba_attn",
    )(qkv_heads, qkv_heads, vt, *slabs)
    return outs[0], [o.reshape(w.shape) for o, w in zip(outs[1:], f32_weights)]


def _dilated_kernel(q1_ref, k1_ref, v1_ref, q4_ref, k4_ref, v4_ref, q16_ref, k16_ref, v16_ref,
                    ob_ref, o_scr, lse_scr, bias_scr, *, seq_len, blocks_per_iter, combine_rows):
    blk = DIL_BLOCK
    branch_refs = ((q1_ref, k1_ref, v1_ref), (q4_ref, k4_ref, v4_ref), (q16_ref, k16_ref, v16_ref))

    qi = lax.broadcasted_iota(jnp.int32, (blk, 2 * blk), 0)
    ki = lax.broadcasted_iota(jnp.int32, (blk, 2 * blk), 1)
    dist = qi + blk - ki
    bias_scr[0] = jnp.where(jnp.logical_and(dist >= 0, dist <= blk), 0.0, NEG).astype(F32)
    bias_scr[1] = jnp.where(ki <= qi, 0.0, NEG).astype(F32)

    for g, (window, d) in enumerate(DIL_PAIRS):
        assert window // d == DIL_BLOCK
        q_ref, k_ref, v_ref = branch_refs[g]
        n_blk = seq_len // d // blk

        def rows_of(ref, r, start, size, d=d):
            if d == 1:
                return ref[0, 0, pl.ds(start, size), :]
            return ref[0, 0, r, pl.ds(start, size), :]

        def key_start(n):
            return pl.multiple_of(jnp.maximum(n - 1, 0) * blk, blk)

        def scores(r, n, q_ref=q_ref, k_ref=k_ref, rows_of=rows_of):
            qb = rows_of(q_ref, r, pl.multiple_of(n * blk, blk), blk)
            kb = rows_of(k_ref, r, key_start(n), 2 * blk)
            return lax.dot_general(qb, kb, NT_DIMS, preferred_element_type=F32)

        def finish(r, n, s, g=g, d=d, v_ref=v_ref, rows_of=rows_of):
            t = s + bias_scr[jnp.where(n == 0, 1, 0)]
            m = jnp.max(t, axis=-1, keepdims=True)
            p = jnp.exp2(t - m)
            vb = rows_of(v_ref, r, key_start(n), 2 * blk)
            v_ones = jnp.concatenate([vb, jnp.ones_like(vb)], axis=1)
            pv = jnp.dot(p.astype(BF16), v_ones, preferred_element_type=F32)
            den = pv[:, HEAD_DIM:]
            q_start = pl.multiple_of(n * blk, blk)
            if d == 1:
                rows = pl.ds(q_start, blk)
            elif d == 4:
                rows = pl.ds(r * (seq_len // 4) + q_start, blk)
            else:
                r4, m4 = r % 4, r // 4
                rows = pl.ds(r4 * (seq_len // 4) + 4 * q_start + m4, blk, stride=4)
            o_scr[g, rows, :] = pv[:, :HEAD_DIM] / den
            lse_scr[g, rows, :] = m + jnp.log2(den)

        def run_blocks(tasks, scores=scores, finish=finish):
            nxt = scores(*tasks[0])
            for idx, (r, n) in enumerate(tasks):
                cur = nxt
                if idx + 1 < len(tasks):
                    nxt = scores(*tasks[idx + 1])
                finish(r, n, cur)

        n_per_iter = max(1, min(blocks_per_iter // d, n_blk))

        def body(it, carry, run_blocks=run_blocks, d=d, n_per_iter=n_per_iter):
            run_blocks([(r, it * n_per_iter + u) for u in range(n_per_iter) for r in range(d)])
            return carry

        lax.fori_loop(0, n_blk // n_per_iter, body, 0)

    def combine(c, carry):
        i0 = pl.multiple_of(c * combine_rows, combine_rows)
        for r4 in range(4):
            tok = pl.ds(4 * i0 + r4, combine_rows, stride=4)
            cls = pl.ds(r4 * (seq_len // 4) + i0, combine_rows)
            l1, l2, l3 = lse_scr[0, tok, :], lse_scr[1, cls, :], lse_scr[2, cls, :]
            lmax = jnp.maximum(jnp.maximum(l1, l2), l3)
            e1, e2, e3 = jnp.exp2(l1 - lmax), jnp.exp2(l2 - lmax), jnp.exp2(l3 - lmax)
            num = e1 * o_scr[0, tok, :] + e2 * o_scr[1, cls, :] + e3 * o_scr[2, cls, :]
            ob_ref[0, 0, tok, :] = num / (e1 + e2 + e3)
        return carry

    lax.fori_loop(0, seq_len // 4 // combine_rows, combine, 0)


def _dilated_call(qkv_heads, d4, d16, *, n_heads_b, q_off, k_off, v_off, blocks_per_iter=16):
    B, _, S, _ = qkv_heads.shape
    offs_nat = (q_off, k_off, v_off)
    offs_dil = (0, n_heads_b, 2 * n_heads_b)
    in_arrays = [qkv_heads] * 3 + [d4] * 3 + [d16] * 3
    in_specs = (
        [pl.BlockSpec((1, 1, S, HEAD_DIM), lambda b, h, off=off: (b, off + h, 0, 0))
         for off in offs_nat]
        + [pl.BlockSpec((1, 1, 4, S // 4, HEAD_DIM), lambda b, h, off=off: (b, off + h, 0, 0, 0))
           for off in offs_dil]
        + [pl.BlockSpec((1, 1, 16, S // 16, HEAD_DIM), lambda b, h, off=off: (b, off + h, 0, 0, 0))
           for off in offs_dil])
    n_br = len(DIL_PAIRS)
    return pl.pallas_call(
        functools.partial(_dilated_kernel, seq_len=S, blocks_per_iter=blocks_per_iter,
                          combine_rows=64),
        grid=(B, n_heads_b),
        in_specs=in_specs,
        out_specs=pl.BlockSpec((1, 1, S, HEAD_DIM), lambda b, h: (b, h, 0, 0)),
        out_shape=jax.ShapeDtypeStruct((B, n_heads_b, S, HEAD_DIM), F32),
        scratch_shapes=[pltpu.VMEM((n_br, S, HEAD_DIM), F32),
                        pltpu.VMEM((n_br, S, HEAD_DIM), F32),
                        pltpu.VMEM((2, DIL_BLOCK, 2 * DIL_BLOCK), F32)],
        compiler_params=pltpu.CompilerParams(
            dimension_semantics=("parallel", "parallel"),
            vmem_limit_bytes=V7X_VMEM_LIMIT_BYTES),
        name="dilated_attn",
    )(*in_arrays)


def _attnout_kernel(x_ref, oa_ref, ob_ref, ga_ref, gb_ref, wo_ref, pg_ref, fg_ref,
                    out_ref, hn_ref, mix_ref, *, row_chunks):
    tm = x_ref.shape[1]
    rows_per_chunk = tm // row_chunks
    for c in range(row_chunks):
        rows = slice(c * rows_per_chunk, (c + 1) * rows_per_chunk)
        col = 0
        for o_ref, g_ref in ((oa_ref, ga_ref), (ob_ref, gb_ref)):
            n_heads = o_ref.shape[1]
            ssq = None
            for h in range(n_heads):
                t = o_ref[0, h, rows, :]
                part = jnp.sum(t * t, axis=-1, keepdims=True)
                ssq = part if ssq is None else ssq + part
            inv = lax.rsqrt(ssq / (n_heads * HEAD_DIM) + RMS_EPS)
            for h in range(n_heads):
                gs = slice(h * HEAD_DIM, (h + 1) * HEAD_DIM)
                mix_ref[rows, col:col + HEAD_DIM] = (
                    o_ref[0, h, rows, :] * inv * g_ref[:, gs]).astype(BF16)
                col += HEAD_DIM
        y = jnp.dot(mix_ref[rows, :], wo_ref[...], preferred_element_type=F32)
        x_new = x_ref[0, rows, :] + y * _rms_scale(y) * pg_ref[...]
        out_ref[0, rows, :] = x_new
        hn_ref[0, rows, :] = (x_new * _rms_scale(x_new) * fg_ref[...]).astype(BF16)


def _attnout_call(x, oa, ob, ga, gb, wo_bf16, pg, ffn_g, *, tm=512, row_chunks=2):
    B, S, D = x.shape
    head_spec_a = pl.BlockSpec((1, oa.shape[1], tm, HEAD_DIM), lambda b, i: (b, 0, i, 0))
    head_spec_b = pl.BlockSpec((1, ob.shape[1], tm, HEAD_DIM), lambda b, i: (b, 0, i, 0))
    row_spec = pl.BlockSpec((1, tm, D), lambda b, i: (b, i, 0))
    return pl.pallas_call(
        functools.partial(_attnout_kernel, row_chunks=row_chunks),
        grid=(B, S // tm),
        in_specs=[
            row_spec,
            head_spec_a, head_spec_b,
            pl.BlockSpec((1, ga.shape[1]), lambda b, i: (0, 0)),
            pl.BlockSpec((1, gb.shape[1]), lambda b, i: (0, 0)),
            pl.BlockSpec(wo_bf16.shape, lambda b, i: (0, 0)),
            pl.BlockSpec((1, D), lambda b, i: (0, 0)),
            pl.BlockSpec((1, D), lambda b, i: (0, 0)),
        ],
        out_specs=[row_spec, row_spec],
        out_shape=[jax.ShapeDtypeStruct((B, S, D), F32), jax.ShapeDtypeStruct((B, S, D), BF16)],
        scratch_shapes=[pltpu.VMEM((tm, wo_bf16.shape[0]), BF16)],
        compiler_params=pltpu.CompilerParams(
            dimension_semantics=("parallel", "parallel"),
            vmem_limit_bytes=V7X_VMEM_LIMIT_BYTES),
        name="attn_out",
    )(x, oa, ob, ga, gb, wo_bf16, pg, ffn_g)


def _ffn_kernel(x_ref, hn_ref, wg_ref, wu_ref, cw_ref, cb_ref, wd_ref, pg_ref,
                out_ref, tail_ref, act_ref):
    i = pl.program_id(1)
    f = pl.program_id(2)
    tm = x_ref.shape[1]
    halo = tail_ref.shape[1]

    @pl.when(i == 0)
    def _():
        tail_ref[f] = jnp.zeros(tail_ref.shape[1:], F32)

    @pl.when(f == 0)
    def _():
        out_ref[0] = jnp.zeros(out_ref.shape[1:], F32)

    def step(is_last):
        gate = jnp.dot(hn_ref[0], wg_ref[...], preferred_element_type=F32)
        up = jnp.dot(hn_ref[0], wu_ref[...], preferred_element_type=F32)
        cw = cw_ref[...]
        cb = cb_ref[...]

        def gated(g_m2, g_m1, g_0, u):
            gc = cb + g_m2 * cw[0:1, :] + g_m1 * cw[1:2, :] + g_0 * cw[2:3, :]
            gelu = 0.5 * gc * (1.0 + jnp.tanh(0.7978845608028654 * (gc + 0.044715 * (gc * gc * gc))))
            return (gelu * u).astype(BF16)

        act_ref[...] = gated(pltpu.roll(gate, 2, 0), pltpu.roll(gate, 1, 0), gate, up)
        head = BF16_SUBLANE_TILE
        ext = jnp.concatenate([tail_ref[f], gate[0:head]], axis=0)
        act_ref[0:head, :] = gated(ext[halo - 2:halo - 2 + head], ext[halo - 1:halo - 1 + head],
                                   gate[0:head], up[0:head])
        tail_ref[f] = gate[tm - halo:, :]
        if not is_last:
            out_ref[0] += jnp.dot(act_ref[...], wd_ref[...], preferred_element_type=F32)
        else:
            half = tm // FFN_LAST_STEP_ROW_CHUNKS
            for r0 in range(0, tm, half):
                rows = slice(r0, r0 + half)
                y = out_ref[0, rows, :] + jnp.dot(act_ref[rows, :], wd_ref[...],
                                                  preferred_element_type=F32)
                out_ref[0, rows, :] = x_ref[0, rows, :] + y * _rms_scale(y) * pg_ref[...]

    last = pl.num_programs(2) - 1
    pl.when(f < last)(lambda: step(False))
    pl.when(f == last)(lambda: step(True))


def _ffn_call(x, hn, wg_bf16, wu_bf16, conv_w, conv_b, wd_bf16, pg, *, tm=512, tf=1024):
    B, S, D = x.shape
    d_ff = wg_bf16.shape[1]
    halo = F32_SUBLANE_TILE
    assert CONV_WIDTH - 1 <= halo
    return pl.pallas_call(
        _ffn_kernel,
        grid=(B, S // tm, d_ff // tf),
        in_specs=[
            pl.BlockSpec((1, tm, D), lambda b, i, f: (b, i, 0)),
            pl.BlockSpec((1, tm, D), lambda b, i, f: (b, i, 0)),
            pl.BlockSpec((D, tf), lambda b, i, f: (0, f)),
            pl.BlockSpec((D, tf), lambda b, i, f: (0, f)),
            pl.BlockSpec((CONV_WIDTH, tf), lambda b, i, f: (0, f)),
            pl.BlockSpec((1, tf), lambda b, i, f: (0, f)),
            pl.BlockSpec((tf, D), lambda b, i, f: (f, 0)),
            pl.BlockSpec((1, D), lambda b, i, f: (0, 0)),
        ],
        out_specs=pl.BlockSpec((1, tm, D), lambda b, i, f: (b, i, 0)),
        out_shape=jax.ShapeDtypeStruct((B, S, D), F32),
        scratch_shapes=[pltpu.VMEM((d_ff // tf, halo, tf), F32), pltpu.VMEM((tm, tf), BF16)],
        compiler_params=pltpu.CompilerParams(
            dimension_semantics=("parallel", "arbitrary", "arbitrary"),
            vmem_limit_bytes=V7X_VMEM_LIMIT_BYTES),
        name="conv_glu_ffn",
    )(x, hn, wg_bf16, wu_bf16, conv_w, conv_b, wd_bf16, pg)


def kernel(x, positions, attn_pre_g, w_qkv, moba_out_g, dil_out_g, w_o, attn_post_g, ffn_pre_g,
           w_gate, w_up, conv_w, conv_b, w_down, ffn_post_g):
    depth = w_qkv.shape[0]
    n_heads_a = moba_out_g.shape[1] // HEAD_DIM
    n_heads_b = dil_out_g.shape[1] // HEAD_DIM
    for l in range(depth):
        qkv_heads, vt, d4, d16 = _qkv_call(x, attn_pre_g[l][None], w_qkv[l].astype(BF16),
                                           positions, n_heads_a=n_heads_a, n_heads_b=n_heads_b)
        oa, (wo_bf16, wg_bf16, wu_bf16, wd_bf16) = _moba_call(
            qkv_heads, vt, [w_o[l], w_gate[l], w_up[l], w_down[l]], n_heads_a=n_heads_a)
        ob = _dilated_call(qkv_heads, d4, d16, n_heads_b=n_heads_b, q_off=3 * n_heads_a,
                           k_off=3 * n_heads_a + n_heads_b, v_off=3 * n_heads_a + 2 * n_heads_b)
        x, hn = _attnout_call(x, oa, ob, moba_out_g[l][None], dil_out_g[l][None],
                              wo_bf16, attn_post_g[l][None], ffn_pre_g[l][None])
        x = _ffn_call(x, hn, wg_bf16, wu_bf16, conv_w[l], conv_b[l][None], wd_bf16,
                      ffn_post_g[l][None])
    return x
```

```python
import functools

import jax
import jax.numpy as jnp
from jax import lax
from jax.experimental import pallas as pl
from jax.experimental.pallas import tpu as pltpu

F32 = jnp.float32
BF16 = jnp.bfloat16

HEAD_DIM = 128
ROT_DIM = HEAD_DIM // 4
ROT_HALF = ROT_DIM // 2
ROPE_THETA = 500000.0
MOBA_BLOCK = 256
MOBA_TOPK = 3
DIL_PAIRS = ((128, 1), (512, 4), (2048, 16))
DIL_BLOCK = 128
CONV_WIDTH = 3
RMS_EPS = 1e-6
SCALE = HEAD_DIM ** -0.5
SCALE_LOG2E = SCALE * 1.4426950408889634
NEG = -1e30

V7X_VMEM_LIMIT_BYTES = 56 * 1024 * 1024
BF16_SUBLANE_TILE = 16
V7X_MXU_WIDTH = 256
F32_SUBLANE_TILE = 8
FFN_LAST_STEP_ROW_CHUNKS = 2
QKV_SLAB_SLOTS = 2
MOBA_VT_ROWS = HEAD_DIM + BF16_SUBLANE_TILE
MOBA_LOOP_GROUP = 2

NT_DIMS = (((1,), (1,)), ((), ()))


def _rms_scale(x):
    return lax.rsqrt(jnp.mean(x * x, axis=-1, keepdims=True) + RMS_EPS)


def _qkv_kernel(x_ref, g_ref, w_ref, pos_ref, freq_ref,
                nat_ref, vt_ref, d4_ref, d16_ref, hn_ref, slab_ref, slab4_ref, *, heads_per_seg):
    tm = hn_ref.shape[0]
    heads_per_dot = V7X_MXU_WIDTH // HEAD_DIM
    dot_width = heads_per_dot * HEAD_DIM
    seg_width = heads_per_seg * HEAD_DIM

    x = x_ref[0]
    hn_ref[...] = (x * _rms_scale(x) * g_ref[...]).astype(BF16)

    ang = pos_ref[0].astype(F32) * freq_ref[...]
    cos_t, sin_t = jnp.cos(ang), jnp.sin(ang)
    lane = lax.broadcasted_iota(jnp.int32, ang.shape, 1)
    sa_t = jnp.where(lane < ROT_HALF, -sin_t, 0.0)
    sb_t = jnp.where(lane >= ROT_HALF, sin_t, 0.0)

    def make_rope(scale):
        cos, sa, sb = cos_t * scale, sa_t * scale, sb_t * scale

        def rope(t):
            return (t * cos + pltpu.roll(t, HEAD_DIM - ROT_HALF, 1) * sa
                    + pltpu.roll(t, ROT_HALF, 1) * sb)
        return rope

    rope_q, rope_k = make_rope(SCALE_LOG2E), make_rope(1.0)

    for seg in (2, 0, 1, 3, 4, 5):
        rope = {0: rope_q, 1: rope_k, 3: rope_q, 4: rope_k}.get(seg)
        with_vt = seg == 2
        with_dilated = seg >= 3
        for c in range(heads_per_seg // heads_per_dot):
            col = seg * seg_width + c * dot_width
            acc = jnp.dot(hn_ref[...], w_ref[:, col:col + dot_width], preferred_element_type=F32)
            for hh in range(heads_per_dot):
                h = c * heads_per_dot + hh
                t = acc[:, hh * HEAD_DIM:(hh + 1) * HEAD_DIM]
                if rope is not None:
                    t = rope(t)
                nat_ref[0, seg * heads_per_seg + h] = t.astype(BF16)
                if with_vt:
                    for blk in range(tm // MOBA_BLOCK):
                        vt_ref[0, h, blk, 0:HEAD_DIM, :] = (
                            t[blk * MOBA_BLOCK:(blk + 1) * MOBA_BLOCK].T.astype(BF16))
                        vt_ref[0, h, blk, HEAD_DIM:, :] = jnp.ones(
                            (MOBA_VT_ROWS - HEAD_DIM, MOBA_BLOCK), BF16)
                if with_dilated:
                    hd = (seg - 3) * heads_per_seg + h
                    slot = h % slab_ref.shape[0]
                    q4 = tm // 4
                    slab_ref[slot] = t
                    for r4 in range(4):
                        cls = slab_ref[slot, pl.ds(r4, q4, stride=4), :]
                        d4_ref[0, hd, r4] = cls.astype(BF16)
                        slab4_ref[slot, r4 * q4:(r4 + 1) * q4, :] = cls
                    for r4 in range(4):
                        for m in range(4):
                            cls = slab4_ref[slot, pl.ds(r4 * q4 + m, q4 // 4, stride=4), :]
                            d16_ref[0, hd, r4 + 4 * m] = cls.astype(BF16)


def _qkv_call(x, g, w_bf16, positions, *, n_heads_a, n_heads_b, tm=256):
    B, S, D = x.shape
    half_freq = ROPE_THETA ** (-jnp.arange(ROT_HALF, dtype=F32) / ROT_HALF)
    lane_freq = jnp.concatenate(
        [half_freq, half_freq, jnp.zeros((HEAD_DIM - ROT_DIM,), F32)])[None]
    pos_col = positions.reshape(B, S, 1)
    N = w_bf16.shape[1]
    assert n_heads_a == n_heads_b and N == 3 * (n_heads_a + n_heads_b) * HEAD_DIM
    assert tm % MOBA_BLOCK == 0 and (tm // 16) % BF16_SUBLANE_TILE == 0
    n_col_heads = N // HEAD_DIM
    kern = functools.partial(_qkv_kernel, heads_per_seg=n_heads_a)
    return pl.pallas_call(
        kern,
        grid=(B, S // tm),
        in_specs=[
            pl.BlockSpec((1, tm, D), lambda b, i: (b, i, 0)),
            pl.BlockSpec((1, D), lambda b, i: (0, 0)),
            pl.BlockSpec((D, N), lambda b, i: (0, 0), pipeline_mode=pl.Buffered(1)),
            pl.BlockSpec((1, tm, 1), lambda b, i: (b, i, 0)),
            pl.BlockSpec((1, HEAD_DIM), lambda b, i: (0, 0)),
        ],
        out_specs=[
            pl.BlockSpec((1, n_col_heads, tm, HEAD_DIM), lambda b, i: (b, 0, i, 0)),
            pl.BlockSpec((1, n_heads_a, tm // MOBA_BLOCK, MOBA_VT_ROWS, MOBA_BLOCK),
                         lambda b, i: (b, 0, i, 0, 0)),
            pl.BlockSpec((1, 3 * n_heads_b, 4, tm // 4, HEAD_DIM), lambda b, i: (b, 0, 0, i, 0)),
            pl.BlockSpec((1, 3 * n_heads_b, 16, tm // 16, HEAD_DIM), lambda b, i: (b, 0, 0, i, 0)),
        ],
        out_shape=[
            jax.ShapeDtypeStruct((B, n_col_heads, S, HEAD_DIM), BF16),
            jax.ShapeDtypeStruct((B, n_heads_a, S // MOBA_BLOCK, MOBA_VT_ROWS, MOBA_BLOCK), BF16),
            jax.ShapeDtypeStruct((B, 3 * n_heads_b, 4, S // 4, HEAD_DIM), BF16),
            jax.ShapeDtypeStruct((B, 3 * n_heads_b, 16, S // 16, HEAD_DIM), BF16),
        ],
        scratch_shapes=[pltpu.VMEM((tm, D), BF16),
                        pltpu.VMEM((QKV_SLAB_SLOTS, tm, HEAD_DIM), F32),
                        pltpu.VMEM((QKV_SLAB_SLOTS, tm, HEAD_DIM), F32)],
        compiler_params=pltpu.CompilerParams(
            dimension_semantics=("parallel", "parallel"),
            vmem_limit_bytes=V7X_VMEM_LIMIT_BYTES),
        name="qkv_rope",
    )(x, g, w_bf16, pos_col, lane_freq)


def _moba_kernel(*refs, n_blocks, heads, n_cast, lookahead=8, loop_lookahead=2,
                 group=MOBA_LOOP_GROUP):
    q_ref, k_ref, vt_ref = refs[:3]
    cast_in = refs[3:3 + n_cast]
    o_ref = refs[3 + n_cast]
    cast_out = refs[4 + n_cast:4 + 2 * n_cast]
    kmean_ref, bias_ref, acc_ref, m_ref, sa_ref, sb_ref = refs[4 + 2 * n_cast:]
    assert n_blocks % group == 0
    i = pl.program_id(2)
    blk_sz = MOBA_BLOCK

    for w_in, w_out in zip(cast_in, cast_out):
        w_out[0] = w_in[0].astype(BF16)

    @pl.when(i == 0)
    def _():
        for h in range(heads):
            for blk in range(n_blocks):
                kb = k_ref[0, h, blk * blk_sz:(blk + 1) * blk_sz, :].astype(F32)
                kmean_ref[h, blk:blk + 1, :] = jnp.mean(kb, axis=0, keepdims=True)

    def select_blocks(h, q):
        km = kmean_ref[h]
        km_hi = km.astype(BF16)
        km_lo = (km - km_hi.astype(F32)).astype(BF16)
        gate = (lax.dot_general(km_hi, q, NT_DIMS, preferred_element_type=F32)
                + lax.dot_general(km_lo, q, NT_DIMS, preferred_element_type=F32))
        blk_id = lax.broadcasted_iota(jnp.int32, gate.shape, 0).astype(F32)
        neg_inf = jnp.float32(-jnp.inf)
        g = jnp.where(blk_id < i.astype(F32), gate, neg_inf)
        sel = jnp.zeros(gate.shape, dtype=jnp.bool_)
        for _ in range(MOBA_TOPK):
            m = jnp.max(g, axis=0, keepdims=True)
            first = jnp.min(jnp.where(g == m, blk_id, float(n_blocks)), axis=0, keepdims=True)
            pick = jnp.logical_and(blk_id == first, m > neg_inf)
            sel = jnp.logical_or(sel, pick)
            g = jnp.where(pick, neg_inf, g)
        bias_ref[h] = jnp.where(sel, 0.0, NEG).astype(F32)

    def scores(h, first_blk, n_blk):
        rows = n_blk * blk_sz
        kb = k_ref[0, h, pl.ds(pl.multiple_of(first_blk * blk_sz, blk_sz), rows), :]
        return lax.dot_general(kb, q_ref[0, h], NT_DIMS, preferred_element_type=F32)

    def pipelined(stage_a, stage_b, depth=lookahead):
        ahead = [stage_a(h) for h in range(min(depth, heads))]
        outs = []
        for h in range(heads):
            if h + depth < heads:
                ahead.append(stage_a(h + depth))
            outs.append(stage_b(h, ahead[h]))
        return outs

    stage = (sa_ref, sb_ref)

    def own_a(h):
        select_blocks(h, q_ref[0, h])
        s_own = scores(h, i, 1)
        stage[0][h] = scores(h, 0, group)
        return s_own

    def own_b(h, s):
        key_pos = lax.broadcasted_iota(jnp.int32, s.shape, 0)
        q_pos = lax.broadcasted_iota(jnp.int32, s.shape, 1)
        t = jnp.where(key_pos <= q_pos, s, NEG)
        m0 = jnp.max(t, axis=0, keepdims=True)
        p = jnp.exp2(t - m0)
        acc_ref[h] = jnp.dot(vt_ref[0, h, i], p.astype(BF16), preferred_element_type=F32)
        return m0

    for h, m0 in enumerate(pipelined(own_a, own_b)):
        m_ref[h] = m0

    n_groups = (i + group - 1) // group
    last_group = n_blocks // group - 1

    def consume_and_prefetch(c, cur_ref, nxt_ref):
        first = c * group
        nxt_first = jnp.minimum(c + 1, last_group) * group

        def next_scores(h):
            nxt_ref[h] = scores(h, nxt_first, group)

        def past_b(h, _):
            m_prev = m_ref[h]
            s = cur_ref[h]
            chunks = [s[g * blk_sz:(g + 1) * blk_sz] for g in range(group)]
            brows = [bias_ref[h, pl.ds(first + g, 1), :] for g in range(group)]
            m_new = m_prev
            for sg, brow in zip(chunks, brows):
                m_new = jnp.maximum(m_new, jnp.max(sg, axis=0, keepdims=True) + brow)
            acc_new = jnp.exp2(m_prev - m_new) * acc_ref[h]
            for g, (sg, brow) in enumerate(zip(chunks, brows)):
                pg = jnp.exp2(sg - (m_new - brow))
                acc_new = acc_new + jnp.dot(vt_ref[0, h, first + g], pg.astype(BF16),
                                            preferred_element_type=F32)
            acc_ref[h] = acc_new
            m_ref[h] = m_new

        pipelined(next_scores, past_b, depth=loop_lookahead)

    def body(c, carry):
        for parity in range(2):
            @pl.when(c % 2 == parity)
            def _(parity=parity):
                consume_and_prefetch(c, stage[parity], stage[1 - parity])
        return carry

    lax.fori_loop(0, n_groups, body, 0)
    for h in range(heads):
        acc_fin = acc_ref[h]
        o_ref[0, h] = (acc_fin[:HEAD_DIM] / acc_fin[HEAD_DIM:HEAD_DIM + 1]).T


def _moba_call(qkv_heads, vt, f32_weights, *, n_heads_a, heads_per_step=8):
    B, _, S, _ = qkv_heads.shape
    n_blocks = S // MOBA_BLOCK
    tq = MOBA_BLOCK
    hps = heads_per_step
    assert n_heads_a % hps == 0
    n_groups = n_heads_a // hps
    k_first = n_groups
    grid = (B, n_groups, S // tq)
    n_steps = grid[0] * grid[1] * grid[2]

    def slab_index(b, h, i):
        return ((b * grid[1] + h) * grid[2] + i, 0, 0)

    slabs, slab_specs, slab_shapes = [], [], []
    for w in f32_weights:
        rows, cols = w.shape
        assert rows % (n_steps * BF16_SUBLANE_TILE) == 0
        slabs.append(w.reshape(n_steps, rows // n_steps, cols))
        slab_specs.append(pl.BlockSpec((1, rows // n_steps, cols), slab_index))
        slab_shapes.append(jax.ShapeDtypeStruct((n_steps, rows // n_steps, cols), BF16))

    kern = functools.partial(_moba_kernel, n_blocks=n_blocks, heads=hps, n_cast=len(slabs))
    outs = pl.pallas_call(
        kern,
        grid=grid,
        in_specs=[
            pl.BlockSpec((1, hps, tq, HEAD_DIM), lambda b, h, i: (b, h, i, 0)),
            pl.BlockSpec((1, hps, S, HEAD_DIM), lambda b, h, i: (b, k_first + h, 0, 0)),
            pl.BlockSpec((1, hps, n_blocks, MOBA_VT_ROWS, MOBA_BLOCK),
                         lambda b, h, i: (b, h, 0, 0, 0), pipeline_mode=pl.Buffered(1)),
        ] + slab_specs,
        out_specs=[pl.BlockSpec((1, hps, tq, HEAD_DIM), lambda b, h, i: (b, h, i, 0))] + slab_specs,
        out_shape=[jax.ShapeDtypeStruct((B, n_heads_a, S, HEAD_DIM), F32)] + slab_shapes,
        scratch_shapes=[pltpu.VMEM((hps, n_blocks, HEAD_DIM), F32),
                        pltpu.VMEM((hps, n_blocks, tq), F32),
                        pltpu.VMEM((hps, MOBA_VT_ROWS, tq), F32),
                        pltpu.VMEM((hps, 1, tq), F32),
                        pltpu.VMEM((hps, MOBA_LOOP_GROUP * MOBA_BLOCK, tq), F32),
                        pltpu.VMEM((hps, MOBA_LOOP_GROUP * MOBA_BLOCK, tq), F32)],
        compiler_params=pltpu.CompilerParams(
            dimension_semantics=("parallel", "parallel", "arbitrary"),
            vmem_limit_bytes=V7X_VMEM_LIMIT_BYTES),
        name="moba_attn",
    )(qkv_heads, qkv_heads, vt, *slabs)
    return outs[0], [o.reshape(w.shape) for o, w in zip(outs[1:], f32_weights)]


def _dilated_kernel(q1_ref, k1_ref, v1_ref, q4_ref, k4_ref, v4_ref, q16_ref, k16_ref, v16_ref,
                    ob_ref, o_scr, lse_scr, bias_scr, *, seq_len, blocks_per_iter, combine_rows):
    blk = DIL_BLOCK
    branch_refs = ((q1_ref, k1_ref, v1_ref), (q4_ref, k4_ref, v4_ref), (q16_ref, k16_ref, v16_ref))

    qi = lax.broadcasted_iota(jnp.int32, (blk, 2 * blk), 0)
    ki = lax.broadcasted_iota(jnp.int32, (blk, 2 * blk), 1)
    dist = qi + blk - ki
    bias_scr[0] = jnp.where(jnp.logical_and(dist >= 0, dist <= blk), 0.0, NEG).astype(F32)
    bias_scr[1] = jnp.where(ki <= qi, 0.0, NEG).astype(F32)

    for g, (window, d) in enumerate(DIL_PAIRS):
        assert window // d == DIL_BLOCK
        q_ref, k_ref, v_ref = branch_refs[g]
        n_blk = seq_len // d // blk

        def rows_of(ref, r, start, size, d=d):
            if d == 1:
                return ref[0, 0, pl.ds(start, size), :]
            return ref[0, 0, r, pl.ds(start, size), :]

        def key_start(n):
            return pl.multiple_of(jnp.maximum(n - 1, 0) * blk, blk)

        def scores(r, n, q_ref=q_ref, k_ref=k_ref, rows_of=rows_of):
            qb = rows_of(q_ref, r, pl.multiple_of(n * blk, blk), blk)
            kb = rows_of(k_ref, r, key_start(n), 2 * blk)
            return lax.dot_general(qb, kb, NT_DIMS, preferred_element_type=F32)

        def finish(r, n, s, g=g, d=d, v_ref=v_ref, rows_of=rows_of):
            t = s + bias_scr[jnp.where(n == 0, 1, 0)]
            m = jnp.max(t, axis=-1, keepdims=True)
            p = jnp.exp2(t - m)
            vb = rows_of(v_ref, r, key_start(n), 2 * blk)
            v_ones = jnp.concatenate([vb, jnp.ones_like(vb)], axis=1)
            pv = jnp.dot(p.astype(BF16), v_ones, preferred_element_type=F32)
            den = pv[:, HEAD_DIM:]
            q_start = pl.multiple_of(n * blk, blk)
            if d == 1:
                rows = pl.ds(q_start, blk)
            elif d == 4:
                rows = pl.ds(r * (seq_len // 4) + q_start, blk)
            else:
                r4, m4 = r % 4, r // 4
                rows = pl.ds(r4 * (seq_len // 4) + 4 * q_start + m4, blk, stride=4)
            o_scr[g, rows, :] = pv[:, :HEAD_DIM] / den
            lse_scr[g, rows, :] = m + jnp.log2(den)

        def run_blocks(tasks, scores=scores, finish=finish):
            nxt = scores(*tasks[0])
            for idx, (r, n) in enumerate(tasks):
                cur = nxt
                if idx + 1 < len(tasks):
                    nxt = scores(*tasks[idx + 1])
                finish(r, n, cur)

        n_per_iter = max(1, min(blocks_per_iter // d, n_blk))

        def body(it, carry, run_blocks=run_blocks, d=d, n_per_iter=n_per_iter):
            run_blocks([(r, it * n_per_iter + u) for u in range(n_per_iter) for r in range(d)])
            return carry

        lax.fori_loop(0, n_blk // n_per_iter, body, 0)

    def combine(c, carry):
        i0 = pl.multiple_of(c * combine_rows, combine_rows)
        for r4 in range(4):
            tok = pl.ds(4 * i0 + r4, combine_rows, stride=4)
            cls = pl.ds(r4 * (seq_len // 4) + i0, combine_rows)
            l1, l2, l3 = lse_scr[0, tok, :], lse_scr[1, cls, :], lse_scr[2, cls, :]
            lmax = jnp.maximum(jnp.maximum(l1, l2), l3)
            e1, e2, e3 = jnp.exp2(l1 - lmax), jnp.exp2(l2 - lmax), jnp.exp2(l3 - lmax)
            num = e1 * o_scr[0, tok, :] + e2 * o_scr[1, cls, :] + e3 * o_scr[2, cls, :]
            ob_ref[0, 0, tok, :] = num / (e1 + e2 + e3)
        return carry

    lax.fori_loop(0, seq_len // 4 // combine_rows, combine, 0)


def _dilated_call(qkv_heads, d4, d16, *, n_heads_b, q_off, k_off, v_off, blocks_per_iter=16):
    B, _, S, _ = qkv_heads.shape
    offs_nat = (q_off, k_off, v_off)
    offs_dil = (0, n_heads_b, 2 * n_heads_b)
    in_arrays = [qkv_heads] * 3 + [d4] * 3 + [d16] * 3
    in_specs = (
        [pl.BlockSpec((1, 1, S, HEAD_DIM), lambda b, h, off=off: (b, off + h, 0, 0))
         for off in offs_nat]
        + [pl.BlockSpec((1, 1, 4, S // 4, HEAD_DIM), lambda b, h, off=off: (b, off + h, 0, 0, 0))
           for off in offs_dil]
        + [pl.BlockSpec((1, 1, 16, S // 16, HEAD_DIM), lambda b, h, off=off: (b, off + h, 0, 0, 0))
           for off in offs_dil])
    n_br = len(DIL_PAIRS)
    return pl.pallas_call(
        functools.partial(_dilated_kernel, seq_len=S, blocks_per_iter=blocks_per_iter,
                          combine_rows=64),
        grid=(B, n_heads_b),
        in_specs=in_specs,
        out_specs=pl.BlockSpec((1, 1, S, HEAD_DIM), lambda b, h: (b, h, 0, 0)),
        out_shape=jax.ShapeDtypeStruct((B, n_heads_b, S, HEAD_DIM), F32),
        scratch_shapes=[pltpu.VMEM((n_br, S, HEAD_DIM), F32),
                        pltpu.VMEM((n_br, S, HEAD_DIM), F32),
                        pltpu.VMEM((2, DIL_BLOCK, 2 * DIL_BLOCK), F32)],
        compiler_params=pltpu.CompilerParams(
            dimension_semantics=("parallel", "parallel"),
            vmem_limit_bytes=V7X_VMEM_LIMIT_BYTES),
        name="dilated_attn",
    )(*in_arrays)


def _attnout_kernel(x_ref, oa_ref, ob_ref, ga_ref, gb_ref, wo_ref, pg_ref, fg_ref,
                    out_ref, hn_ref, mix_ref, *, row_chunks):
    tm = x_ref.shape[1]
    rows_per_chunk = tm // row_chunks
    for c in range(row_chunks):
        rows = slice(c * rows_per_chunk, (c + 1) * rows_per_chunk)
        col = 0
        for o_ref, g_ref in ((oa_ref, ga_ref), (ob_ref, gb_ref)):
            n_heads = o_ref.shape[1]
            ssq = None
            for h in range(n_heads):
                t = o_ref[0, h, rows, :]
                part = jnp.sum(t * t, axis=-1, keepdims=True)
                ssq = part if ssq is None else ssq + part
            inv = lax.rsqrt(ssq / (n_heads * HEAD_DIM) + RMS_EPS)
            for h in range(n_heads):
                gs = slice(h * HEAD_DIM, (h + 1) * HEAD_DIM)
                mix_ref[rows, col:col + HEAD_DIM] = (
                    o_ref[0, h, rows, :] * inv * g_ref[:, gs]).astype(BF16)
                col += HEAD_DIM
        y = jnp.dot(mix_ref[rows, :], wo_ref[...], preferred_element_type=F32)
        x_new = x_ref[0, rows, :] + y * _rms_scale(y) * pg_ref[...]
        out_ref[0, rows, :] = x_new
        hn_ref[0, rows, :] = (x_new * _rms_scale(x_new) * fg_ref[...]).astype(BF16)


def _attnout_call(x, oa, ob, ga, gb, wo_bf16, pg, ffn_g, *, tm=512, row_chunks=4):
    B, S, D = x.shape
    head_spec_a = pl.BlockSpec((1, oa.shape[1], tm, HEAD_DIM), lambda b, i: (b, 0, i, 0))
    head_spec_b = pl.BlockSpec((1, ob.shape[1], tm, HEAD_DIM), lambda b, i: (b, 0, i, 0))
    row_spec = pl.BlockSpec((1, tm, D), lambda b, i: (b, i, 0))
    return pl.pallas_call(
        functools.partial(_attnout_kernel, row_chunks=row_chunks),
        grid=(B, S // tm),
        in_specs=[
            row_spec,
            head_spec_a, head_spec_b,
            pl.BlockSpec((1, ga.shape[1]), lambda b, i: (0, 0)),
            pl.BlockSpec((1, gb.shape[1]), lambda b, i: (0, 0)),
            pl.BlockSpec(wo_bf16.shape, lambda b, i: (0, 0)),
            pl.BlockSpec((1, D), lambda b, i: (0, 0)),
            pl.BlockSpec((1, D), lambda b, i: (0, 0)),
        ],
        out_specs=[row_spec, row_spec],
        out_shape=[jax.ShapeDtypeStruct((B, S, D), F32), jax.ShapeDtypeStruct((B, S, D), BF16)],
        scratch_shapes=[pltpu.VMEM((tm, wo_bf16.shape[0]), BF16)],
        compiler_params=pltpu.CompilerParams(
            dimension_semantics=("parallel", "parallel"),
            vmem_limit_bytes=V7X_VMEM_LIMIT_BYTES),
        name="attn_out",
    )(x, oa, ob, ga, gb, wo_bf16, pg, ffn_g)


def _ffn_kernel(x_ref, hn_ref, wg_ref, wu_ref, cw_ref, cb_ref, wd_ref, pg_ref,
                out_ref, tail_ref, act_ref):
    i = pl.program_id(1)
    f = pl.program_id(2)
    tm = x_ref.shape[1]
    halo = tail_ref.shape[1]

    @pl.when(i == 0)
    def _():
        tail_ref[f] = jnp.zeros(tail_ref.shape[1:], F32)

    def step(kind):
        gate = jnp.dot(hn_ref[0], wg_ref[...], preferred_element_type=F32)
        up = jnp.dot(hn_ref[0], wu_ref[...], preferred_element_type=F32)
        cw = cw_ref[...]
        cb = cb_ref[...]

        def gated(g_m2, g_m1, g_0, u):
            gc = cb + g_m2 * cw[0:1, :] + g_m1 * cw[1:2, :] + g_0 * cw[2:3, :]
            gelu = 0.5 * gc * (1.0 + jnp.tanh(0.7978845608028654 * (gc + 0.044715 * (gc * gc * gc))))
            return (gelu * u).astype(BF16)

        act_ref[...] = gated(pltpu.roll(gate, 2, 0), pltpu.roll(gate, 1, 0), gate, up)
        head = BF16_SUBLANE_TILE
        ext = jnp.concatenate([tail_ref[f], gate[0:head]], axis=0)
        act_ref[0:head, :] = gated(ext[halo - 2:halo - 2 + head], ext[halo - 1:halo - 1 + head],
                                   gate[0:head], up[0:head])
        tail_ref[f] = gate[tm - halo:, :]
        if kind == "first":
            out_ref[0] = jnp.dot(act_ref[...], wd_ref[...], preferred_element_type=F32)
        elif kind == "middle":
            out_ref[0] += jnp.dot(act_ref[...], wd_ref[...], preferred_element_type=F32)
        else:
            half = tm // FFN_LAST_STEP_ROW_CHUNKS
            for r0 in range(0, tm, half):
                rows = slice(r0, r0 + half)
                y = out_ref[0, rows, :] + jnp.dot(act_ref[rows, :], wd_ref[...],
                                                  preferred_element_type=F32)
                out_ref[0, rows, :] = x_ref[0, rows, :] + y * _rms_scale(y) * pg_ref[...]

    last = pl.num_programs(2) - 1
    pl.when(f == 0)(lambda: step("first"))
    pl.when(jnp.logical_and(f > 0, f < last))(lambda: step("middle"))
    pl.when(f == last)(lambda: step("last"))


def _ffn_call(x, hn, wg_bf16, wu_bf16, conv_w, conv_b, wd_bf16, pg, *, tm=512, tf=1024):
    B, S, D = x.shape
    d_ff = wg_bf16.shape[1]
    halo = F32_SUBLANE_TILE
    assert CONV_WIDTH - 1 <= halo and d_ff // tf >= 2
    return pl.pallas_call(
        _ffn_kernel,
        grid=(B, S // tm, d_ff // tf),
        in_specs=[
            pl.BlockSpec((1, tm, D), lambda b, i, f: (b, i, 0)),
            pl.BlockSpec((1, tm, D), lambda b, i, f: (b, i, 0)),
            pl.BlockSpec((D, tf), lambda b, i, f: (0, f)),
            pl.BlockSpec((D, tf), lambda b, i, f: (0, f)),
            pl.BlockSpec((CONV_WIDTH, tf), lambda b, i, f: (0, f)),
            pl.BlockSpec((1, tf), lambda b, i, f: (0, f)),
            pl.BlockSpec((tf, D), lambda b, i, f: (f, 0)),
            pl.BlockSpec((1, D), lambda b, i, f: (0, 0)),
        ],
        out_specs=pl.BlockSpec((1, tm, D), lambda b, i, f: (b, i, 0)),
        out_shape=jax.ShapeDtypeStruct((B, S, D), F32),
        scratch_shapes=[pltpu.VMEM((d_ff // tf, halo, tf), F32), pltpu.VMEM((tm, tf), BF16)],
        compiler_params=pltpu.CompilerParams(
            dimension_semantics=("parallel", "arbitrary", "arbitrary"),
            vmem_limit_bytes=V7X_VMEM_LIMIT_BYTES),
        name="conv_glu_ffn",
    )(x, hn, wg_bf16, wu_bf16, conv_w, conv_b, wd_bf16, pg)


def kernel(x, positions, attn_pre_g, w_qkv, moba_out_g, dil_out_g, w_o, attn_post_g, ffn_pre_g,
           w_gate, w_up, conv_w, conv_b, w_down, ffn_post_g):
    depth = w_qkv.shape[0]
    n_heads_a = moba_out_g.shape[1] // HEAD_DIM
    n_heads_b = dil_out_g.shape[1] // HEAD_DIM
    for l in range(depth):
        qkv_heads, vt, d4, d16 = _qkv_call(x, attn_pre_g[l][None], w_qkv[l].astype(BF16),
                                           positions, n_heads_a=n_heads_a, n_heads_b=n_heads_b)
        oa, (wo_bf16, wg_bf16, wu_bf16, wd_bf16) = _moba_call(
            qkv_heads, vt, [w_o[l], w_gate[l], w_up[l], w_down[l]], n_heads_a=n_heads_a)
        ob = _dilated_call(qkv_heads, d4, d16, n_heads_b=n_heads_b, q_off=3 * n_heads_a,
                           k_off=3 * n_heads_a + n_heads_b, v_off=3 * n_heads_a + 2 * n_heads_b)
        x, hn = _attnout_call(x, oa, ob, moba_out_g[l][None], dil_out_g[l][None],
                              wo_bf16, attn_post_g[l][None], ffn_pre_g[l][None])
        x = _ffn_call(x, hn, wg_bf16, wu_bf16, conv_w[l], conv_b[l][None], wd_bf16,
                      ffn_post_g[l][None])
    return x
```

```python
import functools

import jax
import jax.numpy as jnp
from jax import lax
from jax.experimental import pallas as pl
from jax.experimental.pallas import tpu as pltpu

F32 = jnp.float32
BF16 = jnp.bfloat16

HEAD_DIM = 128
ROT_DIM = HEAD_DIM // 4
ROT_HALF = ROT_DIM // 2
ROPE_THETA = 500000.0
MOBA_BLOCK = 256
MOBA_TOPK = 3
DIL_PAIRS = ((128, 1), (512, 4), (2048, 16))
DIL_BLOCK = 128
CONV_WIDTH = 3
RMS_EPS = 1e-6
SCALE = HEAD_DIM ** -0.5
SCALE_LOG2E = SCALE * 1.4426950408889634
NEG = -1e30

V7X_VMEM_LIMIT_BYTES = 56 * 1024 * 1024
BF16_SUBLANE_TILE = 16
V7X_MXU_WIDTH = 256
F32_SUBLANE_TILE = 8
FFN_LAST_STEP_ROW_CHUNKS = 2
QKV_SLAB_SLOTS = 2
MOBA_VT_ROWS = HEAD_DIM + BF16_SUBLANE_TILE
MOBA_LOOP_GROUP = 2

NT_DIMS = (((1,), (1,)), ((), ()))


def _rms_scale(x):
    return lax.rsqrt(jnp.mean(x * x, axis=-1, keepdims=True) + RMS_EPS)


def _qkv_kernel(x_ref, g_ref, w_ref, pos_ref, freq_ref,
                nat_ref, vt_ref, d4_ref, d16_ref, hn_ref, slab_ref, slab4_ref, *, heads_per_seg):
    tm = hn_ref.shape[0]
    heads_per_dot = V7X_MXU_WIDTH // HEAD_DIM
    dot_width = heads_per_dot * HEAD_DIM
    seg_width = heads_per_seg * HEAD_DIM

    x = x_ref[0]
    hn_ref[...] = (x * _rms_scale(x) * g_ref[...]).astype(BF16)

    ang = pos_ref[0].astype(F32) * freq_ref[...]
    cos_t, sin_t = jnp.cos(ang), jnp.sin(ang)
    lane = lax.broadcasted_iota(jnp.int32, ang.shape, 1)
    sa_t = jnp.where(lane < ROT_HALF, -sin_t, 0.0)
    sb_t = jnp.where(lane >= ROT_HALF, sin_t, 0.0)

    def make_rope(scale):
        cos, sa, sb = cos_t * scale, sa_t * scale, sb_t * scale

        def rope(t):
            return (t * cos + pltpu.roll(t, HEAD_DIM - ROT_HALF, 1) * sa
                    + pltpu.roll(t, ROT_HALF, 1) * sb)
        return rope

    rope_q, rope_k = make_rope(SCALE_LOG2E), make_rope(1.0)

    for seg in (2, 0, 1, 3, 4, 5):
        rope = {0: rope_q, 1: rope_k, 3: rope_q, 4: rope_k}.get(seg)
        with_vt = seg == 2
        with_dilated = seg >= 3
        for c in range(heads_per_seg // heads_per_dot):
            col = seg * seg_width + c * dot_width
            acc = jnp.dot(hn_ref[...], w_ref[:, col:col + dot_width], preferred_element_type=F32)
            for hh in range(heads_per_dot):
                h = c * heads_per_dot + hh
                t = acc[:, hh * HEAD_DIM:(hh + 1) * HEAD_DIM]
                if rope is not None:
                    t = rope(t)
                nat_ref[0, seg * heads_per_seg + h] = t.astype(BF16)
                if with_vt:
                    for blk in range(tm // MOBA_BLOCK):
                        vt_ref[0, h, blk, 0:HEAD_DIM, :] = (
                            t[blk * MOBA_BLOCK:(blk + 1) * MOBA_BLOCK].T.astype(BF16))
                        vt_ref[0, h, blk, HEAD_DIM:, :] = jnp.ones(
                            (MOBA_VT_ROWS - HEAD_DIM, MOBA_BLOCK), BF16)
                if with_dilated:
                    hd = (seg - 3) * heads_per_seg + h
                    slot = h % slab_ref.shape[0]
                    q4 = tm // 4
                    slab_ref[slot] = t
                    for r4 in range(4):
                        cls = slab_ref[slot, pl.ds(r4, q4, stride=4), :]
                        d4_ref[0, hd, r4] = cls.astype(BF16)
                        slab4_ref[slot, r4 * q4:(r4 + 1) * q4, :] = cls
                    for r4 in range(4):
                        for m in range(4):
                            cls = slab4_ref[slot, pl.ds(r4 * q4 + m, q4 // 4, stride=4), :]
                            d16_ref[0, hd, r4 + 4 * m] = cls.astype(BF16)


def _qkv_call(x, g, w_bf16, positions, *, n_heads_a, n_heads_b, tm=256):
    B, S, D = x.shape
    half_freq = ROPE_THETA ** (-jnp.arange(ROT_HALF, dtype=F32) / ROT_HALF)
    lane_freq = jnp.concatenate(
        [half_freq, half_freq, jnp.zeros((HEAD_DIM - ROT_DIM,), F32)])[None]
    pos_col = positions.reshape(B, S, 1)
    N = w_bf16.shape[1]
    assert n_heads_a == n_heads_b and N == 3 * (n_heads_a + n_heads_b) * HEAD_DIM
    assert tm % MOBA_BLOCK == 0 and (tm // 16) % BF16_SUBLANE_TILE == 0
    n_col_heads = N // HEAD_DIM
    kern = functools.partial(_qkv_kernel, heads_per_seg=n_heads_a)
    return pl.pallas_call(
        kern,
        grid=(B, S // tm),
        in_specs=[
            pl.BlockSpec((1, tm, D), lambda b, i: (b, i, 0)),
            pl.BlockSpec((1, D), lambda b, i: (0, 0)),
            pl.BlockSpec((D, N), lambda b, i: (0, 0), pipeline_mode=pl.Buffered(1)),
            pl.BlockSpec((1, tm, 1), lambda b, i: (b, i, 0)),
            pl.BlockSpec((1, HEAD_DIM), lambda b, i: (0, 0)),
        ],
        out_specs=[
            pl.BlockSpec((1, n_col_heads, tm, HEAD_DIM), lambda b, i: (b, 0, i, 0)),
            pl.BlockSpec((1, n_heads_a, tm // MOBA_BLOCK, MOBA_VT_ROWS, MOBA_BLOCK),
                         lambda b, i: (b, 0, i, 0, 0)),
            pl.BlockSpec((1, 3 * n_heads_b, 4, tm // 4, HEAD_DIM), lambda b, i: (b, 0, 0, i, 0)),
            pl.BlockSpec((1, 3 * n_heads_b, 16, tm // 16, HEAD_DIM), lambda b, i: (b, 0, 0, i, 0)),
        ],
        out_shape=[
            jax.ShapeDtypeStruct((B, n_col_heads, S, HEAD_DIM), BF16),
            jax.ShapeDtypeStruct((B, n_heads_a, S // MOBA_BLOCK, MOBA_VT_ROWS, MOBA_BLOCK), BF16),
            jax.ShapeDtypeStruct((B, 3 * n_heads_b, 4, S // 4, HEAD_DIM), BF16),
            jax.ShapeDtypeStruct((B, 3 * n_heads_b, 16, S // 16, HEAD_DIM), BF16),
        ],
        scratch_shapes=[pltpu.VMEM((tm, D), BF16),
                        pltpu.VMEM((QKV_SLAB_SLOTS, tm, HEAD_DIM), F32),
                        pltpu.VMEM((QKV_SLAB_SLOTS, tm, HEAD_DIM), F32)],
        compiler_params=pltpu.CompilerParams(
            dimension_semantics=("parallel", "parallel"),
            vmem_limit_bytes=V7X_VMEM_LIMIT_BYTES),
        name="qkv_rope",
    )(x, g, w_bf16, pos_col, lane_freq)


def _moba_kernel(*refs, n_blocks, heads, n_cast, lookahead=8, loop_lookahead=2,
                 group=MOBA_LOOP_GROUP):
    q_ref, k_ref, vt_ref = refs[:3]
    cast_in = refs[3:3 + n_cast]
    o_ref = refs[3 + n_cast]
    cast_out = refs[4 + n_cast:4 + 2 * n_cast]
    kmean_ref, bias_ref, acc_ref, m_ref, sa_ref, sb_ref = refs[4 + 2 * n_cast:]
    assert n_blocks % group == 0
    i = pl.program_id(2)
    blk_sz = MOBA_BLOCK

    for w_in, w_out in zip(cast_in, cast_out):
        w_out[0] = w_in[0].astype(BF16)

    @pl.when(i == 0)
    def _():
        for h in range(heads):
            for blk in range(n_blocks):
                kb = k_ref[0, h, blk * blk_sz:(blk + 1) * blk_sz, :].astype(F32)
                kmean_ref[h, blk:blk + 1, :] = jnp.mean(kb, axis=0, keepdims=True)

    def select_blocks(h, q):
        km = kmean_ref[h]
        km_hi = km.astype(BF16)
        km_lo = (km - km_hi.astype(F32)).astype(BF16)
        gate = (lax.dot_general(km_hi, q, NT_DIMS, preferred_element_type=F32)
                + lax.dot_general(km_lo, q, NT_DIMS, preferred_element_type=F32))
        blk_id = lax.broadcasted_iota(jnp.int32, gate.shape, 0).astype(F32)
        neg_inf = jnp.float32(-jnp.inf)
        g = jnp.where(blk_id < i.astype(F32), gate, neg_inf)
        sel = jnp.zeros(gate.shape, dtype=jnp.bool_)
        for _ in range(MOBA_TOPK):
            m = jnp.max(g, axis=0, keepdims=True)
            first = jnp.min(jnp.where(g == m, blk_id, float(n_blocks)), axis=0, keepdims=True)
            pick = jnp.logical_and(blk_id == first, m > neg_inf)
            sel = jnp.logical_or(sel, pick)
            g = jnp.where(pick, neg_inf, g)
        bias_ref[h] = jnp.where(sel, 0.0, NEG).astype(F32)

    def scores(h, first_blk, n_blk):
        rows = n_blk * blk_sz
        kb = k_ref[0, h, pl.ds(pl.multiple_of(first_blk * blk_sz, blk_sz), rows), :]
        return lax.dot_general(kb, q_ref[0, h], NT_DIMS, preferred_element_type=F32)

    def pipelined(stage_a, stage_b, depth=lookahead):
        ahead = [stage_a(h) for h in range(min(depth, heads))]
        outs = []
        for h in range(heads):
            if h + depth < heads:
                ahead.append(stage_a(h + depth))
            outs.append(stage_b(h, ahead[h]))
        return outs

    stage = (sa_ref, sb_ref)

    def own_a(h):
        select_blocks(h, q_ref[0, h])
        s_own = scores(h, i, 1)
        stage[0][h] = scores(h, 0, group)
        return s_own

    def own_b(h, s):
        key_pos = lax.broadcasted_iota(jnp.int32, s.shape, 0)
        q_pos = lax.broadcasted_iota(jnp.int32, s.shape, 1)
        t = jnp.where(key_pos <= q_pos, s, NEG)
        m0 = jnp.max(t, axis=0, keepdims=True)
        p = jnp.exp2(t - m0)
        acc_ref[h] = jnp.dot(vt_ref[0, h, i], p.astype(BF16), preferred_element_type=F32)
        return m0

    for h, m0 in enumerate(pipelined(own_a, own_b)):
        m_ref[h] = m0

    n_groups = (i + group - 1) // group
    last_group = n_blocks // group - 1

    def consume_and_prefetch(c, cur_ref, nxt_ref):
        first = c * group
        nxt_first = jnp.minimum(c + 1, last_group) * group

        def next_scores(h):
            nxt_ref[h] = scores(h, nxt_first, group)

        def past_b(h, _):
            m_prev = m_ref[h]
            s = cur_ref[h]
            chunks = [s[g * blk_sz:(g + 1) * blk_sz] for g in range(group)]
            brows = [bias_ref[h, pl.ds(first + g, 1), :] for g in range(group)]
            m_new = m_prev
            for sg, brow in zip(chunks, brows):
                m_new = jnp.maximum(m_new, jnp.max(sg, axis=0, keepdims=True) + brow)
            acc_new = jnp.exp2(m_prev - m_new) * acc_ref[h]
            for g, (sg, brow) in enumerate(zip(chunks, brows)):
                pg = jnp.exp2(sg - (m_new - brow))
                acc_new = acc_new + jnp.dot(vt_ref[0, h, first + g], pg.astype(BF16),
                                            preferred_element_type=F32)
            acc_ref[h] = acc_new
            m_ref[h] = m_new

        pipelined(next_scores, past_b, depth=loop_lookahead)

    def body(c, carry):
        for parity in range(2):
            @pl.when(c % 2 == parity)
            def _(parity=parity):
                consume_and_prefetch(c, stage[parity], stage[1 - parity])
        return carry

    lax.fori_loop(0, n_groups, body, 0)
    for h in range(heads):
        acc_fin = acc_ref[h]
        o_ref[0, h] = (acc_fin[:HEAD_DIM] / acc_fin[HEAD_DIM:HEAD_DIM + 1]).T


def _moba_call(qkv_heads, vt, f32_weights, *, n_heads_a, heads_per_step=8):
    B, _, S, _ = qkv_heads.shape
    n_blocks = S // MOBA_BLOCK
    tq = MOBA_BLOCK
    hps = heads_per_step
    assert n_heads_a % hps == 0
    n_groups = n_heads_a // hps
    k_first = n_groups
    grid = (B, n_groups, S // tq)
    n_steps = grid[0] * grid[1] * grid[2]

    def slab_index(b, h, i):
        return ((b * grid[1] + h) * grid[2] + i, 0, 0)

    slabs, slab_specs, slab_shapes = [], [], []
    for w in f32_weights:
        rows, cols = w.shape
        assert rows % (n_steps * BF16_SUBLANE_TILE) == 0
        slabs.append(w.reshape(n_steps, rows // n_steps, cols))
        slab_specs.append(pl.BlockSpec((1, rows // n_steps, cols), slab_index))
        slab_shapes.append(jax.ShapeDtypeStruct((n_steps, rows // n_steps, cols), BF16))

    kern = functools.partial(_moba_kernel, n_blocks=n_blocks, heads=hps, n_cast=len(slabs))
    outs = pl.pallas_call(
        kern,
        grid=grid,
        in_specs=[
            pl.BlockSpec((1, hps, tq, HEAD_DIM), lambda b, h, i: (b, h, i, 0)),
            pl.BlockSpec((1, hps, S, HEAD_DIM), lambda b, h, i: (b, k_first + h, 0, 0)),
            pl.BlockSpec((1, hps, n_blocks, MOBA_VT_ROWS, MOBA_BLOCK),
                         lambda b, h, i: (b, h, 0, 0, 0), pipeline_mode=pl.Buffered(1)),
        ] + slab_specs,
        out_specs=[pl.BlockSpec((1, hps, tq, HEAD_DIM), lambda b, h, i: (b, h, i, 0))] + slab_specs,
        out_shape=[jax.ShapeDtypeStruct((B, n_heads_a, S, HEAD_DIM), F32)] + slab_shapes,
        scratch_shapes=[pltpu.VMEM((hps, n_blocks, HEAD_DIM), F32),
                        pltpu.VMEM((hps, n_blocks, tq), F32),
                        pltpu.VMEM((hps, MOBA_VT_ROWS, tq), F32),
                        pltpu.VMEM((hps, 1, tq), F32),
                        pltpu.VMEM((hps, MOBA_LOOP_GROUP * MOBA_BLOCK, tq), F32),
                        pltpu.VMEM((hps, MOBA_LOOP_GROUP * MOBA_BLOCK, tq), F32)],
        compiler_params=pltpu.CompilerParams(
            dimension_semantics=("parallel", "parallel", "arbitrary"),
            vmem_limit_bytes=V7X_VMEM_LIMIT_BYTES),
        name="moba_attn",
    )(qkv_heads, qkv_heads, vt, *slabs)
    return outs[0], [o.reshape(w.shape) for o, w in zip(outs[1:], f32_weights)]


def _dilated_kernel(q1_ref, k1_ref, v1_ref, q4_ref, k4_ref, v4_ref, q16_ref, k16_ref, v16_ref,
                    ob_ref, o_scr, lse_scr, bias_scr, *, seq_len, blocks_per_iter, combine_rows):
    blk = DIL_BLOCK
    branch_refs = ((q1_ref, k1_ref, v1_ref), (q4_ref, k4_ref, v4_ref), (q16_ref, k16_ref, v16_ref))

    qi = lax.broadcasted_iota(jnp.int32, (blk, 2 * blk), 0)
    ki = lax.broadcasted_iota(jnp.int32, (blk, 2 * blk), 1)
    dist = qi + blk - ki
    bias_scr[0] = jnp.where(jnp.logical_and(dist >= 0, dist <= blk), 0.0, NEG).astype(F32)
    bias_scr[1] = jnp.where(ki <= qi, 0.0, NEG).astype(F32)

    for g, (window, d) in enumerate(DIL_PAIRS):
        assert window // d == DIL_BLOCK
        q_ref, k_ref, v_ref = branch_refs[g]
        n_blk = seq_len // d // blk

        def rows_of(ref, r, start, size, d=d):
            if d == 1:
                return ref[0, 0, pl.ds(start, size), :]
            return ref[0, 0, r, pl.ds(start, size), :]

        def key_start(n):
            return pl.multiple_of(jnp.maximum(n - 1, 0) * blk, blk)

        def scores(r, n, q_ref=q_ref, k_ref=k_ref, rows_of=rows_of):
            qb = rows_of(q_ref, r, pl.multiple_of(n * blk, blk), blk)
            kb = rows_of(k_ref, r, key_start(n), 2 * blk)
            return lax.dot_general(qb, kb, NT_DIMS, preferred_element_type=F32)

        def finish(r, n, s, g=g, d=d, v_ref=v_ref, rows_of=rows_of):
            t = s + bias_scr[jnp.where(n == 0, 1, 0)]
            m = jnp.max(t, axis=-1, keepdims=True)
            p = jnp.exp2(t - m)
            vb = rows_of(v_ref, r, key_start(n), 2 * blk)
            v_ones = jnp.concatenate([vb, jnp.ones_like(vb)], axis=1)
            pv = jnp.dot(p.astype(BF16), v_ones, preferred_element_type=F32)
            den = pv[:, HEAD_DIM:]
            q_start = pl.multiple_of(n * blk, blk)
            if d == 1:
                rows = pl.ds(q_start, blk)
            elif d == 4:
                rows = pl.ds(r * (seq_len // 4) + q_start, blk)
            else:
                r4, m4 = r % 4, r // 4
                rows = pl.ds(r4 * (seq_len // 4) + 4 * q_start + m4, blk, stride=4)
            o_scr[g, rows, :] = pv[:, :HEAD_DIM] / den
            lse_scr[g, rows, :] = m + jnp.log2(den)

        def run_blocks(tasks, scores=scores, finish=finish):
            nxt = scores(*tasks[0])
            for idx, (r, n) in enumerate(tasks):
                cur = nxt
                if idx + 1 < len(tasks):
                    nxt = scores(*tasks[idx + 1])
                finish(r, n, cur)

        n_per_iter = max(1, min(blocks_per_iter // d, n_blk))

        def body(it, carry, run_blocks=run_blocks, d=d, n_per_iter=n_per_iter):
            run_blocks([(r, it * n_per_iter + u) for u in range(n_per_iter) for r in range(d)])
            return carry

        lax.fori_loop(0, n_blk // n_per_iter, body, 0)

    def combine(c, carry):
        i0 = pl.multiple_of(c * combine_rows, combine_rows)
        for r4 in range(4):
            tok = pl.ds(4 * i0 + r4, combine_rows, stride=4)
            cls = pl.ds(r4 * (seq_len // 4) + i0, combine_rows)
            l1, l2, l3 = lse_scr[0, tok, :], lse_scr[1, cls, :], lse_scr[2, cls, :]
            lmax = jnp.maximum(jnp.maximum(l1, l2), l3)
            e1, e2, e3 = jnp.exp2(l1 - lmax), jnp.exp2(l2 - lmax), jnp.exp2(l3 - lmax)
            num = e1 * o_scr[0, tok, :] + e2 * o_scr[1, cls, :] + e3 * o_scr[2, cls, :]
            ob_ref[0, 0, tok, :] = num / (e1 + e2 + e3)
        return carry

    lax.fori_loop(0, seq_len // 4 // combine_rows, combine, 0)


def _dilated_call(qkv_heads, d4, d16, *, n_heads_b, q_off, k_off, v_off, blocks_per_iter=32):
    B, _, S, _ = qkv_heads.shape
    offs_nat = (q_off, k_off, v_off)
    offs_dil = (0, n_heads_b, 2 * n_heads_b)
    in_arrays = [qkv_heads] * 3 + [d4] * 3 + [d16] * 3
    in_specs = (
        [pl.BlockSpec((1, 1, S, HEAD_DIM), lambda b, h, off=off: (b, off + h, 0, 0))
         for off in offs_nat]
        + [pl.BlockSpec((1, 1, 4, S // 4, HEAD_DIM), lambda b, h, off=off: (b, off + h, 0, 0, 0))
           for off in offs_dil]
        + [pl.BlockSpec((1, 1, 16, S // 16, HEAD_DIM), lambda b, h, off=off: (b, off + h, 0, 0, 0))
           for off in offs_dil])
    n_br = len(DIL_PAIRS)
    return pl.pallas_call(
        functools.partial(_dilated_kernel, seq_len=S, blocks_per_iter=blocks_per_iter,
                          combine_rows=64),
        grid=(B, n_heads_b),
        in_specs=in_specs,
        out_specs=pl.BlockSpec((1, 1, S, HEAD_DIM), lambda b, h: (b, h, 0, 0)),
        out_shape=jax.ShapeDtypeStruct((B, n_heads_b, S, HEAD_DIM), F32),
        scratch_shapes=[pltpu.VMEM((n_br, S, HEAD_DIM), F32),
                        pltpu.VMEM((n_br, S, HEAD_DIM), F32),
                        pltpu.VMEM((2, DIL_BLOCK, 2 * DIL_BLOCK), F32)],
        compiler_params=pltpu.CompilerParams(
            dimension_semantics=("parallel", "parallel"),
            vmem_limit_bytes=V7X_VMEM_LIMIT_BYTES),
        name="dilated_attn",
    )(*in_arrays)


def _attnout_kernel(x_ref, oa_ref, ob_ref, ga_ref, gb_ref, wo_ref, pg_ref, fg_ref,
                    out_ref, hn_ref, mix_ref, *, row_chunks):
    tm = x_ref.shape[1]
    rows_per_chunk = tm // row_chunks
    for c in range(row_chunks):
        rows = slice(c * rows_per_chunk, (c + 1) * rows_per_chunk)
        col = 0
        for o_ref, g_ref in ((oa_ref, ga_ref), (ob_ref, gb_ref)):
            n_heads = o_ref.shape[1]
            ssq = None
            for h in range(n_heads):
                t = o_ref[0, h, rows, :]
                part = jnp.sum(t * t, axis=-1, keepdims=True)
                ssq = part if ssq is None else ssq + part
            inv = lax.rsqrt(ssq / (n_heads * HEAD_DIM) + RMS_EPS)
            for h in range(n_heads):
                gs = slice(h * HEAD_DIM, (h + 1) * HEAD_DIM)
                mix_ref[rows, col:col + HEAD_DIM] = (
                    o_ref[0, h, rows, :] * inv * g_ref[:, gs]).astype(BF16)
                col += HEAD_DIM
        y = jnp.dot(mix_ref[rows, :], wo_ref[...], preferred_element_type=F32)
        x_new = x_ref[0, rows, :] + y * _rms_scale(y) * pg_ref[...]
        out_ref[0, rows, :] = x_new
        hn_ref[0, rows, :] = (x_new * _rms_scale(x_new) * fg_ref[...]).astype(BF16)


def _attnout_call(x, oa, ob, ga, gb, wo_bf16, pg, ffn_g, *, tm=512, row_chunks=4):
    B, S, D = x.shape
    head_spec_a = pl.BlockSpec((1, oa.shape[1], tm, HEAD_DIM), lambda b, i: (b, 0, i, 0))
    head_spec_b = pl.BlockSpec((1, ob.shape[1], tm, HEAD_DIM), lambda b, i: (b, 0, i, 0))
    row_spec = pl.BlockSpec((1, tm, D), lambda b, i: (b, i, 0))
    return pl.pallas_call(
        functools.partial(_attnout_kernel, row_chunks=row_chunks),
        grid=(B, S // tm),
        in_specs=[
            row_spec,
            head_spec_a, head_spec_b,
            pl.BlockSpec((1, ga.shape[1]), lambda b, i: (0, 0)),
            pl.BlockSpec((1, gb.shape[1]), lambda b, i: (0, 0)),
            pl.BlockSpec(wo_bf16.shape, lambda b, i: (0, 0)),
            pl.BlockSpec((1, D), lambda b, i: (0, 0)),
            pl.BlockSpec((1, D), lambda b, i: (0, 0)),
        ],
        out_specs=[row_spec, row_spec],
        out_shape=[jax.ShapeDtypeStruct((B, S, D), F32), jax.ShapeDtypeStruct((B, S, D), BF16)],
        scratch_shapes=[pltpu.VMEM((tm, wo_bf16.shape[0]), BF16)],
        compiler_params=pltpu.CompilerParams(
            dimension_semantics=("parallel", "parallel"),
            vmem_limit_bytes=V7X_VMEM_LIMIT_BYTES),
        name="attn_out",
    )(x, oa, ob, ga, gb, wo_bf16, pg, ffn_g)


def _ffn_kernel(x_ref, hn_ref, wg_ref, wu_ref, cw_ref, cb_ref, wd_ref, pg_ref,
                out_ref, tail_ref, act_ref):
    i = pl.program_id(1)
    f = pl.program_id(2)
    tm = x_ref.shape[1]
    halo = tail_ref.shape[1]

    @pl.when(i == 0)
    def _():
        tail_ref[f] = jnp.zeros(tail_ref.shape[1:], F32)

    def step(kind):
        gate = jnp.dot(hn_ref[0], wg_ref[...], preferred_element_type=F32)
        up = jnp.dot(hn_ref[0], wu_ref[...], preferred_element_type=F32)
        cw = cw_ref[...]
        cb = cb_ref[...]

        def gated(g_m2, g_m1, g_0, u):
            gc = cb + g_m2 * cw[0:1, :] + g_m1 * cw[1:2, :] + g_0 * cw[2:3, :]
            gelu = 0.5 * gc * (1.0 + jnp.tanh(0.7978845608028654 * (gc + 0.044715 * (gc * gc * gc))))
            return (gelu * u).astype(BF16)

        act_ref[...] = gated(pltpu.roll(gate, 2, 0), pltpu.roll(gate, 1, 0), gate, up)
        head = BF16_SUBLANE_TILE
        ext = jnp.concatenate([tail_ref[f], gate[0:head]], axis=0)
        act_ref[0:head, :] = gated(ext[halo - 2:halo - 2 + head], ext[halo - 1:halo - 1 + head],
                                   gate[0:head], up[0:head])
        tail_ref[f] = gate[tm - halo:, :]
        if kind == "first":
            out_ref[0] = jnp.dot(act_ref[...], wd_ref[...], preferred_element_type=F32)
        elif kind == "middle":
            out_ref[0] += jnp.dot(act_ref[...], wd_ref[...], preferred_element_type=F32)
        else:
            half = tm // FFN_LAST_STEP_ROW_CHUNKS
            for r0 in range(0, tm, half):
                rows = slice(r0, r0 + half)
                y = out_ref[0, rows, :] + jnp.dot(act_ref[rows, :], wd_ref[...],
                                                  preferred_element_type=F32)
                out_ref[0, rows, :] = x_ref[0, rows, :] + y * _rms_scale(y) * pg_ref[...]

    last = pl.num_programs(2) - 1
    pl.when(f == 0)(lambda: step("first"))
    pl.when(jnp.logical_and(f > 0, f < last))(lambda: step("middle"))
    pl.when(f == last)(lambda: step("last"))


def _ffn_call(x, hn, wg_bf16, wu_bf16, conv_w, conv_b, wd_bf16, pg, *, tm=512, tf=1024):
    B, S, D = x.shape
    d_ff = wg_bf16.shape[1]
    halo = F32_SUBLANE_TILE
    assert CONV_WIDTH - 1 <= halo and d_ff // tf >= 2
    return pl.pallas_call(
        _ffn_kernel,
        grid=(B, S // tm, d_ff // tf),
        in_specs=[
            pl.BlockSpec((1, tm, D), lambda b, i, f: (b, i, 0)),
            pl.BlockSpec((1, tm, D), lambda b, i, f: (b, i, 0)),
            pl.BlockSpec((D, tf), lambda b, i, f: (0, f)),
            pl.BlockSpec((D, tf), lambda b, i, f: (0, f)),
            pl.BlockSpec((CONV_WIDTH, tf), lambda b, i, f: (0, f)),
            pl.BlockSpec((1, tf), lambda b, i, f: (0, f)),
            pl.BlockSpec((tf, D), lambda b, i, f: (f, 0)),
            pl.BlockSpec((1, D), lambda b, i, f: (0, 0)),
        ],
        out_specs=pl.BlockSpec((1, tm, D), lambda b, i, f: (b, i, 0)),
        out_shape=jax.ShapeDtypeStruct((B, S, D), F32),
        scratch_shapes=[pltpu.VMEM((d_ff // tf, halo, tf), F32), pltpu.VMEM((tm, tf), BF16)],
        compiler_params=pltpu.CompilerParams(
            dimension_semantics=("parallel", "arbitrary", "arbitrary"),
            vmem_limit_bytes=V7X_VMEM_LIMIT_BYTES),
        name="conv_glu_ffn",
    )(x, hn, wg_bf16, wu_bf16, conv_w, conv_b, wd_bf16, pg)


def kernel(x, positions, attn_pre_g, w_qkv, moba_out_g, dil_out_g, w_o, attn_post_g, ffn_pre_g,
           w_gate, w_up, conv_w, conv_b, w_down, ffn_post_g):
    depth = w_qkv.shape[0]
    n_heads_a = moba_out_g.shape[1] // HEAD_DIM
    n_heads_b = dil_out_g.shape[1] // HEAD_DIM
    for l in range(depth):
        qkv_heads, vt, d4, d16 = _qkv_call(x, attn_pre_g[l][None], w_qkv[l].astype(BF16),
                                           positions, n_heads_a=n_heads_a, n_heads_b=n_heads_b)
        oa, (wo_bf16, wg_bf16, wu_bf16, wd_bf16) = _moba_call(
            qkv_heads, vt, [w_o[l], w_gate[l], w_up[l], w_down[l]], n_heads_a=n_heads_a)
        ob = _dilated_call(qkv_heads, d4, d16, n_heads_b=n_heads_b, q_off=3 * n_heads_a,
                           k_off=3 * n_heads_a + n_heads_b, v_off=3 * n_heads_a + 2 * n_heads_b)
        x, hn = _attnout_call(x, oa, ob, moba_out_g[l][None], dil_out_g[l][None],
                              wo_bf16, attn_post_g[l][None], ffn_pre_g[l][None])
        x = _ffn_call(x, hn, wg_bf16, wu_bf16, conv_w[l], conv_b[l][None], wd_bf16,
                      ffn_post_g[l][None])
    return x
```

```python
import functools

import jax
import jax.numpy as jnp
from jax import lax
from jax.experimental import pallas as pl
from jax.experimental.pallas import tpu as pltpu

F32 = jnp.float32
BF16 = jnp.bfloat16

HEAD_DIM = 128
ROT_DIM = HEAD_DIM // 4
ROT_HALF = ROT_DIM // 2
ROPE_THETA = 500000.0
MOBA_BLOCK = 256
MOBA_TOPK = 3
DIL_PAIRS = ((128, 1), (512, 4), (2048, 16))
DIL_BLOCK = 128
CONV_WIDTH = 3
RMS_EPS = 1e-6
SCALE = HEAD_DIM ** -0.5
SCALE_LOG2E = SCALE * 1.4426950408889634
NEG = -1e30

V7X_VMEM_LIMIT_BYTES = 56 * 1024 * 1024
BF16_SUBLANE_TILE = 16
V7X_MXU_WIDTH = 256
F32_SUBLANE_TILE = 8
FFN_LAST_STEP_ROW_CHUNKS = 2
QKV_SLAB_SLOTS = 2
MOBA_VT_ROWS = HEAD_DIM + BF16_SUBLANE_TILE
MOBA_LOOP_GROUP = 2

NT_DIMS = (((1,), (1,)), ((), ()))


def _rms_scale(x):
    return lax.rsqrt(jnp.mean(x * x, axis=-1, keepdims=True) + RMS_EPS)


def _qkv_kernel(x_ref, g_ref, w_ref, pos_ref, freq_ref,
                nat_ref, vt_ref, d4_ref, d16_ref, hn_ref, slab_ref, slab4_ref, *, heads_per_seg):
    tm = hn_ref.shape[0]
    heads_per_dot = V7X_MXU_WIDTH // HEAD_DIM
    dot_width = heads_per_dot * HEAD_DIM
    seg_width = heads_per_seg * HEAD_DIM

    x = x_ref[0]
    hn_ref[...] = (x * _rms_scale(x) * g_ref[...]).astype(BF16)

    ang = pos_ref[0].astype(F32) * freq_ref[...]
    cos_t, sin_t = jnp.cos(ang), jnp.sin(ang)
    lane = lax.broadcasted_iota(jnp.int32, ang.shape, 1)
    sa_t = jnp.where(lane < ROT_HALF, -sin_t, 0.0)
    sb_t = jnp.where(lane >= ROT_HALF, sin_t, 0.0)

    def make_rope(scale):
        cos, sa, sb = cos_t * scale, sa_t * scale, sb_t * scale

        def rope(t):
            return (t * cos + pltpu.roll(t, HEAD_DIM - ROT_HALF, 1) * sa
                    + pltpu.roll(t, ROT_HALF, 1) * sb)
        return rope

    rope_q, rope_k = make_rope(SCALE_LOG2E), make_rope(1.0)

    for seg in (2, 0, 1, 3, 4, 5):
        rope = {0: rope_q, 1: rope_k, 3: rope_q, 4: rope_k}.get(seg)
        with_vt = seg == 2
        with_dilated = seg >= 3
        for c in range(heads_per_seg // heads_per_dot):
            col = seg * seg_width + c * dot_width
            acc = jnp.dot(hn_ref[...], w_ref[:, col:col + dot_width], preferred_element_type=F32)
            for hh in range(heads_per_dot):
                h = c * heads_per_dot + hh
                t = acc[:, hh * HEAD_DIM:(hh + 1) * HEAD_DIM]
                if rope is not None:
                    t = rope(t)
                nat_ref[0, seg * heads_per_seg + h] = t.astype(BF16)
                if with_vt:
                    for blk in range(tm // MOBA_BLOCK):
                        vt_ref[0, h, blk, 0:HEAD_DIM, :] = (
                            t[blk * MOBA_BLOCK:(blk + 1) * MOBA_BLOCK].T.astype(BF16))
                        vt_ref[0, h, blk, HEAD_DIM:, :] = jnp.ones(
                            (MOBA_VT_ROWS - HEAD_DIM, MOBA_BLOCK), BF16)
                if with_dilated:
                    hd = (seg - 3) * heads_per_seg + h
                    slot = h % slab_ref.shape[0]
                    q4 = tm // 4
                    slab_ref[slot] = t
                    for r4 in range(4):
                        cls = slab_ref[slot, pl.ds(r4, q4, stride=4), :]
                        d4_ref[0, hd, r4] = cls.astype(BF16)
                        slab4_ref[slot, r4 * q4:(r4 + 1) * q4, :] = cls
                    for r4 in range(4):
                        for m in range(4):
                            cls = slab4_ref[slot, pl.ds(r4 * q4 + m, q4 // 4, stride=4), :]
                            d16_ref[0, hd, r4 + 4 * m] = cls.astype(BF16)


def _qkv_call(x, g, w_bf16, positions, *, n_heads_a, n_heads_b, tm=256):
    B, S, D = x.shape
    half_freq = ROPE_THETA ** (-jnp.arange(ROT_HALF, dtype=F32) / ROT_HALF)
    lane_freq = jnp.concatenate(
        [half_freq, half_freq, jnp.zeros((HEAD_DIM - ROT_DIM,), F32)])[None]
    pos_col = positions.reshape(B, S, 1)
    N = w_bf16.shape[1]
    assert n_heads_a == n_heads_b and N == 3 * (n_heads_a + n_heads_b) * HEAD_DIM
    assert tm % MOBA_BLOCK == 0 and (tm // 16) % BF16_SUBLANE_TILE == 0
    n_col_heads = N // HEAD_DIM
    kern = functools.partial(_qkv_kernel, heads_per_seg=n_heads_a)
    return pl.pallas_call(
        kern,
        grid=(B, S // tm),
        in_specs=[
            pl.BlockSpec((1, tm, D), lambda b, i: (b, i, 0)),
            pl.BlockSpec((1, D), lambda b, i: (0, 0)),
            pl.BlockSpec((D, N), lambda b, i: (0, 0), pipeline_mode=pl.Buffered(1)),
            pl.BlockSpec((1, tm, 1), lambda b, i: (b, i, 0)),
            pl.BlockSpec((1, HEAD_DIM), lambda b, i: (0, 0)),
        ],
        out_specs=[
            pl.BlockSpec((1, n_col_heads, tm, HEAD_DIM), lambda b, i: (b, 0, i, 0)),
            pl.BlockSpec((1, n_heads_a, tm // MOBA_BLOCK, MOBA_VT_ROWS, MOBA_BLOCK),
                         lambda b, i: (b, 0, i, 0, 0)),
            pl.BlockSpec((1, 3 * n_heads_b, 4, tm // 4, HEAD_DIM), lambda b, i: (b, 0, 0, i, 0)),
            pl.BlockSpec((1, 3 * n_heads_b, 16, tm // 16, HEAD_DIM), lambda b, i: (b, 0, 0, i, 0)),
        ],
        out_shape=[
            jax.ShapeDtypeStruct((B, n_col_heads, S, HEAD_DIM), BF16),
            jax.ShapeDtypeStruct((B, n_heads_a, S // MOBA_BLOCK, MOBA_VT_ROWS, MOBA_BLOCK), BF16),
            jax.ShapeDtypeStruct((B, 3 * n_heads_b, 4, S // 4, HEAD_DIM), BF16),
            jax.ShapeDtypeStruct((B, 3 * n_heads_b, 16, S // 16, HEAD_DIM), BF16),
        ],
        scratch_shapes=[pltpu.VMEM((tm, D), BF16),
                        pltpu.VMEM((QKV_SLAB_SLOTS, tm, HEAD_DIM), F32),
                        pltpu.VMEM((QKV_SLAB_SLOTS, tm, HEAD_DIM), F32)],
        compiler_params=pltpu.CompilerParams(
            dimension_semantics=("parallel", "parallel"),
            vmem_limit_bytes=V7X_VMEM_LIMIT_BYTES),
        name="qkv_rope",
    )(x, g, w_bf16, pos_col, lane_freq)


def _moba_kernel(*refs, n_blocks, heads, n_cast, lookahead=8, loop_lookahead=2,
                 group=MOBA_LOOP_GROUP):
    q_ref, k_ref, vt_ref = refs[:3]
    cast_in = refs[3:3 + n_cast]
    o_ref = refs[3 + n_cast]
    cast_out = refs[4 + n_cast:4 + 2 * n_cast]
    kmean_ref, kmhl_ref, bias_ref, acc_ref, m_ref, sa_ref, sb_ref = refs[4 + 2 * n_cast:]
    assert n_blocks % group == 0
    i = pl.program_id(2)
    blk_sz = MOBA_BLOCK

    for w_in, w_out in zip(cast_in, cast_out):
        w_out[0] = w_in[0].astype(BF16)

    @pl.when(i == 0)
    def _():
        for h in range(heads):
            for blk in range(n_blocks):
                kb = k_ref[0, h, blk * blk_sz:(blk + 1) * blk_sz, :].astype(F32)
                kmean_ref[h, blk:blk + 1, :] = jnp.mean(kb, axis=0, keepdims=True)
            km = kmean_ref[h]
            km_hi = km.astype(BF16)
            kmhl_ref[h, 0:n_blocks, :] = km_hi
            kmhl_ref[h, n_blocks:, :] = (km - km_hi.astype(F32)).astype(BF16)

    def select_blocks(h, gate_terms):
        gate = gate_terms[0:n_blocks] + gate_terms[n_blocks:]
        blk_id = lax.broadcasted_iota(jnp.int32, gate.shape, 0).astype(F32)
        neg_inf = jnp.float32(-jnp.inf)
        g = jnp.where(blk_id < i.astype(F32), gate, neg_inf)
        sel = jnp.zeros(gate.shape, dtype=jnp.bool_)
        for _ in range(MOBA_TOPK):
            m = jnp.max(g, axis=0, keepdims=True)
            first = jnp.min(jnp.where(g == m, blk_id, float(n_blocks)), axis=0, keepdims=True)
            pick = jnp.logical_and(blk_id == first, m > neg_inf)
            sel = jnp.logical_or(sel, pick)
            g = jnp.where(pick, neg_inf, g)
        bias_ref[h] = jnp.where(sel, 0.0, NEG).astype(F32)

    def scores(h, first_blk, n_blk):
        rows = n_blk * blk_sz
        kb = k_ref[0, h, pl.ds(pl.multiple_of(first_blk * blk_sz, blk_sz), rows), :]
        return lax.dot_general(kb, q_ref[0, h], NT_DIMS, preferred_element_type=F32)

    def pipelined(stage_a, stage_b, depth=lookahead):
        ahead = [stage_a(h) for h in range(min(depth, heads))]
        outs = []
        for h in range(heads):
            if h + depth < heads:
                ahead.append(stage_a(h + depth))
            outs.append(stage_b(h, ahead[h]))
        return outs

    stage = (sa_ref, sb_ref)

    def own_a(h):
        kd = k_ref[0, h, pl.ds(pl.multiple_of(i * blk_sz, blk_sz), blk_sz), :]
        both = lax.dot_general(jnp.concatenate([kd, kmhl_ref[h]], axis=0), q_ref[0, h], NT_DIMS,
                               preferred_element_type=F32)
        select_blocks(h, both[blk_sz:])
        stage[0][h] = scores(h, 0, group)
        return both[0:blk_sz]

    def own_b(h, s):
        key_pos = lax.broadcasted_iota(jnp.int32, s.shape, 0)
        q_pos = lax.broadcasted_iota(jnp.int32, s.shape, 1)
        t = jnp.where(key_pos <= q_pos, s, NEG)
        m0 = jnp.max(t, axis=0, keepdims=True)
        p = jnp.exp2(t - m0)
        acc_ref[h] = jnp.dot(vt_ref[0, h, i], p.astype(BF16), preferred_element_type=F32)
        return m0

    for h, m0 in enumerate(pipelined(own_a, own_b)):
        m_ref[h] = m0

    n_groups = (i + group - 1) // group
    last_group = n_blocks // group - 1

    def consume_and_prefetch(c, cur_ref, nxt_ref):
        first = c * group
        nxt_first = jnp.minimum(c + 1, last_group) * group

        def next_scores(h):
            nxt_ref[h] = scores(h, nxt_first, group)

        def past_b(h, _):
            m_prev = m_ref[h]
            s = cur_ref[h]
            chunks = [s[g * blk_sz:(g + 1) * blk_sz] for g in range(group)]
            brows = [bias_ref[h, pl.ds(first + g, 1), :] for g in range(group)]
            m_new = m_prev
            for sg, brow in zip(chunks, brows):
                m_new = jnp.maximum(m_new, jnp.max(sg, axis=0, keepdims=True) + brow)
            acc_new = jnp.exp2(m_prev - m_new) * acc_ref[h]
            for g, (sg, brow) in enumerate(zip(chunks, brows)):
                pg = jnp.exp2(sg - (m_new - brow))
                acc_new = acc_new + jnp.dot(vt_ref[0, h, first + g], pg.astype(BF16),
                                            preferred_element_type=F32)
            acc_ref[h] = acc_new
            m_ref[h] = m_new

        pipelined(next_scores, past_b, depth=loop_lookahead)

    def body(c, carry):
        for parity in range(2):
            @pl.when(c % 2 == parity)
            def _(parity=parity):
                consume_and_prefetch(c, stage[parity], stage[1 - parity])
        return carry

    lax.fori_loop(0, n_groups, body, 0)
    for h in range(heads):
        acc_fin = acc_ref[h]
        o_ref[0, h] = (acc_fin[:HEAD_DIM] / acc_fin[HEAD_DIM:HEAD_DIM + 1]).T


def _moba_call(qkv_heads, vt, f32_weights, *, n_heads_a, heads_per_step=8):
    B, _, S, _ = qkv_heads.shape
    n_blocks = S // MOBA_BLOCK
    tq = MOBA_BLOCK
    hps = heads_per_step
    assert n_heads_a % hps == 0
    n_groups = n_heads_a // hps
    k_first = n_groups
    grid = (B, n_groups, S // tq)
    n_steps = grid[0] * grid[1] * grid[2]

    def slab_index(b, h, i):
        return ((b * grid[1] + h) * grid[2] + i, 0, 0)

    slabs, slab_specs, slab_shapes = [], [], []
    for w in f32_weights:
        rows, cols = w.shape
        assert rows % (n_steps * BF16_SUBLANE_TILE) == 0
        slabs.append(w.reshape(n_steps, rows // n_steps, cols))
        slab_specs.append(pl.BlockSpec((1, rows // n_steps, cols), slab_index))
        slab_shapes.append(jax.ShapeDtypeStruct((n_steps, rows // n_steps, cols), BF16))

    kern = functools.partial(_moba_kernel, n_blocks=n_blocks, heads=hps, n_cast=len(slabs))
    outs = pl.pallas_call(
        kern,
        grid=grid,
        in_specs=[
            pl.BlockSpec((1, hps, tq, HEAD_DIM), lambda b, h, i: (b, h, i, 0)),
            pl.BlockSpec((1, hps, S, HEAD_DIM), lambda b, h, i: (b, k_first + h, 0, 0)),
            pl.BlockSpec((1, hps, n_blocks, MOBA_VT_ROWS, MOBA_BLOCK),
                         lambda b, h, i: (b, h, 0, 0, 0), pipeline_mode=pl.Buffered(1)),
        ] + slab_specs,
        out_specs=[pl.BlockSpec((1, hps, tq, HEAD_DIM), lambda b, h, i: (b, h, i, 0))] + slab_specs,
        out_shape=[jax.ShapeDtypeStruct((B, n_heads_a, S, HEAD_DIM), F32)] + slab_shapes,
        scratch_shapes=[pltpu.VMEM((hps, n_blocks, HEAD_DIM), F32),
                        pltpu.VMEM((hps, 2 * n_blocks, HEAD_DIM), BF16),
                        pltpu.VMEM((hps, n_blocks, tq), F32),
                        pltpu.VMEM((hps, MOBA_VT_ROWS, tq), F32),
                        pltpu.VMEM((hps, 1, tq), F32),
                        pltpu.VMEM((hps, MOBA_LOOP_GROUP * MOBA_BLOCK, tq), F32),
                        pltpu.VMEM((hps, MOBA_LOOP_GROUP * MOBA_BLOCK, tq), F32)],
        compiler_params=pltpu.CompilerParams(
            dimension_semantics=("parallel", "parallel", "arbitrary"),
            vmem_limit_bytes=V7X_VMEM_LIMIT_BYTES),
        name="moba_attn",
    )(qkv_heads, qkv_heads, vt, *slabs)
    return outs[0], [o.reshape(w.shape) for o, w in zip(outs[1:], f32_weights)]


def _dilated_kernel(q1_ref, k1_ref, v1_ref, q4_ref, k4_ref, v4_ref, q16_ref, k16_ref, v16_ref,
                    ob_ref, o_scr, lse_scr, bias_scr, *, seq_len, blocks_per_iter, combine_rows):
    blk = DIL_BLOCK
    branch_refs = ((q1_ref, k1_ref, v1_ref), (q4_ref, k4_ref, v4_ref), (q16_ref, k16_ref, v16_ref))

    qi = lax.broadcasted_iota(jnp.int32, (blk, 2 * blk), 0)
    ki = lax.broadcasted_iota(jnp.int32, (blk, 2 * blk), 1)
    dist = qi + blk - ki
    bias_scr[0] = jnp.where(jnp.logical_and(dist >= 0, dist <= blk), 0.0, NEG).astype(F32)
    bias_scr[1] = jnp.where(ki <= qi, 0.0, NEG).astype(F32)

    for g, (window, d) in enumerate(DIL_PAIRS):
        assert window // d == DIL_BLOCK
        q_ref, k_ref, v_ref = branch_refs[g]
        n_blk = seq_len // d // blk

        def rows_of(ref, r, start, size, d=d):
            if d == 1:
                return ref[0, 0, pl.ds(start, size), :]
            return ref[0, 0, r, pl.ds(start, size), :]

        def key_start(n):
            return pl.multiple_of(jnp.maximum(n - 1, 0) * blk, blk)

        def scores(r, n, q_ref=q_ref, k_ref=k_ref, rows_of=rows_of):
            qb = rows_of(q_ref, r, pl.multiple_of(n * blk, blk), blk)
            kb = rows_of(k_ref, r, key_start(n), 2 * blk)
            return lax.dot_general(qb, kb, NT_DIMS, preferred_element_type=F32)

        def finish(r, n, s, g=g, d=d, v_ref=v_ref, rows_of=rows_of):
            t = s + bias_scr[jnp.where(n == 0, 1, 0)]
            m = jnp.max(t, axis=-1, keepdims=True)
            p = jnp.exp2(t - m)
            vb = rows_of(v_ref, r, key_start(n), 2 * blk)
            v_ones = jnp.concatenate([vb, jnp.ones_like(vb)], axis=1)
            pv = jnp.dot(p.astype(BF16), v_ones, preferred_element_type=F32)
            den = pv[:, HEAD_DIM:]
            q_start = pl.multiple_of(n * blk, blk)
            if d == 1:
                rows = pl.ds(q_start, blk)
            elif d == 4:
                rows = pl.ds(r * (seq_len // 4) + q_start, blk)
            else:
                r4, m4 = r % 4, r // 4
                rows = pl.ds(r4 * (seq_len // 4) + 4 * q_start + m4, blk, stride=4)
            o_scr[g, rows, :] = pv[:, :HEAD_DIM] / den
            lse_scr[g, rows, :] = m + jnp.log2(den)

        def run_blocks(tasks, scores=scores, finish=finish):
            nxt = scores(*tasks[0])
            for idx, (r, n) in enumerate(tasks):
                cur = nxt
                if idx + 1 < len(tasks):
                    nxt = scores(*tasks[idx + 1])
                finish(r, n, cur)

        n_per_iter = max(1, min(blocks_per_iter // d, n_blk))

        def body(it, carry, run_blocks=run_blocks, d=d, n_per_iter=n_per_iter):
            run_blocks([(r, it * n_per_iter + u) for u in range(n_per_iter) for r in range(d)])
            return carry

        lax.fori_loop(0, n_blk // n_per_iter, body, 0)

    def combine(c, carry):
        i0 = pl.multiple_of(c * combine_rows, combine_rows)
        for r4 in range(4):
            tok = pl.ds(4 * i0 + r4, combine_rows, stride=4)
            cls = pl.ds(r4 * (seq_len // 4) + i0, combine_rows)
            l1, l2, l3 = lse_scr[0, tok, :], lse_scr[1, cls, :], lse_scr[2, cls, :]
            lmax = jnp.maximum(jnp.maximum(l1, l2), l3)
            e1, e2, e3 = jnp.exp2(l1 - lmax), jnp.exp2(l2 - lmax), jnp.exp2(l3 - lmax)
            num = e1 * o_scr[0, tok, :] + e2 * o_scr[1, cls, :] + e3 * o_scr[2, cls, :]
            ob_ref[0, 0, tok, :] = num / (e1 + e2 + e3)
        return carry

    lax.fori_loop(0, seq_len // 4 // combine_rows, combine, 0)


def _dilated_call(qkv_heads, d4, d16, *, n_heads_b, q_off, k_off, v_off, blocks_per_iter=32):
    B, _, S, _ = qkv_heads.shape
    offs_nat = (q_off, k_off, v_off)
    offs_dil = (0, n_heads_b, 2 * n_heads_b)
    in_arrays = [qkv_heads] * 3 + [d4] * 3 + [d16] * 3
    in_specs = (
        [pl.BlockSpec((1, 1, S, HEAD_DIM), lambda b, h, off=off: (b, off + h, 0, 0))
         for off in offs_nat]
        + [pl.BlockSpec((1, 1, 4, S // 4, HEAD_DIM), lambda b, h, off=off: (b, off + h, 0, 0, 0))
           for off in offs_dil]
        + [pl.BlockSpec((1, 1, 16, S // 16, HEAD_DIM), lambda b, h, off=off: (b, off + h, 0, 0, 0))
           for off in offs_dil])
    n_br = len(DIL_PAIRS)
    return pl.pallas_call(
        functools.partial(_dilated_kernel, seq_len=S, blocks_per_iter=blocks_per_iter,
                          combine_rows=64),
        grid=(B, n_heads_b),
        in_specs=in_specs,
        out_specs=pl.BlockSpec((1, 1, S, HEAD_DIM), lambda b, h: (b, h, 0, 0)),
        out_shape=jax.ShapeDtypeStruct((B, n_heads_b, S, HEAD_DIM), F32),
        scratch_shapes=[pltpu.VMEM((n_br, S, HEAD_DIM), F32),
                        pltpu.VMEM((n_br, S, HEAD_DIM), F32),
                        pltpu.VMEM((2, DIL_BLOCK, 2 * DIL_BLOCK), F32)],
        compiler_params=pltpu.CompilerParams(
            dimension_semantics=("parallel", "parallel"),
            vmem_limit_bytes=V7X_VMEM_LIMIT_BYTES),
        name="dilated_attn",
    )(*in_arrays)


def _attnout_kernel(x_ref, oa_ref, ob_ref, ga_ref, gb_ref, wo_ref, pg_ref, fg_ref,
                    out_ref, hn_ref, mix_ref, *, row_chunks):
    tm = x_ref.shape[1]
    rows_per_chunk = tm // row_chunks
    for c in range(row_chunks):
        rows = slice(c * rows_per_chunk, (c + 1) * rows_per_chunk)
        col = 0
        for o_ref, g_ref in ((oa_ref, ga_ref), (ob_ref, gb_ref)):
            n_heads = o_ref.shape[1]
            ssq = None
            for h in range(n_heads):
                t = o_ref[0, h, rows, :]
                part = jnp.sum(t * t, axis=-1, keepdims=True)
                ssq = part if ssq is None else ssq + part
            inv = lax.rsqrt(ssq / (n_heads * HEAD_DIM) + RMS_EPS)
            for h in range(n_heads):
                gs = slice(h * HEAD_DIM, (h + 1) * HEAD_DIM)
                mix_ref[rows, col:col + HEAD_DIM] = (
                    o_ref[0, h, rows, :] * inv * g_ref[:, gs]).astype(BF16)
                col += HEAD_DIM
        y = jnp.dot(mix_ref[rows, :], wo_ref[...], preferred_element_type=F32)
        x_new = x_ref[0, rows, :] + y * _rms_scale(y) * pg_ref[...]
        out_ref[0, rows, :] = x_new
        hn_ref[0, rows, :] = (x_new * _rms_scale(x_new) * fg_ref[...]).astype(BF16)


def _attnout_call(x, oa, ob, ga, gb, wo_bf16, pg, ffn_g, *, tm=512, row_chunks=4):
    B, S, D = x.shape
    head_spec_a = pl.BlockSpec((1, oa.shape[1], tm, HEAD_DIM), lambda b, i: (b, 0, i, 0))
    head_spec_b = pl.BlockSpec((1, ob.shape[1], tm, HEAD_DIM), lambda b, i: (b, 0, i, 0))
    row_spec = pl.BlockSpec((1, tm, D), lambda b, i: (b, i, 0))
    return pl.pallas_call(
        functools.partial(_attnout_kernel, row_chunks=row_chunks),
        grid=(B, S // tm),
        in_specs=[
            row_spec,
            head_spec_a, head_spec_b,
            pl.BlockSpec((1, ga.shape[1]), lambda b, i: (0, 0)),
            pl.BlockSpec((1, gb.shape[1]), lambda b, i: (0, 0)),
            pl.BlockSpec(wo_bf16.shape, lambda b, i: (0, 0)),
            pl.BlockSpec((1, D), lambda b, i: (0, 0)),
            pl.BlockSpec((1, D), lambda b, i: (0, 0)),
        ],
        out_specs=[row_spec, row_spec],
        out_shape=[jax.ShapeDtypeStruct((B, S, D), F32), jax.ShapeDtypeStruct((B, S, D), BF16)],
        scratch_shapes=[pltpu.VMEM((tm, wo_bf16.shape[0]), BF16)],
        compiler_params=pltpu.CompilerParams(
            dimension_semantics=("parallel", "parallel"),
            vmem_limit_bytes=V7X_VMEM_LIMIT_BYTES),
        name="attn_out",
    )(x, oa, ob, ga, gb, wo_bf16, pg, ffn_g)


def _ffn_kernel(x_ref, hn_ref, wg_ref, wu_ref, cw_ref, cb_ref, wd_ref, pg_ref,
                out_ref, tail_ref, act_ref):
    i = pl.program_id(1)
    f = pl.program_id(2)
    tm = x_ref.shape[1]
    halo = tail_ref.shape[1]

    @pl.when(i == 0)
    def _():
        tail_ref[f] = jnp.zeros(tail_ref.shape[1:], F32)

    def step(kind):
        gate = jnp.dot(hn_ref[0], wg_ref[...], preferred_element_type=F32)
        up = jnp.dot(hn_ref[0], wu_ref[...], preferred_element_type=F32)
        cw = cw_ref[...]
        cb = cb_ref[...]

        def gated(g_m2, g_m1, g_0, u):
            gc = cb + g_m2 * cw[0:1, :] + g_m1 * cw[1:2, :] + g_0 * cw[2:3, :]
            gelu = 0.5 * gc * (1.0 + jnp.tanh(0.7978845608028654 * (gc + 0.044715 * (gc * gc * gc))))
            return (gelu * u).astype(BF16)

        act_ref[...] = gated(pltpu.roll(gate, 2, 0), pltpu.roll(gate, 1, 0), gate, up)
        head = BF16_SUBLANE_TILE
        ext = jnp.concatenate([tail_ref[f], gate[0:head]], axis=0)
        act_ref[0:head, :] = gated(ext[halo - 2:halo - 2 + head], ext[halo - 1:halo - 1 + head],
                                   gate[0:head], up[0:head])
        tail_ref[f] = gate[tm - halo:, :]
        if kind == "first":
            out_ref[0] = jnp.dot(act_ref[...], wd_ref[...], preferred_element_type=F32)
        elif kind == "middle":
            out_ref[0] += jnp.dot(act_ref[...], wd_ref[...], preferred_element_type=F32)
        else:
            half = tm // FFN_LAST_STEP_ROW_CHUNKS
            for r0 in range(0, tm, half):
                rows = slice(r0, r0 + half)
                y = out_ref[0, rows, :] + jnp.dot(act_ref[rows, :], wd_ref[...],
                                                  preferred_element_type=F32)
                out_ref[0, rows, :] = x_ref[0, rows, :] + y * _rms_scale(y) * pg_ref[...]

    last = pl.num_programs(2) - 1
    pl.when(f == 0)(lambda: step("first"))
    pl.when(jnp.logical_and(f > 0, f < last))(lambda: step("middle"))
    pl.when(f == last)(lambda: step("last"))


def _ffn_call(x, hn, wg_bf16, wu_bf16, conv_w, conv_b, wd_bf16, pg, *, tm=512, tf=1024):
    B, S, D = x.shape
    d_ff = wg_bf16.shape[1]
    halo = F32_SUBLANE_TILE
    assert CONV_WIDTH - 1 <= halo and d_ff // tf >= 2
    return pl.pallas_call(
        _ffn_kernel,
        grid=(B, S // tm, d_ff // tf),
        in_specs=[
            pl.BlockSpec((1, tm, D), lambda b, i, f: (b, i, 0)),
            pl.BlockSpec((1, tm, D), lambda b, i, f: (b, i, 0)),
            pl.BlockSpec((D, tf), lambda b, i, f: (0, f)),
            pl.BlockSpec((D, tf), lambda b, i, f: (0, f)),
            pl.BlockSpec((CONV_WIDTH, tf), lambda b, i, f: (0, f)),
            pl.BlockSpec((1, tf), lambda b, i, f: (0, f)),
            pl.BlockSpec((tf, D), lambda b, i, f: (f, 0)),
            pl.BlockSpec((1, D), lambda b, i, f: (0, 0)),
        ],
        out_specs=pl.BlockSpec((1, tm, D), lambda b, i, f: (b, i, 0)),
        out_shape=jax.ShapeDtypeStruct((B, S, D), F32),
        scratch_shapes=[pltpu.VMEM((d_ff // tf, halo, tf), F32), pltpu.VMEM((tm, tf), BF16)],
        compiler_params=pltpu.CompilerParams(
            dimension_semantics=("parallel", "arbitrary", "arbitrary"),
            vmem_limit_bytes=V7X_VMEM_LIMIT_BYTES),
        name="conv_glu_ffn",
    )(x, hn, wg_bf16, wu_bf16, conv_w, conv_b, wd_bf16, pg)


def kernel(x, positions, attn_pre_g, w_qkv, moba_out_g, dil_out_g, w_o, attn_post_g, ffn_pre_g,
           w_gate, w_up, conv_w, conv_b, w_down, ffn_post_g):
    depth = w_qkv.shape[0]
    n_heads_a = moba_out_g.shape[1] // HEAD_DIM
    n_heads_b = dil_out_g.shape[1] // HEAD_DIM
    for l in range(depth):
        qkv_heads, vt, d4, d16 = _qkv_call(x, attn_pre_g[l][None], w_qkv[l].astype(BF16),
                                           positions, n_heads_a=n_heads_a, n_heads_b=n_heads_b)
        oa, (wo_bf16, wg_bf16, wu_bf16, wd_bf16) = _moba_call(
            qkv_heads, vt, [w_o[l], w_gate[l], w_up[l], w_down[l]], n_heads_a=n_heads_a)
        ob = _dilated_call(qkv_heads, d4, d16, n_heads_b=n_heads_b, q_off=3 * n_heads_a,
                           k_off=3 * n_heads_a + n_heads_b, v_off=3 * n_heads_a + 2 * n_heads_b)
        x, hn = _attnout_call(x, oa, ob, moba_out_g[l][None], dil_out_g[l][None],
                              wo_bf16, attn_post_g[l][None], ffn_pre_g[l][None])
        x = _ffn_call(x, hn, wg_bf16, wu_bf16, conv_w[l], conv_b[l][None], wd_bf16,
                      ffn_post_g[l][None])
    return x
```
